```python
import math
import jax
import jax.numpy as jnp
from jax import lax
import numpy as np

D_MODEL = 1024
BATCH = 8
SEQ = 4096
DEPTH = 2

HEAD_DIM = 64
ATTN_WIDTH = 3 * D_MODEL // 8
ATTN_HEADS = ATTN_WIDTH // HEAD_DIM
ATTN_GROUPS = ((128, 1), (512, 4), (2048, 16))
ATTN_HEADS_PER_GROUP = ATTN_HEADS // len(ATTN_GROUPS)
RWKV_WIDTH = 3 * D_MODEL // 8
RWKV_HEADS = RWKV_WIDTH // HEAD_DIM
DECAY_LORA = 64
AAA_LORA = 64
GATE_LORA = 128
RWKV_IN = 3 * RWKV_WIDTH + DECAY_LORA + AAA_LORA + GATE_LORA
RWKV_SPLITS = (RWKV_WIDTH, 2 * RWKV_WIDTH, 3 * RWKV_WIDTH,
               3 * RWKV_WIDTH + DECAY_LORA, 3 * RWKV_WIDTH + DECAY_LORA + AAA_LORA)
GN_EPS = 64e-5
SSM_WIDTH = D_MODEL // 4
SSM_GROUP_CH = 16
SSM_GROUPS = SSM_WIDTH // SSM_GROUP_CH
SSM_STATE = 64
STEP_MIN = 1e-3
STEP_MAX = 1e-1
N_BRANCH = 3
ATTN_IN = 3 * ATTN_WIDTH
RWKV_OFF = ATTN_IN
SSM_OFF = RWKV_OFF + RWKV_IN
GATE_OFF = SSM_OFF + SSM_WIDTH
N_IN = GATE_OFF + N_BRANCH * D_MODEL
FFN_HIDDEN = ((8 * D_MODEL + 3 * 256 - 1) // (3 * 256)) * 256
NORM_EPS = 1e-6

kernel_name = "hybrid_dilated_rwkv7_s5_block"


def rmsnorm(z, gain):
    zf = z.astype(jnp.float32)
    y = zf * lax.rsqrt(jnp.mean(zf * zf, axis=-1, keepdims=True) + NORM_EPS) * gain.astype(jnp.float32)
    return y.astype(z.dtype)


def token_shift(z):
    return jnp.pad(z, ((0, 0), (1, 0), (0, 0)))[:, :-1]


def dilated_window_attention(q, k, v, window, dilation):
    bsz, s, h, e = q.shape
    n_back = window // dilation
    span = n_back * dilation
    s_pad = -(-s // span) * span
    nb = s_pad // span

    def blocks(z):
        z = jnp.pad(z, ((0, 0), (0, s_pad - s), (0, 0), (0, 0)))
        return z.reshape(bsz, nb, n_back, dilation, h, e)

    def with_prev(z):
        prev = jnp.pad(z, ((0, 0), (1, 0), (0, 0), (0, 0), (0, 0), (0, 0)))[:, :-1]
        return jnp.concatenate([prev, z], axis=2)

    qb = blocks(q).astype(jnp.float32)
    kb = with_prev(blocks(k)).astype(jnp.float32)
    vb = with_prev(blocks(v)).astype(jnp.float32)
    scores = jnp.einsum('bnqrhe,bnkrhe->bnrhqk', qb, kb) * (e ** -0.5)
    qi = jnp.arange(n_back)[:, None]
    kj = jnp.arange(2 * n_back)[None, :]
    band = (kj >= qi) & (kj <= qi + n_back)
    has_prev = (jnp.arange(nb) > 0)[:, None, None] | (jnp.arange(2 * n_back) >= n_back)[None, None, :]
    mask = band[None] & has_prev
    scores = jnp.where(mask[None, :, None, None], scores, -jnp.inf)
    mx = jnp.max(scores, axis=-1, keepdims=True)
    p = jnp.exp(scores - mx)
    den = jnp.sum(p, axis=-1, keepdims=True)
    out = jnp.einsum('bnrhqk,bnkrhe->bnqrhe', p / den, vb)
    lse = (mx + jnp.log(den))[..., 0]
    out = out.reshape(bsz, s_pad, h, e)[:, :s]
    lse = jnp.transpose(lse, (0, 1, 4, 2, 3)).reshape(bsz, s_pad, h)[:, :s]
    return out, lse


def dilated_mixture_attention(p_attn):
    bsz, s, _ = p_attn.shape
    q, k, v = (t.reshape(bsz, s, ATTN_HEADS, HEAD_DIM) for t in jnp.split(p_attn, 3, axis=-1))
    outs, lses = [], []
    for g, (window, dilation) in enumerate(ATTN_GROUPS):
        hs = slice(g * ATTN_HEADS_PER_GROUP, (g + 1) * ATTN_HEADS_PER_GROUP)
        o, lse = dilated_window_attention(q[:, :, hs], k[:, :, hs], v[:, :, hs], window, dilation)
        outs.append(o)
        lses.append(lse)
    alpha = jax.nn.softmax(jnp.stack(lses, axis=0), axis=0)
    o = jnp.concatenate([o_g * alpha[g][..., None] for g, o_g in enumerate(outs)], axis=2)
    return o.reshape(bsz, s, ATTN_WIDTH)


def wkv7_scan(r, w, k, v, a, b):
    bsz, _, h, e = r.shape

    def step(state, inp):
        r_t, w_t, k_t, v_t, a_t, b_t = inp
        sa = jnp.einsum('bhij,bhj->bhi', state, a_t)
        state = (state * w_t[:, :, None, :] + sa[..., None] * b_t[:, :, None, :]
                 + v_t[..., None] * k_t[:, :, None, :])
        return state, jnp.einsum('bhij,bhj->bhi', state, r_t)

    xs = tuple(jnp.moveaxis(t, 1, 0) for t in (r, w, k, v, a, b))
    _, y = lax.scan(step, jnp.zeros((bsz, h, e, e), jnp.float32), xs)
    return jnp.moveaxis(y, 0, 1)


def rwkv7_time_mix(p_rwkv, shift_mix, w0, w2, a0, a2, g2, k_k, k_a, r_k, ln_w, ln_b):
    bsz, s, _ = p_rwkv.shape
    z = p_rwkv.astype(jnp.float32)
    z = z + (token_shift(z) - z) * shift_mix
    r, k, v, xw, xa, xg = jnp.split(z, RWKV_SPLITS, axis=-1)
    w = -jax.nn.softplus(-(w0 + jnp.tanh(xw) @ w2)) - 0.5
    decay = jnp.exp(-jnp.exp(w))
    a = jax.nn.sigmoid(a0 + xa @ a2)
    g = jax.nn.sigmoid(xg) @ g2

    def heads(t):
        return t.reshape(bsz, s, RWKV_HEADS, HEAD_DIM)

    kk = heads(k * k_k)
    kk = kk / jnp.maximum(jnp.linalg.norm(kk, axis=-1, keepdims=True), 1e-12)
    k = k * (1.0 + (a - 1.0) * k_a)
    r_h, k_h, v_h, a_h = heads(r), heads(k), heads(v), heads(a)
    y = wkv7_scan(r_h, heads(decay), k_h, v_h, -kk, kk * a_h)
    mu = jnp.mean(y, axis=-1, keepdims=True)
    var = jnp.mean(jnp.square(y - mu), axis=-1, keepdims=True)
    y = ((y - mu) * lax.rsqrt(var + GN_EPS)).reshape(bsz, s, RWKV_WIDTH) * ln_w + ln_b
    bonus = jnp.sum(r_h * k_h * r_k, axis=-1, keepdims=True) * v_h
    return (y + bonus.reshape(bsz, s, RWKV_WIDTH)) * g


def _complex_linear_combine(e1, e2):
    a1r, a1i, b1r, b1i = e1
    a2r, a2i, b2r, b2i = e2
    return (a2r * a1r - a2i * a1i, a2r * a1i + a2i * a1r,
            a2r * b1r - a2i * b1i + b2r, a2r * b1i + a2i * b1r + b2i)


def s5_glu(p_ssm, a_re, a_im, log_step, b_re, b_im, c_re, c_im, d_skip, w_val, w_gate):
    bsz, s, _ = p_ssm.shape
    u = p_ssm.astype(jnp.float32).reshape(bsz, s, SSM_GROUPS, SSM_GROUP_CH)
    lam_re = a_re.astype(jnp.float32)
    lam_im = a_im.astype(jnp.float32)
    step = jnp.exp(log_step.astype(jnp.float32))[:, None]
    mag = jnp.exp(lam_re * step)
    ang = lam_im * step
    abar_re, abar_im = mag * jnp.cos(ang), mag * jnp.sin(ang)
    inv = 1.0 / (lam_re * lam_re + lam_im * lam_im)
    f_re = ((abar_re - 1.0) * lam_re + abar_im * lam_im) * inv
    f_im = (abar_im * lam_re - (abar_re - 1.0) * lam_im) * inv
    bbar_re = f_re[..., None] * b_re - f_im[..., None] * b_im
    bbar_im = f_re[..., None] * b_im + f_im[..., None] * b_re
    bu_re = jnp.einsum('bsgc,gpc->bsgp', u, bbar_re)
    bu_im = jnp.einsum('bsgc,gpc->bsgp', u, bbar_im)
    shape = bu_re.shape
    _, _, x_re, x_im = lax.associative_scan(
        _complex_linear_combine,
        (jnp.broadcast_to(abar_re, shape), jnp.broadcast_to(abar_im, shape), bu_re, bu_im),
        axis=1)
    y = (jnp.einsum('bsgp,gcp->bsgc', x_re, c_re) - jnp.einsum('bsgp,gcp->bsgc', x_im, c_im)
         + d_skip.reshape(SSM_GROUPS, SSM_GROUP_CH) * u)
    zg = jax.nn.gelu(y.reshape(bsz, s, SSM_WIDTH))
    return (zg @ w_val) * jax.nn.sigmoid(zg @ w_gate)


def swiglu(u, w_gate_up, w_down):
    a, b = jnp.split(u @ w_gate_up, 2, axis=-1)
    return (jax.nn.silu(a) * b) @ w_down


def setup_inputs(seed: int = 0) -> dict:
    key = jax.random.key(seed)
    ks = iter(jax.random.split(key, 40))
    f32 = jnp.float32

    def nrm(shape, scale):
        return jax.random.normal(next(ks), shape, f32) * scale

    def uni(shape, lo, hi):
        return jax.random.uniform(next(ks), shape, f32, lo, hi)

    L, D = DEPTH, D_MODEL
    G, P, C = SSM_GROUPS, SSM_STATE, SSM_GROUP_CH
    return {
        "x": nrm((BATCH, SEQ, D), 1.0),
        "norm_mix": 1.0 + nrm((L, D), 0.02),
        "w_in": nrm((L, D, N_IN), D ** -0.5),
        "rwkv_shift_mix": uni((L, RWKV_IN), 0.0, 1.0),
        "rwkv_w0": uni((L, RWKV_WIDTH), -6.0, -1.0),
        "rwkv_w2": nrm((L, DECAY_LORA, RWKV_WIDTH), 0.1 * DECAY_LORA ** -0.5),
        "rwkv_a0": nrm((L, RWKV_WIDTH), 0.1),
        "rwkv_a2": nrm((L, AAA_LORA, RWKV_WIDTH), 0.5 * AAA_LORA ** -0.5),
        "rwkv_g2": nrm((L, GATE_LORA, RWKV_WIDTH), GATE_LORA ** -0.5),
        "rwkv_k_k": 0.85 + nrm((L, RWKV_WIDTH), 0.02),
        "rwkv_k_a": 1.0 + nrm((L, RWKV_WIDTH), 0.02),
        "rwkv_r_k": nrm((L, RWKV_HEADS, HEAD_DIM), 0.1),
        "rwkv_ln_w": 1.0 + nrm((L, RWKV_WIDTH), 0.02),
        "rwkv_ln_b": nrm((L, RWKV_WIDTH), 0.02),
        "ssm_a_re": -0.5 + nrm((L, G, P), 0.01),
        "ssm_a_im": jnp.pi * jnp.arange(P, dtype=f32) + nrm((L, G, P), 0.01),
        "ssm_log_step": uni((L, G), math.log(STEP_MIN), math.log(STEP_MAX)),
        "ssm_b_re": nrm((L, G, P, C), (2 * C) ** -0.5),
        "ssm_b_im": nrm((L, G, P, C), (2 * C) ** -0.5),
        "ssm_c_re": nrm((L, G, C, P), P ** -0.5),
        "ssm_c_im": nrm((L, G, C, P), P ** -0.5),
        "ssm_d": nrm((L, SSM_WIDTH), 1.0),
        "ssm_glu_val": nrm((L, SSM_WIDTH, SSM_WIDTH), SSM_WIDTH ** -0.5),
        "ssm_glu_gate": nrm((L, SSM_WIDTH, SSM_WIDTH), SSM_WIDTH ** -0.5),
        "w_branch_attn": nrm((L, ATTN_WIDTH, D), ATTN_WIDTH ** -0.5),
        "w_branch_rwkv": nrm((L, RWKV_WIDTH, D), RWKV_WIDTH ** -0.5),
        "w_branch_ssm": nrm((L, SSM_WIDTH, D), SSM_WIDTH ** -0.5),
        "w_out": nrm((L, D, D), D ** -0.5),
        "norm_ffn": 1.0 + nrm((L, D), 0.02),
        "ffn_w_gate_up": nrm((L, D, 2 * FFN_HIDDEN), D ** -0.5),
        "ffn_w_down": nrm((L, FFN_HIDDEN, D), FFN_HIDDEN ** -0.5),
        "norm_final": 1.0 + nrm((D,), 0.02),
    }


def reference(x, norm_mix, w_in, rwkv_shift_mix, rwkv_w0, rwkv_w2, rwkv_a0, rwkv_a2, rwkv_g2,
              rwkv_k_k, rwkv_k_a, rwkv_r_k, rwkv_ln_w, rwkv_ln_b, ssm_a_re, ssm_a_im, ssm_log_step,
              ssm_b_re, ssm_b_im, ssm_c_re, ssm_c_im, ssm_d, ssm_glu_val, ssm_glu_gate,
              w_branch_attn, w_branch_rwkv, w_branch_ssm, w_out, norm_ffn, ffn_w_gate_up,
              ffn_w_down, norm_final):
    h = x
    for l in range(DEPTH):
        u = rmsnorm(h, norm_mix[l])
        p = u @ w_in[l]
        bsz, s, _ = p.shape
        y_attn = dilated_mixture_attention(p[..., :ATTN_IN]) @ w_branch_attn[l]
        y_rwkv = rwkv7_time_mix(p[..., RWKV_OFF:SSM_OFF], rwkv_shift_mix[l], rwkv_w0[l], rwkv_w2[l],
                                rwkv_a0[l], rwkv_a2[l], rwkv_g2[l], rwkv_k_k[l], rwkv_k_a[l],
                                rwkv_r_k[l], rwkv_ln_w[l], rwkv_ln_b[l]) @ w_branch_rwkv[l]
        y_ssm = s5_glu(p[..., SSM_OFF:GATE_OFF], ssm_a_re[l], ssm_a_im[l], ssm_log_step[l],
                       ssm_b_re[l], ssm_b_im[l], ssm_c_re[l], ssm_c_im[l], ssm_d[l],
                       ssm_glu_val[l], ssm_glu_gate[l]) @ w_branch_ssm[l]
        gates = jax.nn.sigmoid(p[..., GATE_OFF:].astype(jnp.float32)).reshape(bsz, s, N_BRANCH, D_MODEL)
        merged = gates[:, :, 0] * y_attn + gates[:, :, 1] * y_rwkv + gates[:, :, 2] * y_ssm
        h = h + (merged @ w_out[l]).astype(h.dtype)
        u = rmsnorm(h, norm_ffn[l])
        h = h + swiglu(u, ffn_w_gate_up[l], ffn_w_down[l]).astype(h.dtype)
    return rmsnorm(h, norm_final)
```

```python
import functools
import math

import jax
import jax.numpy as jnp
from jax import lax
from jax.experimental import pallas as pl
from jax.experimental.pallas import tpu as pltpu

F32 = jnp.float32
BF16 = jnp.bfloat16

HEAD_DIM = 64
ATTN_GROUPS = ((128, 1), (512, 4), (2048, 16))
ATTN_N_BACK = 128
HEADS_PER_GROUP = 2
DECAY_LORA = 64
AAA_LORA = 64
GATE_LORA = 128
SSM_GROUP_CH = 16
SSM_STATE = 64
GN_EPS = 64e-5
NORM_EPS = 1e-6
N_BRANCH = 3

V7X_LANES = 128
V7X_VMEM_BYTES = 64 * 1024 * 1024
VMEM_LIMIT = V7X_VMEM_BYTES - 8 * 1024 * 1024

ROW_TILE = 512
COL_CHUNK = 384
FFN_CHUNK = 256
ATTN_BLOCK = 2048
RWKV_BLOCK = 512
RWKV_CHUNK = 64
SSM_CHUNK = 16
SSM_SCAN_LEVELS = 8


def _cparams(*sem):
    return pltpu.CompilerParams(dimension_semantics=sem, vmem_limit_bytes=VMEM_LIMIT)


def _resident(shape):
    nd = len(shape)
    return pl.BlockSpec(shape, lambda *_: (0,) * nd, pipeline_mode=pl.Buffered(1))


def _dot(a, b):
    return jnp.dot(a.astype(BF16), b.astype(BF16), preferred_element_type=F32)


def _dot_nt(a, b):
    return lax.dot_general(a.astype(BF16), b.astype(BF16), (((1,), (1,)), ((), ())),
                           preferred_element_type=F32)


def _dot_tn(a, b):
    return lax.dot_general(a.astype(BF16), b.astype(BF16), (((0,), (0,)), ((), ())),
                           preferred_element_type=F32)


def _split3(x):
    h1 = x.astype(BF16)
    r1 = x - h1.astype(F32)
    h2 = r1.astype(BF16)
    h3 = (r1 - h2.astype(F32)).astype(BF16)
    return h1, h2, h3


def _dot_xw(x, w):
    h1, h2, h3 = _split3(x)
    return (jnp.dot(h1, w, preferred_element_type=F32) + jnp.dot(h2, w, preferred_element_type=F32)
            + jnp.dot(h3, w, preferred_element_type=F32))


def _dot_wx(w, x):
    h1, h2, h3 = _split3(x)
    return (jnp.dot(w, h1, preferred_element_type=F32) + jnp.dot(w, h2, preferred_element_type=F32)
            + jnp.dot(w, h3, preferred_element_type=F32))


def _dot_hi(a, b, dims=(((1,), (0,)), ((), ()))):
    ah = a.astype(BF16)
    al = (a - ah.astype(F32)).astype(BF16)
    bh = b.astype(BF16)
    bl = (b - bh.astype(F32)).astype(BF16)
    dg = functools.partial(lax.dot_general, dimension_numbers=dims, preferred_element_type=F32)
    return dg(ah, bh) + dg(ah, bl) + dg(al, bh)


def _rmsnorm(x, gain):
    ms = jnp.mean(x * x, axis=-1, keepdims=True)
    return x * lax.rsqrt(ms + NORM_EPS) * gain


def _sigmoid(x):
    return 1.0 / (1.0 + jnp.exp(-x))


def _in_proj_body(h_ref, gain_ref, w_ref, qkv_ref, rw_ref, ssm_ref, gate_ref, *, widths):
    attn_in, rwkv_in, ssm_w, gate_w = widths
    u = _rmsnorm(h_ref[...], gain_ref[...]).astype(BF16)

    def proj(lo, hi):
        return jnp.dot(u, w_ref[:, lo:hi], preferred_element_type=F32)

    def chunks(lo, hi):
        out, c = [], lo
        while c < hi:
            out.append((c, min(c + COL_CHUNK, hi)))
            c += COL_CHUNK
        return out

    for lo, hi in chunks(0, attn_in):
        y = proj(lo, hi)
        for j in range((hi - lo) // V7X_LANES):
            qkv_ref[lo // V7X_LANES + j] = y[:, j * V7X_LANES:(j + 1) * V7X_LANES]
    off = attn_in
    for lo, hi in chunks(off, off + rwkv_in):
        rw_ref[:, lo - off:hi - off] = proj(lo, hi)
    off += rwkv_in
    ssm_ref[...] = proj(off, off + ssm_w)
    off += ssm_w
    for lo, hi in chunks(off, off + gate_w):
        gate_ref[:, lo - off:hi - off] = _sigmoid(proj(lo, hi))


def _in_proj(h, gain, w_bf16, widths):
    t, d = h.shape
    attn_in, rwkv_in, ssm_w, gate_w = widths
    n_slab = attn_in // V7X_LANES
    tm = ROW_TILE
    return pl.pallas_call(
        functools.partial(_in_proj_body, widths=widths),
        grid=(t // tm,),
        in_specs=[pl.BlockSpec((tm, d), lambda i: (i, 0)),
                  _resident((1, d)),
                  _resident(w_bf16.shape)],
        out_specs=[pl.BlockSpec((n_slab, tm, V7X_LANES), lambda i: (0, i, 0)),
                   pl.BlockSpec((tm, rwkv_in), lambda i: (i, 0)),
                   pl.BlockSpec((tm, ssm_w), lambda i: (i, 0)),
                   pl.BlockSpec((tm, gate_w), lambda i: (i, 0))],
        out_shape=[jax.ShapeDtypeStruct((n_slab, t, V7X_LANES), F32),
                   jax.ShapeDtypeStruct((t, rwkv_in), F32),
                   jax.ShapeDtypeStruct((t, ssm_w), F32),
                   jax.ShapeDtypeStruct((t, gate_w), F32)],
        compiler_params=_cparams("arbitrary"),
        name="in_proj",
    )(h, gain.reshape(1, d), w_bf16)


def _attn_body(qkv_ref, o_ref, kb0, vb0, kb1, vb1, kb2, vb2, obuf, lbuf, *, tb):
    n = pl.program_id(1)
    nb = ATTN_N_BACK
    kbufs, vbufs = (kb0, kb1, kb2), (vb0, vb1, vb2)
    n_grp = len(ATTN_GROUPS)
    lane = lax.broadcasted_iota(jnp.int32, (nb, V7X_LANES), 1)
    head0 = lane < HEAD_DIM
    qi = lax.broadcasted_iota(jnp.int32, (nb, 2 * nb), 0)
    kj = lax.broadcasted_iota(jnp.int32, (nb, 2 * nb), 1)
    band = (kj >= qi) & (kj <= qi + nb)
    scale = HEAD_DIM ** -0.5

    for g, (window, dil) in enumerate(ATTN_GROUPS):
        span = window
        kb, vb = kbufs[g], vbufs[g]

        @pl.when(n == 0)
        def _():
            kb[0:span, :] = jnp.zeros((span, V7X_LANES), F32)
            vb[0:span, :] = jnp.zeros((span, V7X_LANES), F32)

        @pl.when(n > 0)
        def _():
            kb[0:span, :] = kb[tb:tb + span, :]
            vb[0:span, :] = vb[tb:tb + span, :]

        kb[span:span + tb, :] = qkv_ref[n_grp + g]
        vb[span:span + tb, :] = qkv_ref[2 * n_grp + g]
        n_sub = tb // span

        def unit(idx, carry, g=g, dil=dil, span=span, kb=kb, vb=vb, n_sub=n_sub):
            j = idx // dil
            r = idx - j * dil
            base = j * span + r
            q = qkv_ref[g, pl.ds(base, nb, stride=dil), :] * scale
            k2 = kb[pl.ds(base, 2 * nb, stride=dil), :].astype(BF16)
            v2 = vb[pl.ds(base, 2 * nb, stride=dil), :].astype(BF16)
            has_prev = (n * n_sub + j) > 0
            mask = band & ((kj >= nb) | has_prev)
            outs, lses = [], []
            for hmask in (head0, ~head0):
                s = _dot_nt(jnp.where(hmask, q, 0.0), k2)
                s = jnp.where(mask, s, -1e30)
                mx = jnp.max(s, axis=-1, keepdims=True)
                p = jnp.exp(s - mx)
                den = jnp.sum(p, axis=-1, keepdims=True)
                outs.append(_dot(p, v2) / den)
                lses.append(mx + jnp.log(den))
            obuf[g, pl.ds(base, nb, stride=dil), :] = jnp.where(head0, outs[0], outs[1])
            lbuf[g, pl.ds(base, nb, stride=dil), :] = jnp.where(head0, lses[0], lses[1])
            return carry

        lax.fori_loop(0, n_sub * dil, unit, 0)

    rows = 256
    def mix(i, carry):
        sl = pl.ds(pl.multiple_of(i * rows, rows), rows)
        ls = [lbuf[g, sl, :] for g in range(n_grp)]
        m = functools.reduce(jnp.maximum, ls)
        es = [jnp.exp(l - m) for l in ls]
        tot = functools.reduce(lambda a, b: a + b, es)
        for g in range(n_grp):
            o_ref[sl, g * V7X_LANES:(g + 1) * V7X_LANES] = obuf[g, sl, :] * (es[g] / tot)
        return carry
    lax.fori_loop(0, tb // rows, mix, 0)


def _attention(qkv, batch, seq):
    n_slab, t, _ = qkv.shape
    tb = min(ATTN_BLOCK, seq)
    nblk = seq // tb
    n_grp = len(ATTN_GROUPS)
    scratch = []
    for window, _ in ATTN_GROUPS:
        scratch += [pltpu.VMEM((window + tb, V7X_LANES), F32)] * 2
    scratch += [pltpu.VMEM((n_grp, tb, V7X_LANES), F32)] * 2
    return pl.pallas_call(
        functools.partial(_attn_body, tb=tb),
        grid=(batch, nblk),
        in_specs=[pl.BlockSpec((n_slab, tb, V7X_LANES), lambda b, n: (0, b * nblk + n, 0))],
        out_specs=pl.BlockSpec((tb, n_grp * V7X_LANES), lambda b, n: (b * nblk + n, 0)),
        out_shape=jax.ShapeDtypeStruct((t, n_grp * V7X_LANES), F32),
        scratch_shapes=scratch,
        compiler_params=_cparams("arbitrary", "arbitrary"),
        name="dilated_attention",
    )(qkv)


def _rwkv_body(z_ref, mix_ref, w0_ref, w2_ref, a0_ref, a2_ref, g2_ref, kk_ref, ka_ref, rk_ref,
               lnw_ref, lnb_ref, seg_ref, tri_ref, o_ref, zbuf, zz_s, state, *, tr, width):
    n = pl.program_id(1)
    c_len = RWKV_CHUNK
    n_heads = width // HEAD_DIM
    pad = 8

    @pl.when(n == 0)
    def _():
        zbuf[0:pad, :] = jnp.zeros((pad, zbuf.shape[1]), F32)
        state[...] = jnp.zeros(state.shape, F32)

    @pl.when(n > 0)
    def _():
        zbuf[0:pad, :] = zbuf[tr:tr + pad, :]

    zbuf[pad:pad + tr, :] = z_ref[...]
    mix = mix_ref[...]
    for i in range(tr // c_len):
        cur = zbuf[pad + i * c_len:pad + (i + 1) * c_len, :]
        prev = zbuf[pad - 1 + i * c_len:pad - 1 + (i + 1) * c_len, :]
        zz_s[i * c_len:(i + 1) * c_len, :] = cur + (prev - cur) * mix

    row = lax.broadcasted_iota(jnp.int32, (c_len, c_len), 0)
    col = lax.broadcasted_iota(jnp.int32, (c_len, c_len), 1)
    strict = col < row
    incl = col <= row
    eye = (col == row).astype(F32)
    seg = seg_ref[...]
    tri = tri_ref[...]
    lora = DECAY_LORA + AAA_LORA
    n_sq = int(math.log2(c_len)) - 1

    def chunk(c, carry):
        sl = pl.ds(pl.multiple_of(c * c_len, c_len), c_len)
        zz = zz_s[sl, :]
        r = zz[:, 0:width]
        k = zz[:, width:2 * width]
        v = zz[:, 2 * width:3 * width]
        t = zz[:, 3 * width:3 * width + lora]
        xg = zz[:, 3 * width + lora:3 * width + lora + GATE_LORA]

        wpre = w0_ref[...] + _dot_hi(jnp.tanh(t), w2_ref[...])
        nw = -wpre
        softplus = jnp.maximum(nw, 0.0) + jnp.log(1.0 + jnp.exp(-jnp.abs(nw)))
        lw = -jnp.exp(-softplus - 0.5)
        a = _sigmoid(a0_ref[...] + _dot_hi(t, a2_ref[...]))
        gate = _dot(_sigmoid(xg), g2_ref[...])

        kk = k * kk_ref[...]
        nrm = jnp.sqrt(_dot_xw(kk * kk, seg))
        kk = kk / jnp.maximum(nrm, 1e-12)
        k = k * (1.0 + (a - 1.0) * ka_ref[...])
        bonus = _dot_xw(r * k * rk_ref[...], seg) * v
        avec = -kk
        bvec = kk * a

        lc = _dot_wx(tri, lw)
        lc_end = lc[c_len - 1:c_len, :]
        e_pos = jnp.exp(lc)
        e_neg = jnp.exp(-lc)
        e_end = jnp.exp(lc_end - lc)
        rt = (r * e_pos).astype(BF16)
        at = (avec * jnp.exp(lc - lw)).astype(BF16)
        kt = (k * e_neg).astype(BF16)
        bt = (bvec * e_neg).astype(BF16)
        kt_end = (k * e_end).astype(BF16)
        bt_end = (bvec * e_end).astype(BF16)
        p_end = jnp.exp(lc_end)
        vb = v.astype(BF16)

        ys = []
        for h in range(n_heads):
            hs = slice(h * HEAD_DIM, (h + 1) * HEAD_DIM)
            sm = state[h]
            smb = sm.astype(BF16)
            ar = jnp.concatenate([at[:, hs], rt[:, hs]], axis=0)
            gb = _dot_nt(ar, bt[:, hs])
            gk = _dot_nt(ar, kt[:, hs])
            lmat = jnp.where(strict, gb[0:c_len], 0.0)
            ak = jnp.where(strict, gk[0:c_len], 0.0)
            rb = jnp.where(incl, gb[c_len:], 0.0)
            rk = jnp.where(incl, gk[c_len:], 0.0)
            tinv = eye + lmat
            pw = lmat
            for _ in range(n_sq):
                pw = _dot(pw, pw)
                tinv = tinv + _dot(tinv, pw)
            vh = vb[:, hs]
            u = _dot(tinv, _dot_nt(at[:, hs], smb) + _dot(ak, vh))
            ub = u.astype(BF16)
            ys.append(_dot_nt(rt[:, hs], smb) + _dot(rb, ub) + _dot(rk, vh))
            uv = jnp.concatenate([ub, vh], axis=0)
            bk = jnp.concatenate([bt_end[:, hs], kt_end[:, hs]], axis=0)
            state[h] = sm * p_end[:, hs] + _dot_tn(uv, bk)
        y = jnp.concatenate(ys, axis=1)

        inv_e = 1.0 / HEAD_DIM
        mu = _dot_xw(y, seg) * inv_e
        yc = y - mu
        var = _dot_xw(yc * yc, seg) * inv_e
        yn = yc * lax.rsqrt(var + GN_EPS) * lnw_ref[...] + lnb_ref[...]
        o_ref[sl, :] = (yn + bonus) * gate
        return carry

    lax.fori_loop(0, tr // c_len, chunk, 0)


def _rwkv(z, batch, seq, prm):
    t, zin = z.shape
    width = prm["w0"].shape[-1]
    tr = min(RWKV_BLOCK, seq)
    nblk = seq // tr
    n_heads = width // HEAD_DIM
    lora = DECAY_LORA + AAA_LORA
    row = lambda x: x.reshape(1, -1).astype(F32)
    zeros = jnp.zeros((AAA_LORA, width), F32)
    w2p = jnp.concatenate([prm["w2"], zeros], axis=0)
    a2p = jnp.concatenate([jnp.zeros((DECAY_LORA, width), F32), prm["a2"]], axis=0)
    hid = jnp.arange(width) // HEAD_DIM
    seg = (hid[:, None] == hid[None, :]).astype(BF16)
    tri = jnp.tril(jnp.ones((RWKV_CHUNK, RWKV_CHUNK), BF16))
    params = [row(prm["mix"]), row(prm["w0"]), w2p, row(prm["a0"]), a2p, prm["g2"].astype(BF16),
              row(prm["k_k"]), row(prm["k_a"]), row(prm["r_k"]), row(prm["ln_w"]), row(prm["ln_b"]),
              seg, tri]
    return pl.pallas_call(
        functools.partial(_rwkv_body, tr=tr, width=width),
        grid=(batch, nblk),
        in_specs=[pl.BlockSpec((tr, zin), lambda b, n: (b * nblk + n, 0))]
                 + [_resident(p.shape) for p in params],
        out_specs=pl.BlockSpec((tr, width), lambda b, n: (b * nblk + n, 0)),
        out_shape=jax.ShapeDtypeStruct((t, width), F32),
        scratch_shapes=[pltpu.VMEM((tr + 8, zin), F32), pltpu.VMEM((tr, zin), F32),
                        pltpu.VMEM((n_heads, HEAD_DIM, HEAD_DIM), F32)],
        compiler_params=_cparams("arbitrary", "arbitrary"),
        name="rwkv7_time_mix",
    )(z, *params)


def _ssm_prep_body(are_ref, aim_ref, lstep_ref, bre_ref, bim_ref, cre_ref, cim_ref,
                   kt_ref, wre_ref, wim_ref, vre_ref, vim_ref, pre_ref, pim_ref):
    lam_re = are_ref[0]
    lam_im = aim_ref[0]
    step = jnp.exp(lstep_ref[0])
    mag = jnp.exp(lam_re * step)
    ang = lam_im * step
    abar_re, abar_im = mag * jnp.cos(ang), mag * jnp.sin(ang)
    inv = 1.0 / (lam_re * lam_re + lam_im * lam_im)
    f_re = ((abar_re - 1.0) * lam_re + abar_im * lam_im) * inv
    f_im = (abar_im * lam_re - (abar_re - 1.0) * lam_im) * inv
    b_re, b_im = bre_ref[0], bim_ref[0]
    bbar_re = f_re * b_re - f_im * b_im
    bbar_im = f_re * b_im + f_im * b_re
    c_re, c_im = cre_ref[0], cim_ref[0]

    def power(tau):
        m = jnp.exp(lam_re * step * tau)
        return m * jnp.cos(ang * tau), m * jnp.sin(ang * tau)

    nt = (((1,), (1,)), ((), ()))
    for tau in range(SSM_CHUNK):
        p_re, p_im = power(float(tau))
        cp_re = c_re * p_re - c_im * p_im
        cp_im = c_re * p_im + c_im * p_re
        kt_ref[0, tau] = _dot_hi(cp_re, bbar_re, nt) - _dot_hi(cp_im, bbar_im, nt)
        q_re, q_im = power(float(SSM_CHUNK - 1 - tau))
        wre_ref[0, tau] = q_re * bbar_re - q_im * bbar_im
        wim_ref[0, tau] = q_re * bbar_im + q_im * bbar_re
        s_re, s_im = power(float(tau + 1))
        vre_ref[0, tau] = c_re * s_re - c_im * s_im
        vim_ref[0, tau] = -(c_re * s_im + c_im * s_re)
    for lvl in range(SSM_SCAN_LEVELS):
        p_re, p_im = power(float(SSM_CHUNK * 2 ** lvl))
        pre_ref[0, lvl:lvl + 1, :] = p_re
        pim_ref[0, lvl:lvl + 1, :] = p_im


def _ssm_prep(a_re, a_im, log_step, b_re, b_im, c_re, c_im):
    g, p = a_re.shape
    c = SSM_GROUP_CH
    l = SSM_CHUNK
    grp3 = lambda shape: pl.BlockSpec((1,) + shape, lambda i: (i, 0, 0))
    grp4 = lambda shape: pl.BlockSpec((1,) + shape, lambda i: (i, 0, 0, 0))
    sds = jax.ShapeDtypeStruct
    return pl.pallas_call(
        _ssm_prep_body,
        grid=(g,),
        in_specs=[grp3((1, p)), grp3((1, p)), grp3((1, 1)), grp3((c, p)), grp3((c, p)),
                  grp3((c, p)), grp3((c, p))],
        out_specs=[grp4((l, c, c)), grp4((l, c, p)), grp4((l, c, p)), grp4((l, c, p)), grp4((l, c, p)),
                   grp3((SSM_SCAN_LEVELS, p)), grp3((SSM_SCAN_LEVELS, p))],
        out_shape=[sds((g, l, c, c), F32), sds((g, l, c, p), F32), sds((g, l, c, p), F32),
                   sds((g, l, c, p), F32), sds((g, l, c, p), F32),
                   sds((g, SSM_SCAN_LEVELS, p), F32), sds((g, SSM_SCAN_LEVELS, p), F32)],
        compiler_params=_cparams("arbitrary"),
        name="ssm_discretise",
    )(a_re.reshape(g, 1, p), a_im.reshape(g, 1, p), log_step.reshape(g, 1, 1),
      jnp.swapaxes(b_re, 1, 2), jnp.swapaxes(b_im, 1, 2), c_re, c_im)


def _ssm_scan_body(u_ref, t_ref, w_ref, v_ref, m1_ref, m2_ref, d_ref, y_ref, *, n_chunks):
    u = u_ref[0]
    ub = u.astype(BF16)
    x = jnp.dot(ub, w_ref[0], preferred_element_type=F32)
    rows, two_p = x.shape
    pos = lax.broadcasted_iota(jnp.int32, (rows, two_p), 0) % n_chunks
    m1 = m1_ref[0]
    m2 = m2_ref[0]
    for lvl in range(SSM_SCAN_LEVELS):
        s = 2 ** lvl
        if s >= n_chunks:
            break
        xs = jnp.where(pos >= s, pltpu.roll(x, s, 0), 0.0)
        x = x + xs * m1[lvl:lvl + 1, :] + pltpu.roll(xs, two_p // 2, 1) * m2[lvl:lvl + 1, :]
    x_in = jnp.where(pos >= 1, pltpu.roll(x, 1, 0), 0.0)
    y = (jnp.dot(ub, t_ref[0], preferred_element_type=F32)
         + jnp.dot(x_in.astype(BF16), v_ref[0], preferred_element_type=F32)
         + u * d_ref[0])
    y_ref[0] = y


def _ssm_scan(u, tmat, wmat, vmat, m1, m2, dvec, n_chunks):
    g, rows, lc = u.shape
    blk = lambda a: pl.BlockSpec((1,) + a.shape[1:], lambda i: (i, 0, 0))
    return pl.pallas_call(
        functools.partial(_ssm_scan_body, n_chunks=n_chunks),
        grid=(g,),
        in_specs=[blk(u), blk(tmat), blk(wmat), blk(vmat), blk(m1), blk(m2), blk(dvec)],
        out_specs=blk(u),
        out_shape=jax.ShapeDtypeStruct(u.shape, F32),
        compiler_params=_cparams("arbitrary"),
        name="ssm_chunk_scan",
    )(u, tmat, wmat, vmat, m1, m2, dvec)


def _ssm(u_tok, batch, seq, prm):
    t, w = u_tok.shape
    c, l, p = SSM_GROUP_CH, SSM_CHUNK, SSM_STATE
    g = w // c
    n_chunks = seq // l
    kt, wre, wim, vre, vim, pre, pim = _ssm_prep(prm["a_re"], prm["a_im"], prm["log_step"],
                                                 prm["b_re"], prm["b_im"], prm["c_re"], prm["c_im"])
    i_out = jnp.arange(l)[None, :]
    j_in = jnp.arange(l)[:, None]
    lag = jnp.clip(i_out - j_in, 0, l - 1)
    tblk = jnp.where((i_out >= j_in)[None, :, :, None, None], kt[:, lag], 0.0)
    tmat = jnp.transpose(tblk, (0, 1, 4, 2, 3)).reshape(g, l * c, l * c).astype(BF16)
    wmat = jnp.concatenate([wre, wim], axis=-1).reshape(g, l * c, 2 * p).astype(BF16)
    vmat = jnp.concatenate([jnp.transpose(vre, (0, 3, 1, 2)), jnp.transpose(vim, (0, 3, 1, 2))],
                           axis=1).reshape(g, 2 * p, l * c).astype(BF16)
    m1 = jnp.concatenate([pre, pre], axis=-1)
    m2 = jnp.concatenate([-pim, pim], axis=-1)
    dvec = jnp.tile(prm["d"].reshape(g, 1, c), (1, 1, l))
    u = jnp.transpose(u_tok.reshape(batch, n_chunks, l, g, c), (3, 0, 1, 2, 4)).reshape(g, batch * n_chunks, l * c)
    y = _ssm_scan(u, tmat, wmat, vmat, m1, m2, dvec, n_chunks)
    return jnp.transpose(y.reshape(g, batch, n_chunks, l, c), (1, 2, 3, 0, 4)).reshape(t, w)


def _merge_body(oa_ref, or_ref, ys_ref, gate_ref, h_ref, wa_ref, wr_ref, ws_ref, wv_ref, wg_ref, wo_ref, out_ref):
    d = h_ref.shape[1]
    y = ys_ref[...]
    zg = (0.5 * y * (1.0 + jnp.tanh(math.sqrt(2.0 / math.pi) * (y + 0.044715 * (y * y * y))))).astype(BF16)
    o_ssm = (jnp.dot(zg, wv_ref[...], preferred_element_type=F32)
             * _sigmoid(jnp.dot(zg, wg_ref[...], preferred_element_type=F32)))
    merged = (gate_ref[:, 0:d] * _dot(oa_ref[...], wa_ref[...])
              + gate_ref[:, d:2 * d] * _dot(or_ref[...], wr_ref[...])
              + gate_ref[:, 2 * d:3 * d] * _dot(o_ssm, ws_ref[...]))
    out_ref[...] = h_ref[...] + _dot(merged, wo_ref[...])


def _merge(o_attn, o_rwkv, y_ssm, gates, h, weights):
    t, d = h.shape
    tm = ROW_TILE
    rows = lambda a: pl.BlockSpec((tm, a.shape[1]), lambda i: (i, 0))
    acts = [o_attn, o_rwkv, y_ssm, gates, h]
    return pl.pallas_call(
        _merge_body,
        grid=(t // tm,),
        in_specs=[rows(a) for a in acts] + [_resident(w.shape) for w in weights],
        out_specs=rows(h),
        out_shape=jax.ShapeDtypeStruct((t, d), F32),
        compiler_params=_cparams("arbitrary"),
        name="merge_out_proj",
    )(*acts, *weights)


def _ffn_body(h_ref, gain_ref, wgu_ref, wd_ref, fgain_ref, o_ref, *, hidden, final_norm):
    x = h_ref[...]
    u = _rmsnorm(x, gain_ref[...]).astype(BF16)
    acc = jnp.zeros(x.shape, F32)
    for c in range(hidden // FFN_CHUNK):
        lo = c * FFN_CHUNK
        a = jnp.dot(u, wgu_ref[:, lo:lo + FFN_CHUNK], preferred_element_type=F32)
        b = jnp.dot(u, wgu_ref[:, hidden + lo:hidden + lo + FFN_CHUNK], preferred_element_type=F32)
        hm = (a * _sigmoid(a) * b).astype(BF16)
        acc = acc + jnp.dot(hm, wd_ref[lo:lo + FFN_CHUNK, :], preferred_element_type=F32)
    y = x + acc
    if final_norm:
        y = _rmsnorm(y, fgain_ref[...])
    o_ref[...] = y


def _ffn(h, gain, wgu, wd, final_gain, final_norm):
    t, d = h.shape
    hidden = wd.shape[0]
    tm = ROW_TILE
    return pl.pallas_call(
        functools.partial(_ffn_body, hidden=hidden, final_norm=final_norm),
        grid=(t // tm,),
        in_specs=[pl.BlockSpec((tm, d), lambda i: (i, 0)), _resident((1, d)),
                  _resident(wgu.shape), _resident(wd.shape), _resident((1, d))],
        out_specs=pl.BlockSpec((tm, d), lambda i: (i, 0)),
        out_shape=jax.ShapeDtypeStruct((t, d), F32),
        compiler_params=_cparams("arbitrary"),
        name="swiglu_ffn",
    )(h, gain.reshape(1, d), wgu, wd, final_gain.reshape(1, d))


def kernel(x, norm_mix, w_in, rwkv_shift_mix, rwkv_w0, rwkv_w2, rwkv_a0, rwkv_a2, rwkv_g2, rwkv_k_k, rwkv_k_a, rwkv_r_k, rwkv_ln_w, rwkv_ln_b, ssm_a_re, ssm_a_im, ssm_log_step, ssm_b_re, ssm_b_im, ssm_c_re, ssm_c_im, ssm_d, ssm_glu_val, ssm_glu_gate, w_branch_attn, w_branch_rwkv, w_branch_ssm, w_out, norm_ffn, ffn_w_gate_up, ffn_w_down, norm_final):
    batch, seq, d = x.shape
    depth = w_in.shape[0]
    attn_w = w_branch_attn.shape[1]
    rwkv_in = rwkv_shift_mix.shape[1]
    ssm_w = w_branch_ssm.shape[1]
    widths = (3 * attn_w, rwkv_in, ssm_w, N_BRANCH * d)
    assert sum(widths) == w_in.shape[2]
    assert attn_w == len(ATTN_GROUPS) * HEADS_PER_GROUP * HEAD_DIM
    assert seq % ATTN_BLOCK == 0 or seq in (w for w, _ in ATTN_GROUPS)
    assert seq // SSM_CHUNK <= 2 ** SSM_SCAN_LEVELS and (batch * seq) % ROW_TILE == 0

    h = x.reshape(batch * seq, d)
    for l in range(depth):
        qkv, z_rwkv, u_ssm, gates = _in_proj(h, norm_mix[l], w_in[l].astype(BF16), widths)
        o_attn = _attention(qkv, batch, seq)
        o_rwkv = _rwkv(z_rwkv, batch, seq, dict(
            mix=rwkv_shift_mix[l], w0=rwkv_w0[l], w2=rwkv_w2[l], a0=rwkv_a0[l], a2=rwkv_a2[l],
            g2=rwkv_g2[l], k_k=rwkv_k_k[l], k_a=rwkv_k_a[l], r_k=rwkv_r_k[l],
            ln_w=rwkv_ln_w[l], ln_b=rwkv_ln_b[l]))
        y_ssm = _ssm(u_ssm, batch, seq, dict(
            a_re=ssm_a_re[l], a_im=ssm_a_im[l], log_step=ssm_log_step[l], b_re=ssm_b_re[l],
            b_im=ssm_b_im[l], c_re=ssm_c_re[l], c_im=ssm_c_im[l], d=ssm_d[l]))
        bf = lambda w: w.astype(BF16)
        h = _merge(o_attn, o_rwkv, y_ssm, gates, h,
                   [bf(w_branch_attn[l]), bf(w_branch_rwkv[l]), bf(w_branch_ssm[l]),
                    bf(ssm_glu_val[l]), bf(ssm_glu_gate[l]), bf(w_out[l])])
        last = l == depth - 1
        h = _ffn(h, norm_ffn[l], bf(ffn_w_gate_up[l]), bf(ffn_w_down[l]), norm_final, last)
    return h.reshape(batch, seq, d)
```

```python
import functools
import math

import jax
import jax.numpy as jnp
from jax import lax
from jax.experimental import pallas as pl
from jax.experimental.pallas import tpu as pltpu

F32 = jnp.float32
BF16 = jnp.bfloat16

HEAD_DIM = 64
ATTN_GROUPS = ((128, 1), (512, 4), (2048, 16))
ATTN_N_BACK = 128
HEADS_PER_GROUP = 2
DECAY_LORA = 64
AAA_LORA = 64
GATE_LORA = 128
SSM_GROUP_CH = 16
SSM_STATE = 64
GN_EPS = 64e-5
NORM_EPS = 1e-6
N_BRANCH = 3

V7X_LANES = 128
V7X_VMEM_BYTES = 64 * 1024 * 1024
VMEM_LIMIT = V7X_VMEM_BYTES - 8 * 1024 * 1024

ROW_TILE = 512
COL_CHUNK = 384
FFN_CHUNK = 256
ATTN_BLOCK = 2048
RWKV_BLOCK = 512
RWKV_CHUNK = 64
SSM_CHUNK = 16
SSM_SCAN_LEVELS = 8


def _cparams(*sem):
    return pltpu.CompilerParams(dimension_semantics=sem, vmem_limit_bytes=VMEM_LIMIT)


def _resident(shape):
    nd = len(shape)
    return pl.BlockSpec(shape, lambda *_: (0,) * nd, pipeline_mode=pl.Buffered(1))


def _dot(a, b):
    return jnp.dot(a.astype(BF16), b.astype(BF16), preferred_element_type=F32)


def _dot_nt(a, b):
    return lax.dot_general(a.astype(BF16), b.astype(BF16), (((1,), (1,)), ((), ())),
                           preferred_element_type=F32)


def _dot_tn(a, b):
    return lax.dot_general(a.astype(BF16), b.astype(BF16), (((0,), (0,)), ((), ())),
                           preferred_element_type=F32)


def _split3(x):
    h1 = x.astype(BF16)
    r1 = x - h1.astype(F32)
    h2 = r1.astype(BF16)
    h3 = (r1 - h2.astype(F32)).astype(BF16)
    return h1, h2, h3


def _dot_xw(x, w):
    m = x.shape[0]
    y = jnp.dot(jnp.concatenate(_split3(x), axis=0), w, preferred_element_type=F32)
    return y[0:m] + y[m:2 * m] + y[2 * m:3 * m]


def _dot_wx(w, x):
    n = x.shape[1]
    y = jnp.dot(w, jnp.concatenate(_split3(x), axis=1), preferred_element_type=F32)
    return y[:, 0:n] + y[:, n:2 * n] + y[:, 2 * n:3 * n]


def _dot_hi(a, b, dims=(((1,), (0,)), ((), ()))):
    ah = a.astype(BF16)
    al = (a - ah.astype(F32)).astype(BF16)
    bh = b.astype(BF16)
    bl = (b - bh.astype(F32)).astype(BF16)
    dg = functools.partial(lax.dot_general, dimension_numbers=dims, preferred_element_type=F32)
    return dg(ah, bh) + dg(ah, bl) + dg(al, bh)


def _rmsnorm(x, gain):
    ms = jnp.mean(x * x, axis=-1, keepdims=True)
    return x * lax.rsqrt(ms + NORM_EPS) * gain


def _sigmoid(x):
    return 1.0 / (1.0 + jnp.exp(-x))


def _in_proj_body(h_ref, gain_ref, w_ref, qkv_ref, rw_ref, ssm_ref, gate_ref, *, widths):
    attn_in, rwkv_in, ssm_w, gate_w = widths
    u = _rmsnorm(h_ref[...], gain_ref[...]).astype(BF16)

    def proj(lo, hi):
        return jnp.dot(u, w_ref[:, lo:hi], preferred_element_type=F32)

    def chunks(lo, hi):
        out, c = [], lo
        while c < hi:
            out.append((c, min(c + COL_CHUNK, hi)))
            c += COL_CHUNK
        return out

    for lo, hi in chunks(0, attn_in):
        y = proj(lo, hi)
        for j in range((hi - lo) // V7X_LANES):
            qkv_ref[lo // V7X_LANES + j] = y[:, j * V7X_LANES:(j + 1) * V7X_LANES]
    off = attn_in
    for lo, hi in chunks(off, off + rwkv_in):
        rw_ref[:, lo - off:hi - off] = proj(lo, hi)
    off += rwkv_in
    ssm_ref[...] = proj(off, off + ssm_w)
    off += ssm_w
    for lo, hi in chunks(off, off + gate_w):
        gate_ref[:, lo - off:hi - off] = _sigmoid(proj(lo, hi))


def _in_proj(h, gain, w_bf16, widths):
    t, d = h.shape
    attn_in, rwkv_in, ssm_w, gate_w = widths
    n_slab = attn_in // V7X_LANES
    tm = ROW_TILE
    return pl.pallas_call(
        functools.partial(_in_proj_body, widths=widths),
        grid=(t // tm,),
        in_specs=[pl.BlockSpec((tm, d), lambda i: (i, 0)),
                  _resident((1, d)),
                  _resident(w_bf16.shape)],
        out_specs=[pl.BlockSpec((n_slab, tm, V7X_LANES), lambda i: (0, i, 0)),
                   pl.BlockSpec((tm, rwkv_in), lambda i: (i, 0)),
                   pl.BlockSpec((tm, ssm_w), lambda i: (i, 0)),
                   pl.BlockSpec((tm, gate_w), lambda i: (i, 0))],
        out_shape=[jax.ShapeDtypeStruct((n_slab, t, V7X_LANES), F32),
                   jax.ShapeDtypeStruct((t, rwkv_in), F32),
                   jax.ShapeDtypeStruct((t, ssm_w), F32),
                   jax.ShapeDtypeStruct((t, gate_w), F32)],
        compiler_params=_cparams("arbitrary"),
        name="in_proj",
    )(h, gain.reshape(1, d), w_bf16)


def _attn_body(qkv_ref, o_ref, kb0, vb0, kb1, vb1, kb2, vb2, obuf, lbuf, *, tb):
    n = pl.program_id(1)
    nb = ATTN_N_BACK
    kbufs, vbufs = (kb0, kb1, kb2), (vb0, vb1, vb2)
    n_grp = len(ATTN_GROUPS)
    lane = lax.broadcasted_iota(jnp.int32, (nb, V7X_LANES), 1)
    head0 = lane < HEAD_DIM
    qi = lax.broadcasted_iota(jnp.int32, (nb, 2 * nb), 0)
    kj = lax.broadcasted_iota(jnp.int32, (nb, 2 * nb), 1)
    band = (kj >= qi) & (kj <= qi + nb)
    scale = HEAD_DIM ** -0.5

    for g, (window, dil) in enumerate(ATTN_GROUPS):
        span = window
        kb, vb = kbufs[g], vbufs[g]

        @pl.when(n == 0)
        def _():
            kb[0:span, :] = jnp.zeros((span, V7X_LANES), F32)
            vb[0:span, :] = jnp.zeros((span, V7X_LANES), F32)

        @pl.when(n > 0)
        def _():
            kb[0:span, :] = kb[tb:tb + span, :]
            vb[0:span, :] = vb[tb:tb + span, :]

        kb[span:span + tb, :] = qkv_ref[n_grp + g]
        vb[span:span + tb, :] = qkv_ref[2 * n_grp + g]
        n_sub = tb // span

        def unit(idx, carry, g=g, dil=dil, span=span, kb=kb, vb=vb, n_sub=n_sub):
            j = idx // dil
            r = idx - j * dil
            base = j * span + r
            q = qkv_ref[g, pl.ds(base, nb, stride=dil), :] * scale
            k2 = kb[pl.ds(base, 2 * nb, stride=dil), :].astype(BF16)
            v2 = vb[pl.ds(base, 2 * nb, stride=dil), :].astype(BF16)
            has_prev = (n * n_sub + j) > 0
            mask = band & ((kj >= nb) | has_prev)
            scores = [_dot_nt(jnp.where(hmask, q, 0.0), k2) for hmask in (head0, ~head0)]
            outs, lses = [], []
            for s in scores:
                s = jnp.where(mask, s, -1e30)
                mx = jnp.max(s, axis=-1, keepdims=True)
                p = jnp.exp(s - mx)
                den = jnp.sum(p, axis=-1, keepdims=True)
                outs.append(_dot(p, v2) / den)
                lses.append(mx + jnp.log(den))
            obuf[g, pl.ds(base, nb, stride=dil), :] = jnp.where(head0, outs[0], outs[1])
            lbuf[g, pl.ds(base, nb, stride=dil), :] = jnp.where(head0, lses[0], lses[1])
            return carry

        lax.fori_loop(0, n_sub * dil, unit, 0)

    rows = 256
    def mix(i, carry):
        sl = pl.ds(pl.multiple_of(i * rows, rows), rows)
        ls = [lbuf[g, sl, :] for g in range(n_grp)]
        m = functools.reduce(jnp.maximum, ls)
        es = [jnp.exp(l - m) for l in ls]
        tot = functools.reduce(lambda a, b: a + b, es)
        for g in range(n_grp):
            o_ref[sl, g * V7X_LANES:(g + 1) * V7X_LANES] = obuf[g, sl, :] * (es[g] / tot)
        return carry
    lax.fori_loop(0, tb // rows, mix, 0)


def _attention(qkv, batch, seq):
    n_slab, t, _ = qkv.shape
    tb = min(ATTN_BLOCK, seq)
    nblk = seq // tb
    n_grp = len(ATTN_GROUPS)
    scratch = []
    for window, _ in ATTN_GROUPS:
        scratch += [pltpu.VMEM((window + tb, V7X_LANES), F32)] * 2
    scratch += [pltpu.VMEM((n_grp, tb, V7X_LANES), F32)] * 2
    return pl.pallas_call(
        functools.partial(_attn_body, tb=tb),
        grid=(batch, nblk),
        in_specs=[pl.BlockSpec((n_slab, tb, V7X_LANES), lambda b, n: (0, b * nblk + n, 0))],
        out_specs=pl.BlockSpec((tb, n_grp * V7X_LANES), lambda b, n: (b * nblk + n, 0)),
        out_shape=jax.ShapeDtypeStruct((t, n_grp * V7X_LANES), F32),
        scratch_shapes=scratch,
        compiler_params=_cparams("arbitrary", "arbitrary"),
        name="dilated_attention",
    )(qkv)


def _rwkv_body(z_ref, mix_ref, w0_ref, w2_ref, a0_ref, a2_ref, g2_ref, kk_ref, ka_ref, rk_ref,
               lnw_ref, lnb_ref, seg_ref, tri_ref, o_ref, zbuf, zz_s, state, *, tr, width):
    n = pl.program_id(1)
    c_len = RWKV_CHUNK
    n_heads = width // HEAD_DIM
    pad = 8

    @pl.when(n == 0)
    def _():
        zbuf[0:pad, :] = jnp.zeros((pad, zbuf.shape[1]), F32)
        state[...] = jnp.zeros(state.shape, F32)

    @pl.when(n > 0)
    def _():
        zbuf[0:pad, :] = zbuf[tr:tr + pad, :]

    zbuf[pad:pad + tr, :] = z_ref[...]
    mix = mix_ref[...]
    for i in range(tr // c_len):
        cur = zbuf[pad + i * c_len:pad + (i + 1) * c_len, :]
        prev = zbuf[pad - 1 + i * c_len:pad - 1 + (i + 1) * c_len, :]
        zz_s[i * c_len:(i + 1) * c_len, :] = cur + (prev - cur) * mix

    row = lax.broadcasted_iota(jnp.int32, (c_len, c_len), 0)
    col = lax.broadcasted_iota(jnp.int32, (c_len, c_len), 1)
    strict = col < row
    incl = col <= row
    eye = (col == row).astype(F32)
    seg = seg_ref[...]
    tri = tri_ref[...]
    lora = DECAY_LORA + AAA_LORA
    n_sq = int(math.log2(c_len)) - 1

    def chunk(c, carry):
        sl = pl.ds(pl.multiple_of(c * c_len, c_len), c_len)
        zz = zz_s[sl, :]
        r = zz[:, 0:width]
        k = zz[:, width:2 * width]
        v = zz[:, 2 * width:3 * width]
        t = zz[:, 3 * width:3 * width + lora]
        xg = zz[:, 3 * width + lora:3 * width + lora + GATE_LORA]

        wpre = w0_ref[...] + _dot_hi(jnp.tanh(t), w2_ref[...])
        nw = -wpre
        softplus = jnp.maximum(nw, 0.0) + jnp.log(1.0 + jnp.exp(-jnp.abs(nw)))
        lw = -jnp.exp(-softplus - 0.5)
        a = _sigmoid(a0_ref[...] + _dot_hi(t, a2_ref[...]))
        gate = _dot(_sigmoid(xg), g2_ref[...])

        kk = k * kk_ref[...]
        nrm = jnp.sqrt(_dot_xw(kk * kk, seg))
        kk = kk / jnp.maximum(nrm, 1e-12)
        k = k * (1.0 + (a - 1.0) * ka_ref[...])
        bonus = _dot_xw(r * k * rk_ref[...], seg) * v
        avec = -kk
        bvec = kk * a

        lc = _dot_wx(tri, lw)
        lc_end = lc[c_len - 1:c_len, :]
        e_pos = jnp.exp(lc)
        e_neg = jnp.exp(-lc)
        e_end = jnp.exp(lc_end - lc)
        rt = (r * e_pos).astype(BF16)
        at = (avec * jnp.exp(lc - lw)).astype(BF16)
        kt = (k * e_neg).astype(BF16)
        bt = (bvec * e_neg).astype(BF16)
        kt_end = (k * e_end).astype(BF16)
        bt_end = (bvec * e_end).astype(BF16)
        p_end = jnp.exp(lc_end)
        vb = v.astype(BF16)

        heads = range(n_heads)
        hsl = [slice(h * HEAD_DIM, (h + 1) * HEAD_DIM) for h in heads]
        sm = [state[h] for h in heads]
        smb = [m.astype(BF16) for m in sm]
        ar = [jnp.concatenate([at[:, s], rt[:, s]], axis=0) for s in hsl]
        vh = [vb[:, s] for s in hsl]
        gb = [_dot_nt(ar[h], bt[:, hsl[h]]) for h in heads]
        gk = [_dot_nt(ar[h], kt[:, hsl[h]]) for h in heads]
        lmat = [jnp.where(strict, gb[h][0:c_len], 0.0) for h in heads]
        rb = [jnp.where(incl, gb[h][c_len:], 0.0) for h in heads]
        akrk = [jnp.concatenate([jnp.where(strict, gk[h][0:c_len], 0.0),
                                 jnp.where(incl, gk[h][c_len:], 0.0)], axis=0) for h in heads]
        pw = [_dot(lmat[h], lmat[h]) for h in heads]
        tinv = [eye + lmat[h] for h in heads]
        s0 = [_dot_nt(ar[h], smb[h]) for h in heads]
        kv = [_dot(akrk[h], vh[h]) for h in heads]
        for _ in range(n_sq - 1):
            pw_next = [_dot(pw[h], pw[h]) for h in heads]
            tinv = [tinv[h] + _dot(tinv[h], pw[h]) for h in heads]
            pw = pw_next
        tinv = [tinv[h] + _dot(tinv[h], pw[h]) for h in heads]
        ub = [_dot(tinv[h], s0[h][0:c_len] + kv[h][0:c_len]).astype(BF16) for h in heads]
        ys = [s0[h][c_len:] + _dot(rb[h], ub[h]) + kv[h][c_len:] for h in heads]
        for h in heads:
            uv = jnp.concatenate([ub[h], vh[h]], axis=0)
            bk = jnp.concatenate([bt_end[:, hsl[h]], kt_end[:, hsl[h]]], axis=0)
            state[h] = sm[h] * p_end[:, hsl[h]] + _dot_tn(uv, bk)
        y = jnp.concatenate(ys, axis=1)

        inv_e = 1.0 / HEAD_DIM
        mu = _dot_xw(y, seg) * inv_e
        yc = y - mu
        var = _dot_xw(yc * yc, seg) * inv_e
        yn = yc * lax.rsqrt(var + GN_EPS) * lnw_ref[...] + lnb_ref[...]
        o_ref[sl, :] = (yn + bonus) * gate
        return carry

    lax.fori_loop(0, tr // c_len, chunk, 0)


def _rwkv(z, batch, seq, prm):
    t, zin = z.shape
    width = prm["w0"].shape[-1]
    tr = min(RWKV_BLOCK, seq)
    nblk = seq // tr
    n_heads = width // HEAD_DIM
    lora = DECAY_LORA + AAA_LORA
    row = lambda x: x.reshape(1, -1).astype(F32)
    zeros = jnp.zeros((AAA_LORA, width), F32)
    w2p = jnp.concatenate([prm["w2"], zeros], axis=0)
    a2p = jnp.concatenate([jnp.zeros((DECAY_LORA, width), F32), prm["a2"]], axis=0)
    hid = jnp.arange(width) // HEAD_DIM
    seg = (hid[:, None] == hid[None, :]).astype(BF16)
    tri = jnp.tril(jnp.ones((RWKV_CHUNK, RWKV_CHUNK), BF16))
    params = [row(prm["mix"]), row(prm["w0"]), w2p, row(prm["a0"]), a2p, prm["g2"].astype(BF16),
              row(prm["k_k"]), row(prm["k_a"]), row(prm["r_k"]), row(prm["ln_w"]), row(prm["ln_b"]),
              seg, tri]
    return pl.pallas_call(
        functools.partial(_rwkv_body, tr=tr, width=width),
        grid=(batch, nblk),
        in_specs=[pl.BlockSpec((tr, zin), lambda b, n: (b * nblk + n, 0))]
                 + [_resident(p.shape) for p in params],
        out_specs=pl.BlockSpec((tr, width), lambda b, n: (b * nblk + n, 0)),
        out_shape=jax.ShapeDtypeStruct((t, width), F32),
        scratch_shapes=[pltpu.VMEM((tr + 8, zin), F32), pltpu.VMEM((tr, zin), F32),
                        pltpu.VMEM((n_heads, HEAD_DIM, HEAD_DIM), F32)],
        compiler_params=_cparams("arbitrary", "arbitrary"),
        name="rwkv7_time_mix",
    )(z, *params)


def _ssm_prep_body(are_ref, aim_ref, lstep_ref, bre_ref, bim_ref, cre_ref, cim_ref,
                   kt_ref, wre_ref, wim_ref, vre_ref, vim_ref, pre_ref, pim_ref):
    lam_re = are_ref[0]
    lam_im = aim_ref[0]
    step = jnp.exp(lstep_ref[0])
    mag = jnp.exp(lam_re * step)
    ang = lam_im * step
    abar_re, abar_im = mag * jnp.cos(ang), mag * jnp.sin(ang)
    inv = 1.0 / (lam_re * lam_re + lam_im * lam_im)
    f_re = ((abar_re - 1.0) * lam_re + abar_im * lam_im) * inv
    f_im = (abar_im * lam_re - (abar_re - 1.0) * lam_im) * inv
    b_re, b_im = bre_ref[0], bim_ref[0]
    bbar_re = f_re * b_re - f_im * b_im
    bbar_im = f_re * b_im + f_im * b_re
    c_re, c_im = cre_ref[0], cim_ref[0]

    def power(tau):
        m = jnp.exp(lam_re * step * tau)
        return m * jnp.cos(ang * tau), m * jnp.sin(ang * tau)

    nt = (((1,), (1,)), ((), ()))
    for tau in range(SSM_CHUNK):
        p_re, p_im = power(float(tau))
        cp_re = c_re * p_re - c_im * p_im
        cp_im = c_re * p_im + c_im * p_re
        kt_ref[0, tau] = _dot_hi(cp_re, bbar_re, nt) - _dot_hi(cp_im, bbar_im, nt)
        q_re, q_im = power(float(SSM_CHUNK - 1 - tau))
        wre_ref[0, tau] = q_re * bbar_re - q_im * bbar_im
        wim_ref[0, tau] = q_re * bbar_im + q_im * bbar_re
        s_re, s_im = power(float(tau + 1))
        vre_ref[0, tau] = c_re * s_re - c_im * s_im
        vim_ref[0, tau] = -(c_re * s_im + c_im * s_re)
    for lvl in range(SSM_SCAN_LEVELS):
        p_re, p_im = power(float(SSM_CHUNK * 2 ** lvl))
        pre_ref[0, lvl:lvl + 1, :] = p_re
        pim_ref[0, lvl:lvl + 1, :] = p_im


def _ssm_prep(a_re, a_im, log_step, b_re, b_im, c_re, c_im):
    g, p = a_re.shape
    c = SSM_GROUP_CH
    l = SSM_CHUNK
    grp3 = lambda shape: pl.BlockSpec((1,) + shape, lambda i: (i, 0, 0))
    grp4 = lambda shape: pl.BlockSpec((1,) + shape, lambda i: (i, 0, 0, 0))
    sds = jax.ShapeDtypeStruct
    return pl.pallas_call(
        _ssm_prep_body,
        grid=(g,),
        in_specs=[grp3((1, p)), grp3((1, p)), grp3((1, 1)), grp3((c, p)), grp3((c, p)),
                  grp3((c, p)), grp3((c, p))],
        out_specs=[grp4((l, c, c)), grp4((l, c, p)), grp4((l, c, p)), grp4((l, c, p)), grp4((l, c, p)),
                   grp3((SSM_SCAN_LEVELS, p)), grp3((SSM_SCAN_LEVELS, p))],
        out_shape=[sds((g, l, c, c), F32), sds((g, l, c, p), F32), sds((g, l, c, p), F32),
                   sds((g, l, c, p), F32), sds((g, l, c, p), F32),
                   sds((g, SSM_SCAN_LEVELS, p), F32), sds((g, SSM_SCAN_LEVELS, p), F32)],
        compiler_params=_cparams("arbitrary"),
        name="ssm_discretise",
    )(a_re.reshape(g, 1, p), a_im.reshape(g, 1, p), log_step.reshape(g, 1, 1),
      jnp.swapaxes(b_re, 1, 2), jnp.swapaxes(b_im, 1, 2), c_re, c_im)


def _ssm_scan_body(u_ref, t_ref, w_ref, v_ref, m1_ref, m2_ref, d_ref, y_ref, *, n_chunks):
    u = u_ref[0]
    ub = u.astype(BF16)
    x = jnp.dot(ub, w_ref[0], preferred_element_type=F32)
    rows, two_p = x.shape
    pos = lax.broadcasted_iota(jnp.int32, (rows, two_p), 0) % n_chunks
    m1 = m1_ref[0]
    m2 = m2_ref[0]
    for lvl in range(SSM_SCAN_LEVELS):
        s = 2 ** lvl
        if s >= n_chunks:
            break
        xs = jnp.where(pos >= s, pltpu.roll(x, s, 0), 0.0)
        x = x + xs * m1[lvl:lvl + 1, :] + pltpu.roll(xs, two_p // 2, 1) * m2[lvl:lvl + 1, :]
    x_in = jnp.where(pos >= 1, pltpu.roll(x, 1, 0), 0.0)
    y = (jnp.dot(ub, t_ref[0], preferred_element_type=F32)
         + jnp.dot(x_in.astype(BF16), v_ref[0], preferred_element_type=F32)
         + u * d_ref[0])
    y_ref[0] = y


def _ssm_scan(u, tmat, wmat, vmat, m1, m2, dvec, n_chunks):
    g, rows, lc = u.shape
    blk = lambda a: pl.BlockSpec((1,) + a.shape[1:], lambda i: (i, 0, 0))
    return pl.pallas_call(
        functools.partial(_ssm_scan_body, n_chunks=n_chunks),
        grid=(g,),
        in_specs=[blk(u), blk(tmat), blk(wmat), blk(vmat), blk(m1), blk(m2), blk(dvec)],
        out_specs=blk(u),
        out_shape=jax.ShapeDtypeStruct(u.shape, F32),
        compiler_params=_cparams("arbitrary"),
        name="ssm_chunk_scan",
    )(u, tmat, wmat, vmat, m1, m2, dvec)


def _ssm(u_tok, batch, seq, prm):
    t, w = u_tok.shape
    c, l, p = SSM_GROUP_CH, SSM_CHUNK, SSM_STATE
    g = w // c
    n_chunks = seq // l
    kt, wre, wim, vre, vim, pre, pim = _ssm_prep(prm["a_re"], prm["a_im"], prm["log_step"],
                                                 prm["b_re"], prm["b_im"], prm["c_re"], prm["c_im"])
    i_out = jnp.arange(l)[None, :]
    j_in = jnp.arange(l)[:, None]
    lag = jnp.clip(i_out - j_in, 0, l - 1)
    tblk = jnp.where((i_out >= j_in)[None, :, :, None, None], kt[:, lag], 0.0)
    tmat = jnp.transpose(tblk, (0, 1, 4, 2, 3)).reshape(g, l * c, l * c).astype(BF16)
    wmat = jnp.concatenate([wre, wim], axis=-1).reshape(g, l * c, 2 * p).astype(BF16)
    vmat = jnp.concatenate([jnp.transpose(vre, (0, 3, 1, 2)), jnp.transpose(vim, (0, 3, 1, 2))],
                           axis=1).reshape(g, 2 * p, l * c).astype(BF16)
    m1 = jnp.concatenate([pre, pre], axis=-1)
    m2 = jnp.concatenate([-pim, pim], axis=-1)
    dvec = jnp.tile(prm["d"].reshape(g, 1, c), (1, 1, l))
    u = jnp.transpose(u_tok.reshape(batch, n_chunks, l, g, c), (3, 0, 1, 2, 4)).reshape(g, batch * n_chunks, l * c)
    y = _ssm_scan(u, tmat, wmat, vmat, m1, m2, dvec, n_chunks)
    return jnp.transpose(y.reshape(g, batch, n_chunks, l, c), (1, 2, 3, 0, 4)).reshape(t, w)


def _merge_body(oa_ref, or_ref, ys_ref, gate_ref, h_ref, wa_ref, wr_ref, ws_ref, wv_ref, wg_ref, wo_ref, out_ref):
    d = h_ref.shape[1]
    y = ys_ref[...]
    zg = (0.5 * y * (1.0 + jnp.tanh(math.sqrt(2.0 / math.pi) * (y + 0.044715 * (y * y * y))))).astype(BF16)
    o_ssm = (jnp.dot(zg, wv_ref[...], preferred_element_type=F32)
             * _sigmoid(jnp.dot(zg, wg_ref[...], preferred_element_type=F32)))
    merged = (gate_ref[:, 0:d] * _dot(oa_ref[...], wa_ref[...])
              + gate_ref[:, d:2 * d] * _dot(or_ref[...], wr_ref[...])
              + gate_ref[:, 2 * d:3 * d] * _dot(o_ssm, ws_ref[...]))
    out_ref[...] = h_ref[...] + _dot(merged, wo_ref[...])


def _merge(o_attn, o_rwkv, y_ssm, gates, h, weights):
    t, d = h.shape
    tm = ROW_TILE
    rows = lambda a: pl.BlockSpec((tm, a.shape[1]), lambda i: (i, 0))
    acts = [o_attn, o_rwkv, y_ssm, gates, h]
    return pl.pallas_call(
        _merge_body,
        grid=(t // tm,),
        in_specs=[rows(a) for a in acts] + [_resident(w.shape) for w in weights],
        out_specs=rows(h),
        out_shape=jax.ShapeDtypeStruct((t, d), F32),
        compiler_params=_cparams("arbitrary"),
        name="merge_out_proj",
    )(*acts, *weights)


def _ffn_body(h_ref, gain_ref, wgu_ref, wd_ref, fgain_ref, o_ref, *, hidden, final_norm):
    x = h_ref[...]
    u = _rmsnorm(x, gain_ref[...]).astype(BF16)
    acc = jnp.zeros(x.shape, F32)
    for c in range(hidden // FFN_CHUNK):
        lo = c * FFN_CHUNK
        a = jnp.dot(u, wgu_ref[:, lo:lo + FFN_CHUNK], preferred_element_type=F32)
        b = jnp.dot(u, wgu_ref[:, hidden + lo:hidden + lo + FFN_CHUNK], preferred_element_type=F32)
        hm = (a * _sigmoid(a) * b).astype(BF16)
        acc = acc + jnp.dot(hm, wd_ref[lo:lo + FFN_CHUNK, :], preferred_element_type=F32)
    y = x + acc
    if final_norm:
        y = _rmsnorm(y, fgain_ref[...])
    o_ref[...] = y


def _ffn(h, gain, wgu, wd, final_gain, final_norm):
    t, d = h.shape
    hidden = wd.shape[0]
    tm = ROW_TILE
    return pl.pallas_call(
        functools.partial(_ffn_body, hidden=hidden, final_norm=final_norm),
        grid=(t // tm,),
        in_specs=[pl.BlockSpec((tm, d), lambda i: (i, 0)), _resident((1, d)),
                  _resident(wgu.shape), _resident(wd.shape), _resident((1, d))],
        out_specs=pl.BlockSpec((tm, d), lambda i: (i, 0)),
        out_shape=jax.ShapeDtypeStruct((t, d), F32),
        compiler_params=_cparams("arbitrary"),
        name="swiglu_ffn",
    )(h, gain.reshape(1, d), wgu, wd, final_gain.reshape(1, d))


def kernel(x, norm_mix, w_in, rwkv_shift_mix, rwkv_w0, rwkv_w2, rwkv_a0, rwkv_a2, rwkv_g2, rwkv_k_k, rwkv_k_a, rwkv_r_k, rwkv_ln_w, rwkv_ln_b, ssm_a_re, ssm_a_im, ssm_log_step, ssm_b_re, ssm_b_im, ssm_c_re, ssm_c_im, ssm_d, ssm_glu_val, ssm_glu_gate, w_branch_attn, w_branch_rwkv, w_branch_ssm, w_out, norm_ffn, ffn_w_gate_up, ffn_w_down, norm_final):
    batch, seq, d = x.shape
    depth = w_in.shape[0]
    attn_w = w_branch_attn.shape[1]
    rwkv_in = rwkv_shift_mix.shape[1]
    ssm_w = w_branch_ssm.shape[1]
    widths = (3 * attn_w, rwkv_in, ssm_w, N_BRANCH * d)
    assert sum(widths) == w_in.shape[2]
    assert attn_w == len(ATTN_GROUPS) * HEADS_PER_GROUP * HEAD_DIM
    assert seq % ATTN_BLOCK == 0 or seq in (w for w, _ in ATTN_GROUPS)
    assert seq // SSM_CHUNK <= 2 ** SSM_SCAN_LEVELS and (batch * seq) % ROW_TILE == 0

    h = x.reshape(batch * seq, d)
    for l in range(depth):
        qkv, z_rwkv, u_ssm, gates = _in_proj(h, norm_mix[l], w_in[l].astype(BF16), widths)
        o_attn = _attention(qkv, batch, seq)
        o_rwkv = _rwkv(z_rwkv, batch, seq, dict(
            mix=rwkv_shift_mix[l], w0=rwkv_w0[l], w2=rwkv_w2[l], a0=rwkv_a0[l], a2=rwkv_a2[l],
            g2=rwkv_g2[l], k_k=rwkv_k_k[l], k_a=rwkv_k_a[l], r_k=rwkv_r_k[l],
            ln_w=rwkv_ln_w[l], ln_b=rwkv_ln_b[l]))
        y_ssm = _ssm(u_ssm, batch, seq, dict(
            a_re=ssm_a_re[l], a_im=ssm_a_im[l], log_step=ssm_log_step[l], b_re=ssm_b_re[l],
            b_im=ssm_b_im[l], c_re=ssm_c_re[l], c_im=ssm_c_im[l], d=ssm_d[l]))
        bf = lambda w: w.astype(BF16)
        h = _merge(o_attn, o_rwkv, y_ssm, gates, h,
                   [bf(w_branch_attn[l]), bf(w_branch_rwkv[l]), bf(w_branch_ssm[l]),
                    bf(ssm_glu_val[l]), bf(ssm_glu_gate[l]), bf(w_out[l])])
        last = l == depth - 1
        h = _ffn(h, norm_ffn[l], bf(ffn_w_gate_up[l]), bf(ffn_w_down[l]), norm_final, last)
    return h.reshape(batch, seq, d)
```

```python
import functools
import math

import jax
import jax.numpy as jnp
from jax import lax
from jax.experimental import pallas as pl
from jax.experimental.pallas import tpu as pltpu

F32 = jnp.float32
BF16 = jnp.bfloat16

HEAD_DIM = 64
ATTN_GROUPS = ((128, 1), (512, 4), (2048, 16))
ATTN_N_BACK = 128
HEADS_PER_GROUP = 2
DECAY_LORA = 64
AAA_LORA = 64
GATE_LORA = 128
SSM_GROUP_CH = 16
SSM_STATE = 64
GN_EPS = 64e-5
NORM_EPS = 1e-6
N_BRANCH = 3

V7X_LANES = 128
V7X_VMEM_BYTES = 64 * 1024 * 1024
VMEM_LIMIT = V7X_VMEM_BYTES - 8 * 1024 * 1024

ROW_TILE = 512
COL_CHUNK = 512
FFN_CHUNK = 256
ATTN_BLOCK = 2048
ATTN_UNROLL = 2
RWKV_BLOCK = 512
RWKV_CHUNK = 64
RWKV_SEQS = 2
SSM_CHUNK = 16
SSM_SCAN_LEVELS = 8


def _cparams(*sem):
    return pltpu.CompilerParams(dimension_semantics=sem, vmem_limit_bytes=VMEM_LIMIT)


def _resident(shape):
    nd = len(shape)
    return pl.BlockSpec(shape, lambda *_: (0,) * nd, pipeline_mode=pl.Buffered(1))


def _dot(a, b):
    return jnp.dot(a.astype(BF16), b.astype(BF16), preferred_element_type=F32)


def _dot_nt(a, b):
    return lax.dot_general(a.astype(BF16), b.astype(BF16), (((1,), (1,)), ((), ())),
                           preferred_element_type=F32)


def _dot_tn(a, b):
    return lax.dot_general(a.astype(BF16), b.astype(BF16), (((0,), (0,)), ((), ())),
                           preferred_element_type=F32)


def _split3(x):
    h1 = x.astype(BF16)
    r1 = x - h1.astype(F32)
    h2 = r1.astype(BF16)
    h3 = (r1 - h2.astype(F32)).astype(BF16)
    return h1, h2, h3


def _dot_xw(x, w):
    m = x.shape[0]
    y = jnp.dot(jnp.concatenate(_split3(x), axis=0), w, preferred_element_type=F32)
    return y[0:m] + y[m:2 * m] + y[2 * m:3 * m]


def _dot_wx(w, x):
    n = x.shape[1]
    y = jnp.dot(w, jnp.concatenate(_split3(x), axis=1), preferred_element_type=F32)
    return y[:, 0:n] + y[:, n:2 * n] + y[:, 2 * n:3 * n]


def _dot_hi(a, b, dims=(((1,), (0,)), ((), ()))):
    ah = a.astype(BF16)
    al = (a - ah.astype(F32)).astype(BF16)
    bh = b.astype(BF16)
    bl = (b - bh.astype(F32)).astype(BF16)
    dg = functools.partial(lax.dot_general, dimension_numbers=dims, preferred_element_type=F32)
    return dg(ah, bh) + dg(ah, bl) + dg(al, bh)


def _rmsnorm(x, gain):
    ms = jnp.mean(x * x, axis=-1, keepdims=True)
    return x * lax.rsqrt(ms + NORM_EPS) * gain


def _sigmoid(x):
    return 1.0 / (1.0 + jnp.exp(-x))


def _in_proj_body(h_ref, gain_ref, w_ref, qkv_ref, rw_ref, ssm_ref, gate_ref, *, widths):
    attn_in, rwkv_in, ssm_w, gate_w = widths
    u = _rmsnorm(h_ref[...], gain_ref[...]).astype(BF16)

    rw_off = attn_in
    ssm_off = rw_off + rwkv_in
    gate_off = ssm_off + ssm_w
    total = gate_off + gate_w

    def store(col, y):
        if col < rw_off:
            qkv_ref[col // V7X_LANES] = y
        elif col < ssm_off:
            rw_ref[:, col - rw_off:col - rw_off + V7X_LANES] = y
        elif col < gate_off:
            ssm_ref[(col - ssm_off) // V7X_LANES] = y
        else:
            gate_ref[:, col - gate_off:col - gate_off + V7X_LANES] = _sigmoid(y).astype(gate_ref.dtype)

    for lo in range(0, total, COL_CHUNK):
        hi = min(lo + COL_CHUNK, total)
        y = jnp.dot(u, w_ref[:, lo:hi], preferred_element_type=F32)
        for j in range((hi - lo) // V7X_LANES):
            store(lo + j * V7X_LANES, y[:, j * V7X_LANES:(j + 1) * V7X_LANES])


def _in_proj(h, gain, w_bf16, widths):
    t, d = h.shape
    attn_in, rwkv_in, ssm_w, gate_w = widths
    n_slab = attn_in // V7X_LANES
    tm = ROW_TILE
    return pl.pallas_call(
        functools.partial(_in_proj_body, widths=widths),
        grid=(t // tm,),
        in_specs=[pl.BlockSpec((tm, d), lambda i: (i, 0)),
                  _resident((1, d)),
                  _resident(w_bf16.shape)],
        out_specs=[pl.BlockSpec((n_slab, tm, V7X_LANES), lambda i: (0, i, 0)),
                   pl.BlockSpec((tm, rwkv_in), lambda i: (i, 0)),
                   pl.BlockSpec((ssm_w // V7X_LANES, tm, V7X_LANES), lambda i: (0, i, 0)),
                   pl.BlockSpec((tm, gate_w), lambda i: (i, 0))],
        out_shape=[jax.ShapeDtypeStruct((n_slab, t, V7X_LANES), F32),
                   jax.ShapeDtypeStruct((t, rwkv_in), F32),
                   jax.ShapeDtypeStruct((ssm_w // V7X_LANES, t, V7X_LANES), F32),
                   jax.ShapeDtypeStruct((t, gate_w), BF16)],
        compiler_params=_cparams("arbitrary"),
        name="in_proj",
    )(h, gain.reshape(1, d), w_bf16)


def _attn_body(qkv_ref, o_ref, kb0, vb0, kb1, vb1, kb2, vb2, obuf, lbuf, *, tb):
    n = pl.program_id(1)
    nb = ATTN_N_BACK
    kbufs, vbufs = (kb0, kb1, kb2), (vb0, vb1, vb2)
    n_grp = len(ATTN_GROUPS)
    lane = lax.broadcasted_iota(jnp.int32, (nb, V7X_LANES), 1)
    head0 = lane < HEAD_DIM
    qi = lax.broadcasted_iota(jnp.int32, (nb, 2 * nb), 0)
    kj = lax.broadcasted_iota(jnp.int32, (nb, 2 * nb), 1)
    band = (kj >= qi) & (kj <= qi + nb)
    scale = HEAD_DIM ** -0.5

    for g, (window, dil) in enumerate(ATTN_GROUPS):
        span = window
        kb, vb = kbufs[g], vbufs[g]

        @pl.when(n == 0)
        def _():
            kb[0:span, :] = jnp.zeros((span, V7X_LANES), F32)
            vb[0:span, :] = jnp.zeros((span, V7X_LANES), F32)

        @pl.when(n > 0)
        def _():
            kb[0:span, :] = kb[tb:tb + span, :]
            vb[0:span, :] = vb[tb:tb + span, :]

        kb[span:span + tb, :] = qkv_ref[n_grp + g]
        vb[span:span + tb, :] = qkv_ref[2 * n_grp + g]
        n_sub = tb // span

        def units(it, carry, g=g, dil=dil, span=span, kb=kb, vb=vb, n_sub=n_sub):
            bases, masks, qs, k2s, v2s = [], [], [], [], []
            for k in range(ATTN_UNROLL):
                idx = it * ATTN_UNROLL + k
                j = idx // dil
                base = j * span + (idx - j * dil)
                bases.append(base)
                masks.append(band & ((kj >= nb) | ((n * n_sub + j) > 0)))
                qs.append(qkv_ref[g, pl.ds(base, nb, stride=dil), :] * scale)
                k2s.append(kb[pl.ds(base, 2 * nb, stride=dil), :].astype(BF16))
                v2s.append(vb[pl.ds(base, 2 * nb, stride=dil), :].astype(BF16))
            scores = [[_dot_nt(jnp.where(hmask, qs[k], 0.0), k2s[k]) for hmask in (head0, ~head0)]
                      for k in range(ATTN_UNROLL)]
            probs, dens, lses = [], [], []
            for k in range(ATTN_UNROLL):
                for s in scores[k]:
                    s = jnp.where(masks[k], s, -1e30)
                    mx = jnp.max(s, axis=-1, keepdims=True)
                    p = jnp.exp(s - mx)
                    den = jnp.sum(p, axis=-1, keepdims=True)
                    probs.append(p.astype(BF16))
                    dens.append(den)
                    lses.append(mx + jnp.log(den))
            for k in range(ATTN_UNROLL):
                o0 = jnp.dot(probs[2 * k], v2s[k], preferred_element_type=F32) / dens[2 * k]
                o1 = jnp.dot(probs[2 * k + 1], v2s[k], preferred_element_type=F32) / dens[2 * k + 1]
                obuf[g, pl.ds(bases[k], nb, stride=dil), :] = jnp.where(head0, o0, o1)
                lbuf[g, pl.ds(bases[k], nb, stride=dil), :] = jnp.where(head0, lses[2 * k], lses[2 * k + 1])
            return carry

        lax.fori_loop(0, n_sub * dil // ATTN_UNROLL, units, 0)

    rows = 256
    def mix(i, carry):
        sl = pl.ds(pl.multiple_of(i * rows, rows), rows)
        ls = [lbuf[g, sl, :] for g in range(n_grp)]
        m = functools.reduce(jnp.maximum, ls)
        es = [jnp.exp(l - m) for l in ls]
        tot = functools.reduce(lambda a, b: a + b, es)
        for g in range(n_grp):
            o_ref[sl, g * V7X_LANES:(g + 1) * V7X_LANES] = (obuf[g, sl, :] * (es[g] / tot)).astype(o_ref.dtype)
        return carry
    lax.fori_loop(0, tb // rows, mix, 0)


def _attention(qkv, batch, seq):
    n_slab, t, _ = qkv.shape
    tb = min(ATTN_BLOCK, seq)
    nblk = seq // tb
    n_grp = len(ATTN_GROUPS)
    scratch = []
    for window, _ in ATTN_GROUPS:
        scratch += [pltpu.VMEM((window + tb, V7X_LANES), F32)] * 2
    scratch += [pltpu.VMEM((n_grp, tb, V7X_LANES), F32)] * 2
    return pl.pallas_call(
        functools.partial(_attn_body, tb=tb),
        grid=(batch, nblk),
        in_specs=[pl.BlockSpec((n_slab, tb, V7X_LANES), lambda b, n: (0, b * nblk + n, 0))],
        out_specs=pl.BlockSpec((tb, n_grp * V7X_LANES), lambda b, n: (b * nblk + n, 0)),
        out_shape=jax.ShapeDtypeStruct((t, n_grp * V7X_LANES), BF16),
        scratch_shapes=scratch,
        compiler_params=_cparams("arbitrary", "arbitrary"),
        name="dilated_attention",
    )(qkv)


def _rwkv_body(z_ref, mix_ref, w0_ref, w2_ref, a0_ref, a2_ref, g2_ref, kk_ref, ka_ref, rk_ref,
               lnw_ref, lnb_ref, seg_ref, tri_ref, o_ref, zbuf, zz_s, state, *, tr, width):
    n = pl.program_id(1)
    nseq = z_ref.shape[0]
    c_len = RWKV_CHUNK
    n_heads = width // HEAD_DIM
    pad = 8

    @pl.when(n == 0)
    def _():
        zbuf[:, 0:pad, :] = jnp.zeros((nseq, pad, zbuf.shape[2]), F32)
        state[...] = jnp.zeros(state.shape, F32)

    @pl.when(n > 0)
    def _():
        zbuf[:, 0:pad, :] = zbuf[:, tr:tr + pad, :]

    zbuf[:, pad:pad + tr, :] = z_ref[...]
    mix = mix_ref[...]
    for q in range(nseq):
        for i in range(tr // c_len):
            cur = zbuf[q, pad + i * c_len:pad + (i + 1) * c_len, :]
            prev = zbuf[q, pad - 1 + i * c_len:pad - 1 + (i + 1) * c_len, :]
            zz_s[q, i * c_len:(i + 1) * c_len, :] = cur + (prev - cur) * mix

    row = lax.broadcasted_iota(jnp.int32, (c_len, c_len), 0)
    col = lax.broadcasted_iota(jnp.int32, (c_len, c_len), 1)
    strict = col < row
    incl = col <= row
    eye = (col == row).astype(F32)
    seg = seg_ref[...]
    tri = tri_ref[...]
    lora = DECAY_LORA + AAA_LORA
    n_sq = int(math.log2(c_len)) - 1
    rsl = [slice(q * c_len, (q + 1) * c_len) for q in range(nseq)]
    hsl = [slice(h * HEAD_DIM, (h + 1) * HEAD_DIM) for h in range(n_heads)]
    units = [(q, h) for q in range(nseq) for h in range(n_heads)]
    n_units = range(len(units))

    def chunk(c, carry):
        sl = pl.ds(pl.multiple_of(c * c_len, c_len), c_len)
        zz = jnp.concatenate([zz_s[q, sl, :] for q in range(nseq)], axis=0)
        r = zz[:, 0:width]
        k = zz[:, width:2 * width]
        v = zz[:, 2 * width:3 * width]
        t = zz[:, 3 * width:3 * width + lora]
        xg = zz[:, 3 * width + lora:3 * width + lora + GATE_LORA]

        wpre = w0_ref[...] + _dot_hi(jnp.tanh(t), w2_ref[...])
        nw = -wpre
        softplus = jnp.maximum(nw, 0.0) + jnp.log(1.0 + jnp.exp(-jnp.abs(nw)))
        lw = -jnp.exp(-softplus - 0.5)
        a = _sigmoid(a0_ref[...] + _dot_hi(t, a2_ref[...]))
        gate = _dot(_sigmoid(xg), g2_ref[...])

        kk = k * kk_ref[...]
        nrm = jnp.sqrt(_dot_xw(kk * kk, seg))
        kk = kk / jnp.maximum(nrm, 1e-12)
        k = k * (1.0 + (a - 1.0) * ka_ref[...])
        bonus = _dot_xw(r * k * rk_ref[...], seg) * v
        avec = -kk
        bvec = kk * a

        lc = _dot_wx(tri, lw)
        lc_last = [lc[s.stop - 1:s.stop, :] for s in rsl]
        lc_end = jnp.concatenate([jnp.broadcast_to(e, (c_len, width)) for e in lc_last], axis=0)
        e_pos = jnp.exp(lc)
        e_neg = jnp.exp(-lc)
        e_end = jnp.exp(lc_end - lc)
        rt = (r * e_pos).astype(BF16)
        at = (avec * jnp.exp(lc - lw)).astype(BF16)
        kt = (k * e_neg).astype(BF16)
        bt = (bvec * e_neg).astype(BF16)
        kt_end = (k * e_end).astype(BF16)
        bt_end = (bvec * e_end).astype(BF16)
        p_end = [jnp.exp(e) for e in lc_last]
        vb = v.astype(BF16)

        blk = lambda x, i: x[rsl[units[i][0]], hsl[units[i][1]]]
        sm = [state[i] for i in n_units]
        smb = [m.astype(BF16) for m in sm]
        ar = [jnp.concatenate([blk(at, i), blk(rt, i)], axis=0) for i in n_units]
        vh = [blk(vb, i) for i in n_units]
        gb = [_dot_nt(ar[i], blk(bt, i)) for i in n_units]
        gk = [_dot_nt(ar[i], blk(kt, i)) for i in n_units]
        lmat = [jnp.where(strict, gb[i][0:c_len], 0.0) for i in n_units]
        rb = [jnp.where(incl, gb[i][c_len:], 0.0) for i in n_units]
        akrk = [jnp.concatenate([jnp.where(strict, gk[i][0:c_len], 0.0),
                                 jnp.where(incl, gk[i][c_len:], 0.0)], axis=0) for i in n_units]
        pw = [_dot(lmat[i], lmat[i]) for i in n_units]
        tinv = [eye + lmat[i] for i in n_units]
        s0 = [_dot_nt(ar[i], smb[i]) for i in n_units]
        kv = [_dot(akrk[i], vh[i]) for i in n_units]
        for _ in range(n_sq - 1):
            pw_next = [_dot(pw[i], pw[i]) for i in n_units]
            tinv = [tinv[i] + _dot(tinv[i], pw[i]) for i in n_units]
            pw = pw_next
        tinv = [tinv[i] + _dot(tinv[i], pw[i]) for i in n_units]
        ub = [_dot(tinv[i], s0[i][0:c_len] + kv[i][0:c_len]).astype(BF16) for i in n_units]
        ys = [s0[i][c_len:] + _dot(rb[i], ub[i]) + kv[i][c_len:] for i in n_units]
        for i in n_units:
            q, h = units[i]
            uv = jnp.concatenate([ub[i], vh[i]], axis=0)
            bk = jnp.concatenate([blk(bt_end, i), blk(kt_end, i)], axis=0)
            state[i] = sm[i] * p_end[q][:, hsl[h]] + _dot_tn(uv, bk)
        y = jnp.concatenate([jnp.concatenate(ys[q * n_heads:(q + 1) * n_heads], axis=1)
                             for q in range(nseq)], axis=0)

        inv_e = 1.0 / HEAD_DIM
        mu = _dot_xw(y, seg) * inv_e
        yc = y - mu
        var = _dot_xw(yc * yc, seg) * inv_e
        yn = yc * lax.rsqrt(var + GN_EPS) * lnw_ref[...] + lnb_ref[...]
        out = ((yn + bonus) * gate).astype(o_ref.dtype)
        for q in range(nseq):
            o_ref[q, sl, :] = out[rsl[q], :]
        return carry

    lax.fori_loop(0, tr // c_len, chunk, 0)


def _rwkv(z, batch, seq, prm):
    t, zin = z.shape
    width = prm["w0"].shape[-1]
    tr = min(RWKV_BLOCK, seq)
    nblk = seq // tr
    nseq = RWKV_SEQS if batch % RWKV_SEQS == 0 else 1
    n_heads = width // HEAD_DIM
    row = lambda x: x.reshape(1, -1).astype(F32)
    w2p = jnp.concatenate([prm["w2"], jnp.zeros((AAA_LORA, width), F32)], axis=0)
    a2p = jnp.concatenate([jnp.zeros((DECAY_LORA, width), F32), prm["a2"]], axis=0)
    hid = jnp.arange(width) // HEAD_DIM
    seg = (hid[:, None] == hid[None, :]).astype(BF16)
    pos = jnp.arange(nseq * RWKV_CHUNK)
    tri = ((pos[:, None] >= pos[None, :])
           & (pos[:, None] // RWKV_CHUNK == pos[None, :] // RWKV_CHUNK)).astype(BF16)
    params = [row(prm["mix"]), row(prm["w0"]), w2p, row(prm["a0"]), a2p, prm["g2"].astype(BF16),
              row(prm["k_k"]), row(prm["k_a"]), row(prm["r_k"]), row(prm["ln_w"]), row(prm["ln_b"]),
              seg, tri]
    out = pl.pallas_call(
        functools.partial(_rwkv_body, tr=tr, width=width),
        grid=(batch // nseq, nblk),
        in_specs=[pl.BlockSpec((nseq, tr, zin), lambda b, n: (b, n, 0))]
                 + [_resident(p.shape) for p in params],
        out_specs=pl.BlockSpec((nseq, tr, width), lambda b, n: (b, n, 0)),
        out_shape=jax.ShapeDtypeStruct((batch, seq, width), BF16),
        scratch_shapes=[pltpu.VMEM((nseq, tr + 8, zin), F32), pltpu.VMEM((nseq, tr, zin), F32),
                        pltpu.VMEM((nseq * n_heads, HEAD_DIM, HEAD_DIM), F32)],
        compiler_params=_cparams("arbitrary", "arbitrary"),
        name="rwkv7_time_mix",
    )(z.reshape(batch, seq, zin), *params)
    return out.reshape(t, width)


def _ssm_prep_body(are_ref, aim_ref, lstep_ref, bre_ref, bim_ref, cre_ref, cim_ref,
                   kt_ref, wre_ref, wim_ref, vre_ref, vim_ref, pre_ref, pim_ref):
    lam_re = are_ref[0]
    lam_im = aim_ref[0]
    step = jnp.exp(lstep_ref[0])
    mag = jnp.exp(lam_re * step)
    ang = lam_im * step
    abar_re, abar_im = mag * jnp.cos(ang), mag * jnp.sin(ang)
    inv = 1.0 / (lam_re * lam_re + lam_im * lam_im)
    f_re = ((abar_re - 1.0) * lam_re + abar_im * lam_im) * inv
    f_im = (abar_im * lam_re - (abar_re - 1.0) * lam_im) * inv
    b_re, b_im = bre_ref[0], bim_ref[0]
    bbar_re = f_re * b_re - f_im * b_im
    bbar_im = f_re * b_im + f_im * b_re
    c_re, c_im = cre_ref[0], cim_ref[0]

    def power(tau):
        m = jnp.exp(lam_re * step * tau)
        return m * jnp.cos(ang * tau), m * jnp.sin(ang * tau)

    nt = (((1,), (1,)), ((), ()))
    for tau in range(SSM_CHUNK):
        p_re, p_im = power(float(tau))
        cp_re = c_re * p_re - c_im * p_im
        cp_im = c_re * p_im + c_im * p_re
        kt_ref[0, tau] = _dot_hi(cp_re, bbar_re, nt) - _dot_hi(cp_im, bbar_im, nt)
        q_re, q_im = power(float(SSM_CHUNK - 1 - tau))
        wre_ref[0, tau] = q_re * bbar_re - q_im * bbar_im
        wim_ref[0, tau] = q_re * bbar_im + q_im * bbar_re
        s_re, s_im = power(float(tau + 1))
        vre_ref[0, tau] = c_re * s_re - c_im * s_im
        vim_ref[0, tau] = -(c_re * s_im + c_im * s_re)
    for lvl in range(SSM_SCAN_LEVELS):
        p_re, p_im = power(float(SSM_CHUNK * 2 ** lvl))
        pre_ref[0, lvl:lvl + 1, :] = p_re
        pim_ref[0, lvl:lvl + 1, :] = p_im


def _ssm_prep(a_re, a_im, log_step, b_re, b_im, c_re, c_im):
    g, p = a_re.shape
    c = SSM_GROUP_CH
    l = SSM_CHUNK
    grp3 = lambda shape: pl.BlockSpec((1,) + shape, lambda i: (i, 0, 0))
    grp4 = lambda shape: pl.BlockSpec((1,) + shape, lambda i: (i, 0, 0, 0))
    sds = jax.ShapeDtypeStruct
    return pl.pallas_call(
        _ssm_prep_body,
        grid=(g,),
        in_specs=[grp3((1, p)), grp3((1, p)), grp3((1, 1)), grp3((c, p)), grp3((c, p)),
                  grp3((c, p)), grp3((c, p))],
        out_specs=[grp4((l, c, c)), grp4((l, c, p)), grp4((l, c, p)), grp4((l, c, p)), grp4((l, c, p)),
                   grp3((SSM_SCAN_LEVELS, p)), grp3((SSM_SCAN_LEVELS, p))],
        out_shape=[sds((g, l, c, c), F32), sds((g, l, c, p), F32), sds((g, l, c, p), F32),
                   sds((g, l, c, p), F32), sds((g, l, c, p), F32),
                   sds((g, SSM_SCAN_LEVELS, p), F32), sds((g, SSM_SCAN_LEVELS, p), F32)],
        compiler_params=_cparams("arbitrary"),
        name="ssm_discretise",
    )(a_re.reshape(g, 1, p), a_im.reshape(g, 1, p), log_step.reshape(g, 1, 1),
      jnp.swapaxes(b_re, 1, 2), jnp.swapaxes(b_im, 1, 2), c_re, c_im)


def _ssm_scan_body(u_ref, kd_ref, wd_ref, vd_ref, mre_ref, mim_ref, d_ref, y_ref, *, n_chunks):
    l = SSM_CHUNK
    n_half = u_ref.shape[0]
    row = lax.broadcasted_iota(jnp.int32, (n_chunks, 1), 0)
    for h in range(n_half):
        x32 = [u_ref[h, pl.ds(j, n_chunks, stride=l), :] for j in range(l)]
        xb = [x.astype(BF16) for x in x32]
        z = jnp.dot(jnp.concatenate(xb, axis=1), wd_ref[h], preferred_element_type=F32)
        half = z.shape[1] // 2
        z_re, z_im = z[:, :half], z[:, half:]
        for lvl in range(SSM_SCAN_LEVELS):
            s = 2 ** lvl
            if s >= n_chunks:
                break
            s_re = jnp.where(row >= s, pltpu.roll(z_re, s, 0), 0.0)
            s_im = jnp.where(row >= s, pltpu.roll(z_im, s, 0), 0.0)
            a_re = mre_ref[h, lvl:lvl + 1, :]
            a_im = mim_ref[h, lvl:lvl + 1, :]
            z_re, z_im = z_re + s_re * a_re - s_im * a_im, z_im + s_re * a_im + s_im * a_re
        zin = jnp.concatenate([jnp.where(row >= 1, pltpu.roll(z_re, 1, 0), 0.0),
                               jnp.where(row >= 1, pltpu.roll(z_im, 1, 0), 0.0)], axis=1)
        y_state = jnp.dot(zin.astype(BF16), vd_ref[h], preferred_element_type=F32)
        dvec = d_ref[h]
        lanes = x32[0].shape[1]
        for i in range(l):
            lags = jnp.concatenate([xb[i - tau] for tau in range(i + 1)], axis=1)
            y = (jnp.dot(lags, kd_ref[h, 0:(i + 1) * lanes, :], preferred_element_type=F32)
                 + y_state[:, i * lanes:(i + 1) * lanes] + x32[i] * dvec)
            y_ref[h, pl.ds(i, n_chunks, stride=l), :] = y


def _ssm_scan(u, kd, wd, vd, m_re, m_im, dvec, batch, seq):
    n_half, t, lanes = u.shape
    n_chunks = seq // SSM_CHUNK
    weights = [kd, wd, vd, m_re, m_im, dvec]
    tok = pl.BlockSpec((n_half, seq, lanes), lambda b: (0, b, 0))
    return pl.pallas_call(
        functools.partial(_ssm_scan_body, n_chunks=n_chunks),
        grid=(batch,),
        in_specs=[tok] + [_resident(w.shape) for w in weights],
        out_specs=tok,
        out_shape=jax.ShapeDtypeStruct(u.shape, F32),
        compiler_params=_cparams("arbitrary"),
        name="ssm_chunk_scan",
    )(u, *weights)


def _ssm(u_slabs, batch, seq, prm):
    n_half, t, lanes = u_slabs.shape
    c, l, p = SSM_GROUP_CH, SSM_CHUNK, SSM_STATE
    gh = lanes // c
    kt, wre, wim, vre, vim, pre, pim = _ssm_prep(prm["a_re"], prm["a_im"], prm["log_step"],
                                                 prm["b_re"], prm["b_im"], prm["c_re"], prm["c_im"])
    eye = jnp.eye(gh, dtype=F32)
    kd = jnp.einsum('hgtcd,gk->htgdkc', kt.reshape(n_half, gh, l, c, c), eye)
    kd = kd.reshape(n_half, l * lanes, lanes).astype(BF16)
    wd = jnp.einsum('rhgjdp,gk->hjgdrkp', jnp.stack([wre, wim]).reshape(2, n_half, gh, l, c, p), eye)
    wd = wd.reshape(n_half, l * lanes, 2 * gh * p).astype(BF16)
    vd = jnp.einsum('rhgicp,gk->hrgpikc', jnp.stack([vre, vim]).reshape(2, n_half, gh, l, c, p), eye)
    vd = vd.reshape(n_half, 2 * gh * p, l * lanes).astype(BF16)
    lv = SSM_SCAN_LEVELS
    m_re = jnp.transpose(pre.reshape(n_half, gh, lv, p), (0, 2, 1, 3)).reshape(n_half, lv, gh * p)
    m_im = jnp.transpose(pim.reshape(n_half, gh, lv, p), (0, 2, 1, 3)).reshape(n_half, lv, gh * p)
    dvec = prm["d"].reshape(n_half, 1, lanes).astype(F32)
    return _ssm_scan(u_slabs, kd, wd, vd, m_re, m_im, dvec, batch, seq)


def _merge_body(oa_ref, or_ref, ys_ref, gate_ref, h_ref, wa_ref, wr_ref, ws_ref, wv_ref, wg_ref, wo_ref, out_ref):
    d = h_ref.shape[1]
    y = jnp.concatenate([ys_ref[i] for i in range(ys_ref.shape[0])], axis=1)
    zg = (0.5 * y * (1.0 + jnp.tanh(math.sqrt(2.0 / math.pi) * (y + 0.044715 * (y * y * y))))).astype(BF16)
    o_ssm = (jnp.dot(zg, wv_ref[...], preferred_element_type=F32)
             * _sigmoid(jnp.dot(zg, wg_ref[...], preferred_element_type=F32)))
    gate = lambda i: gate_ref[:, i * d:(i + 1) * d].astype(F32)
    merged = (gate(0) * _dot(oa_ref[...], wa_ref[...])
              + gate(1) * _dot(or_ref[...], wr_ref[...])
              + gate(2) * _dot(o_ssm, ws_ref[...]))
    out_ref[...] = h_ref[...] + _dot(merged, wo_ref[...])


def _merge(o_attn, o_rwkv, y_ssm, gates, h, weights):
    t, d = h.shape
    tm = ROW_TILE
    rows = lambda a: pl.BlockSpec((tm, a.shape[1]), lambda i: (i, 0))
    slabs = lambda a: pl.BlockSpec((a.shape[0], tm, a.shape[2]), lambda i: (0, i, 0))
    acts = [o_attn, o_rwkv, y_ssm, gates, h]
    return pl.pallas_call(
        _merge_body,
        grid=(t // tm,),
        in_specs=[rows(o_attn), rows(o_rwkv), slabs(y_ssm), rows(gates), rows(h)]
                 + [_resident(w.shape) for w in weights],
        out_specs=rows(h),
        out_shape=jax.ShapeDtypeStruct((t, d), F32),
        compiler_params=_cparams("arbitrary"),
        name="merge_out_proj",
    )(*acts, *weights)


def _ffn_body(h_ref, gain_ref, wgu_ref, wd_ref, fgain_ref, o_ref, *, hidden, final_norm):
    x = h_ref[...]
    u = _rmsnorm(x, gain_ref[...]).astype(BF16)
    acc = jnp.zeros(x.shape, F32)
    for c in range(hidden // FFN_CHUNK):
        lo = c * FFN_CHUNK
        a = jnp.dot(u, wgu_ref[:, lo:lo + FFN_CHUNK], preferred_element_type=F32)
        b = jnp.dot(u, wgu_ref[:, hidden + lo:hidden + lo + FFN_CHUNK], preferred_element_type=F32)
        hm = (a * _sigmoid(a) * b).astype(BF16)
        acc = acc + jnp.dot(hm, wd_ref[lo:lo + FFN_CHUNK, :], preferred_element_type=F32)
    y = x + acc
    if final_norm:
        y = _rmsnorm(y, fgain_ref[...])
    o_ref[...] = y


def _ffn(h, gain, wgu, wd, final_gain, final_norm):
    t, d = h.shape
    hidden = wd.shape[0]
    tm = ROW_TILE
    return pl.pallas_call(
        functools.partial(_ffn_body, hidden=hidden, final_norm=final_norm),
        grid=(t // tm,),
        in_specs=[pl.BlockSpec((tm, d), lambda i: (i, 0)), _resident((1, d)),
                  _resident(wgu.shape), _resident(wd.shape), _resident((1, d))],
        out_specs=pl.BlockSpec((tm, d), lambda i: (i, 0)),
        out_shape=jax.ShapeDtypeStruct((t, d), F32),
        compiler_params=_cparams("arbitrary"),
        name="swiglu_ffn",
    )(h, gain.reshape(1, d), wgu, wd, final_gain.reshape(1, d))


def kernel(x, norm_mix, w_in, rwkv_shift_mix, rwkv_w0, rwkv_w2, rwkv_a0, rwkv_a2, rwkv_g2, rwkv_k_k, rwkv_k_a, rwkv_r_k, rwkv_ln_w, rwkv_ln_b, ssm_a_re, ssm_a_im, ssm_log_step, ssm_b_re, ssm_b_im, ssm_c_re, ssm_c_im, ssm_d, ssm_glu_val, ssm_glu_gate, w_branch_attn, w_branch_rwkv, w_branch_ssm, w_out, norm_ffn, ffn_w_gate_up, ffn_w_down, norm_final):
    batch, seq, d = x.shape
    depth = w_in.shape[0]
    attn_w = w_branch_attn.shape[1]
    rwkv_in = rwkv_shift_mix.shape[1]
    ssm_w = w_branch_ssm.shape[1]
    widths = (3 * attn_w, rwkv_in, ssm_w, N_BRANCH * d)
    assert sum(widths) == w_in.shape[2]
    assert attn_w == len(ATTN_GROUPS) * HEADS_PER_GROUP * HEAD_DIM
    assert seq % ATTN_BLOCK == 0 or seq in (w for w, _ in ATTN_GROUPS)
    assert seq // SSM_CHUNK <= 2 ** SSM_SCAN_LEVELS and (batch * seq) % ROW_TILE == 0

    h = x.reshape(batch * seq, d)
    for l in range(depth):
        qkv, z_rwkv, u_ssm, gates = _in_proj(h, norm_mix[l], w_in[l].astype(BF16), widths)
        o_attn = _attention(qkv, batch, seq)
        o_rwkv = _rwkv(z_rwkv, batch, seq, dict(
            mix=rwkv_shift_mix[l], w0=rwkv_w0[l], w2=rwkv_w2[l], a0=rwkv_a0[l], a2=rwkv_a2[l],
            g2=rwkv_g2[l], k_k=rwkv_k_k[l], k_a=rwkv_k_a[l], r_k=rwkv_r_k[l],
            ln_w=rwkv_ln_w[l], ln_b=rwkv_ln_b[l]))
        y_ssm = _ssm(u_ssm, batch, seq, dict(
            a_re=ssm_a_re[l], a_im=ssm_a_im[l], log_step=ssm_log_step[l], b_re=ssm_b_re[l],
            b_im=ssm_b_im[l], c_re=ssm_c_re[l], c_im=ssm_c_im[l], d=ssm_d[l]))
        bf = lambda w: w.astype(BF16)
        h = _merge(o_attn, o_rwkv, y_ssm, gates, h,
                   [bf(w_branch_attn[l]), bf(w_branch_rwkv[l]), bf(w_branch_ssm[l]),
                    bf(ssm_glu_val[l]), bf(ssm_glu_gate[l]), bf(w_out[l])])
        last = l == depth - 1
        h = _ffn(h, norm_ffn[l], bf(ffn_w_gate_up[l]), bf(ffn_w_down[l]), norm_final, last)
    return h.reshape(batch, seq, d)
```

```python
import functools
import math

import jax
import jax.numpy as jnp
from jax import lax
from jax.experimental import pallas as pl
from jax.experimental.pallas import tpu as pltpu

F32 = jnp.float32
BF16 = jnp.bfloat16

HEAD_DIM = 64
ATTN_GROUPS = ((128, 1), (512, 4), (2048, 16))
ATTN_N_BACK = 128
HEADS_PER_GROUP = 2
DECAY_LORA = 64
AAA_LORA = 64
GATE_LORA = 128
SSM_GROUP_CH = 16
SSM_STATE = 64
GN_EPS = 64e-5
NORM_EPS = 1e-6
N_BRANCH = 3

V7X_LANES = 128
V7X_VMEM_BYTES = 64 * 1024 * 1024
VMEM_LIMIT = V7X_VMEM_BYTES - 8 * 1024 * 1024

ROW_TILE = 512
COL_CHUNK = 512
FFN_CHUNK = 256
ATTN_BLOCK = 2048
ATTN_UNROLL = 4
RWKV_BLOCK = 256
RWKV_CHUNK = 64
RWKV_SEQS = 4
RWKV_GROUP_LANES = 256
SSM_CHUNK = 16
SSM_SCAN_LEVELS = 8


def _cparams(*sem):
    return pltpu.CompilerParams(dimension_semantics=sem, vmem_limit_bytes=VMEM_LIMIT)


def _resident(shape):
    nd = len(shape)
    return pl.BlockSpec(shape, lambda *_: (0,) * nd, pipeline_mode=pl.Buffered(1))


def _dot(a, b):
    return jnp.dot(a.astype(BF16), b.astype(BF16), preferred_element_type=F32)


def _dot_nt(a, b):
    return lax.dot_general(a.astype(BF16), b.astype(BF16), (((1,), (1,)), ((), ())),
                           preferred_element_type=F32)


def _dot_tn(a, b):
    return lax.dot_general(a.astype(BF16), b.astype(BF16), (((0,), (0,)), ((), ())),
                           preferred_element_type=F32)


def _split(x, terms):
    out, rest = [], x
    for _ in range(terms - 1):
        h = rest.astype(BF16)
        out.append(h)
        rest = rest - h.astype(F32)
    out.append(rest.astype(BF16))
    return out


def _dot_xw(x, w, terms=3):
    m = x.shape[0]
    y = jnp.dot(jnp.concatenate(_split(x, terms), axis=0), w, preferred_element_type=F32)
    return functools.reduce(lambda a, b: a + b, [y[i * m:(i + 1) * m] for i in range(terms)])


def _dot_wx(w, x, terms=3):
    n = x.shape[1]
    y = jnp.dot(w, jnp.concatenate(_split(x, terms), axis=1), preferred_element_type=F32)
    return functools.reduce(lambda a, b: a + b, [y[:, i * n:(i + 1) * n] for i in range(terms)])


def _dot_hi(a, b, dims=(((1,), (0,)), ((), ()))):
    ah = a.astype(BF16)
    al = (a - ah.astype(F32)).astype(BF16)
    bh = b.astype(BF16)
    bl = (b - bh.astype(F32)).astype(BF16)
    dg = functools.partial(lax.dot_general, dimension_numbers=dims, preferred_element_type=F32)
    return dg(ah, bh) + dg(ah, bl) + dg(al, bh)


def _rmsnorm(x, gain):
    ms = jnp.mean(x * x, axis=-1, keepdims=True)
    return x * lax.rsqrt(ms + NORM_EPS) * gain


def _sigmoid(x):
    return 1.0 / (1.0 + jnp.exp(-x))


def _in_proj_body(h_ref, gain_ref, w_ref, qkv_ref, rw_ref, ssm_ref, gate_ref, *, widths):
    attn_in, rwkv_in, ssm_w, gate_w = widths
    u = _rmsnorm(h_ref[...], gain_ref[...]).astype(BF16)

    rw_off = attn_in
    ssm_off = rw_off + rwkv_in
    gate_off = ssm_off + ssm_w
    total = gate_off + gate_w

    def store(col, y):
        if col < rw_off:
            qkv_ref[col // V7X_LANES] = y
        elif col < ssm_off:
            rw_ref[:, col - rw_off:col - rw_off + V7X_LANES] = y
        elif col < gate_off:
            ssm_ref[(col - ssm_off) // V7X_LANES] = y
        else:
            gate_ref[:, col - gate_off:col - gate_off + V7X_LANES] = _sigmoid(y).astype(gate_ref.dtype)

    for lo in range(0, total, COL_CHUNK):
        hi = min(lo + COL_CHUNK, total)
        y = jnp.dot(u, w_ref[:, lo:hi], preferred_element_type=F32)
        for j in range((hi - lo) // V7X_LANES):
            store(lo + j * V7X_LANES, y[:, j * V7X_LANES:(j + 1) * V7X_LANES])


def _in_proj(h, gain, w_bf16, widths):
    t, d = h.shape
    attn_in, rwkv_in, ssm_w, gate_w = widths
    n_slab = attn_in // V7X_LANES
    tm = ROW_TILE
    return pl.pallas_call(
        functools.partial(_in_proj_body, widths=widths),
        grid=(t // tm,),
        in_specs=[pl.BlockSpec((tm, d), lambda i: (i, 0)),
                  _resident((1, d)),
                  _resident(w_bf16.shape)],
        out_specs=[pl.BlockSpec((n_slab, tm, V7X_LANES), lambda i: (0, i, 0)),
                   pl.BlockSpec((tm, rwkv_in), lambda i: (i, 0)),
                   pl.BlockSpec((ssm_w // V7X_LANES, tm, V7X_LANES), lambda i: (0, i, 0)),
                   pl.BlockSpec((tm, gate_w), lambda i: (i, 0))],
        out_shape=[jax.ShapeDtypeStruct((n_slab, t, V7X_LANES), F32),
                   jax.ShapeDtypeStruct((t, rwkv_in), F32),
                   jax.ShapeDtypeStruct((ssm_w // V7X_LANES, t, V7X_LANES), F32),
                   jax.ShapeDtypeStruct((t, gate_w), BF16)],
        compiler_params=_cparams("arbitrary"),
        name="in_proj",
    )(h, gain.reshape(1, d), w_bf16)


def _attn_body(qkv_ref, o_ref, kb0, vb0, kb1, vb1, kb2, vb2, obuf, lbuf, *, tb):
    n = pl.program_id(1)
    nb = ATTN_N_BACK
    kbufs, vbufs = (kb0, kb1, kb2), (vb0, vb1, vb2)
    n_grp = len(ATTN_GROUPS)
    lane = lax.broadcasted_iota(jnp.int32, (nb, V7X_LANES), 1)
    head0 = lane < HEAD_DIM
    qi = lax.broadcasted_iota(jnp.int32, (nb, 2 * nb), 0)
    kj = lax.broadcasted_iota(jnp.int32, (nb, 2 * nb), 1)
    band = (kj >= qi) & (kj <= qi + nb)
    scale = HEAD_DIM ** -0.5

    for g, (window, dil) in enumerate(ATTN_GROUPS):
        span = window
        kb, vb = kbufs[g], vbufs[g]

        @pl.when(n == 0)
        def _():
            kb[0:span, :] = jnp.zeros((span, V7X_LANES), F32)
            vb[0:span, :] = jnp.zeros((span, V7X_LANES), F32)

        @pl.when(n > 0)
        def _():
            kb[0:span, :] = kb[tb:tb + span, :]
            vb[0:span, :] = vb[tb:tb + span, :]

        kb[span:span + tb, :] = qkv_ref[n_grp + g]
        vb[span:span + tb, :] = qkv_ref[2 * n_grp + g]
        n_sub = tb // span

        def units(it, carry, g=g, dil=dil, span=span, kb=kb, vb=vb, n_sub=n_sub):
            bases, masks, qs, k2s, v2s = [], [], [], [], []
            for k in range(ATTN_UNROLL):
                idx = it * ATTN_UNROLL + k
                j = idx // dil
                base = j * span + (idx - j * dil)
                bases.append(base)
                masks.append(band & ((kj >= nb) | ((n * n_sub + j) > 0)))
                qs.append(qkv_ref[g, pl.ds(base, nb, stride=dil), :] * scale)
                k2s.append(kb[pl.ds(base, 2 * nb, stride=dil), :].astype(BF16))
                v2s.append(vb[pl.ds(base, 2 * nb, stride=dil), :].astype(BF16))
            scores = [[_dot_nt(jnp.where(hmask, qs[k], 0.0), k2s[k]) for hmask in (head0, ~head0)]
                      for k in range(ATTN_UNROLL)]
            probs, dens, lses = [], [], []
            for k in range(ATTN_UNROLL):
                for s in scores[k]:
                    s = jnp.where(masks[k], s, -1e30)
                    mx = jnp.max(s, axis=-1, keepdims=True)
                    p = jnp.exp(s - mx)
                    den = jnp.sum(p, axis=-1, keepdims=True)
                    probs.append(p.astype(BF16))
                    dens.append(den)
                    lses.append(mx + jnp.log(den))
            for k in range(ATTN_UNROLL):
                o0 = jnp.dot(probs[2 * k], v2s[k], preferred_element_type=F32) / dens[2 * k]
                o1 = jnp.dot(probs[2 * k + 1], v2s[k], preferred_element_type=F32) / dens[2 * k + 1]
                obuf[g, pl.ds(bases[k], nb, stride=dil), :] = jnp.where(head0, o0, o1)
                lbuf[g, pl.ds(bases[k], nb, stride=dil), :] = jnp.where(head0, lses[2 * k], lses[2 * k + 1])
            return carry

        lax.fori_loop(0, n_sub * dil // ATTN_UNROLL, units, 0)

    rows = 256
    def mix(i, carry):
        sl = pl.ds(pl.multiple_of(i * rows, rows), rows)
        ls = [lbuf[g, sl, :] for g in range(n_grp)]
        m = functools.reduce(jnp.maximum, ls)
        es = [jnp.exp(l - m) for l in ls]
        tot = functools.reduce(lambda a, b: a + b, es)
        for g in range(n_grp):
            o_ref[sl, g * V7X_LANES:(g + 1) * V7X_LANES] = (obuf[g, sl, :] * (es[g] / tot)).astype(o_ref.dtype)
        return carry
    lax.fori_loop(0, tb // rows, mix, 0)


def _attention(qkv, batch, seq):
    n_slab, t, _ = qkv.shape
    tb = min(ATTN_BLOCK, seq)
    nblk = seq // tb
    n_grp = len(ATTN_GROUPS)
    scratch = []
    for window, _ in ATTN_GROUPS:
        scratch += [pltpu.VMEM((window + tb, V7X_LANES), F32)] * 2
    scratch += [pltpu.VMEM((n_grp, tb, V7X_LANES), F32)] * 2
    return pl.pallas_call(
        functools.partial(_attn_body, tb=tb),
        grid=(batch, nblk),
        in_specs=[pl.BlockSpec((n_slab, tb, V7X_LANES), lambda b, n: (0, b * nblk + n, 0))],
        out_specs=pl.BlockSpec((tb, n_grp * V7X_LANES), lambda b, n: (b * nblk + n, 0)),
        out_shape=jax.ShapeDtypeStruct((t, n_grp * V7X_LANES), BF16),
        scratch_shapes=scratch,
        compiler_params=_cparams("arbitrary", "arbitrary"),
        name="dilated_attention",
    )(qkv)


def _rwkv_body(z_ref, mix_ref, w0_ref, w2_ref, a0_ref, a2_ref, g2_ref, kk_ref, ka_ref, rk_ref,
               lnw_ref, lnb_ref, seg_ref, tri_ref, o_ref, zbuf, zz_s, state, *, tr, width):
    n = pl.program_id(1)
    nseq = z_ref.shape[0]
    c_len = RWKV_CHUNK
    pad = 8

    @pl.when(n == 0)
    def _():
        zbuf[:, 0:pad, :] = jnp.zeros((nseq, pad, zbuf.shape[2]), F32)
        state[...] = jnp.zeros(state.shape, F32)

    @pl.when(n > 0)
    def _():
        zbuf[:, 0:pad, :] = zbuf[:, tr:tr + pad, :]

    zbuf[:, pad:pad + tr, :] = z_ref[...]
    mix = mix_ref[...]
    for q in range(nseq):
        for i in range(tr // c_len):
            cur = zbuf[q, pad + i * c_len:pad + (i + 1) * c_len, :]
            prev = zbuf[q, pad - 1 + i * c_len:pad - 1 + (i + 1) * c_len, :]
            zz_s[q, i * c_len:(i + 1) * c_len, :] = cur + (prev - cur) * mix

    seg = seg_ref[...]
    tri = tri_ref[...]
    lora = DECAY_LORA + AAA_LORA
    n_sq = int(math.log2(c_len)) - 1
    rsl = [slice(q * c_len, (q + 1) * c_len) for q in range(nseq)]
    per = RWKV_GROUP_LANES // V7X_LANES
    cols = [(q, j) for q in range(nseq) for j in range(width // V7X_LANES)]
    whole = [cols[i:i + per] for q in range(nseq)
             for i in range(q * (width // V7X_LANES), (q + 1) * (width // V7X_LANES) - per + 1, per)]
    rest = [c for c in cols if not any(c in g for g in whole)]
    groups = whole + [rest[i:i + per] for i in range(0, len(rest), per)]
    n_grp = range(len(groups))
    gw = [len(g) * V7X_LANES for g in groups]

    def group_masks(w):
        r = lax.broadcasted_iota(jnp.int32, (c_len, w), 0)
        l = lax.broadcasted_iota(jnp.int32, (c_len, w), 1) % HEAD_DIM
        br = lax.broadcasted_iota(jnp.int32, (w, w), 0) // HEAD_DIM
        bl = lax.broadcasted_iota(jnp.int32, (w, w), 1) // HEAD_DIM
        return l < r, l <= r, (l == r).astype(F32), br == bl
    masks = {w: group_masks(w) for w in set(gw)}

    def chunk(c, carry):
        sl = pl.ds(pl.multiple_of(c * c_len, c_len), c_len)
        zz = jnp.concatenate([zz_s[q, sl, :] for q in range(nseq)], axis=0)
        r = zz[:, 0:width]
        k = zz[:, width:2 * width]
        v = zz[:, 2 * width:3 * width]
        t = zz[:, 3 * width:3 * width + lora]
        xg = zz[:, 3 * width + lora:3 * width + lora + GATE_LORA]

        wpre = w0_ref[...] + _dot_hi(jnp.tanh(t), w2_ref[...])
        nw = -wpre
        softplus = jnp.maximum(nw, 0.0) + jnp.log(1.0 + jnp.exp(-jnp.abs(nw)))
        lw = -jnp.exp(-softplus - 0.5)
        a = _sigmoid(a0_ref[...] + _dot(t, a2_ref[...]))
        gate = _dot(_sigmoid(xg), g2_ref[...])

        kk = k * kk_ref[...]
        nrm = jnp.sqrt(_dot_xw(kk * kk, seg, terms=2))
        kk = kk / jnp.maximum(nrm, 1e-12)
        k = k * (1.0 + (a - 1.0) * ka_ref[...])
        bonus = _dot(r * k * rk_ref[...], seg) * v
        avec = -kk
        bvec = kk * a

        lc = _dot_wx(tri, lw)
        lc_last = [lc[s.stop - 1:s.stop, :] for s in rsl]
        lc_end = jnp.concatenate([jnp.broadcast_to(e, (c_len, width)) for e in lc_last], axis=0)
        e_pos = jnp.exp(lc)
        e_neg = jnp.exp(-lc)
        e_end = jnp.exp(lc_end - lc)
        rt = (r * e_pos).astype(BF16)
        at = (avec * jnp.exp(lc - lw)).astype(BF16)
        kt = (k * e_neg).astype(BF16)
        bt = (bvec * e_neg).astype(BF16)
        kt_end = (k * e_end).astype(BF16)
        bt_end = (bvec * e_end).astype(BF16)
        p_end = [jnp.exp(e) for e in lc_last]
        vb = v.astype(BF16)

        pack = lambda x, g: jnp.concatenate(
            [x[rsl[q], j * V7X_LANES:(j + 1) * V7X_LANES] for q, j in groups[g]], axis=1)
        bd = lambda x, g: jnp.where(masks[gw[g]][3], jnp.concatenate([x] * (gw[g] // HEAD_DIM), axis=0), 0)
        strict = [masks[w][0] for w in gw]
        incl = [masks[w][1] for w in gw]
        a_p, r_p, b_p, k_p, be_p, ke_p, v_p = ([pack(x, g) for g in n_grp]
                                               for x in (at, rt, bt, kt, bt_end, kt_end, vb))
        sm = [state[g, :, 0:gw[g]] for g in n_grp]
        smb = [m.astype(BF16) for m in sm]
        ar = [jnp.concatenate([a_p[g], r_p[g]], axis=0) for g in n_grp]
        gbk = [_dot_nt(ar[g], jnp.concatenate([bd(b_p[g], g), bd(k_p[g], g)], axis=0)) for g in n_grp]
        lmat = [jnp.where(strict[g], gbk[g][0:c_len, 0:gw[g]], 0.0) for g in n_grp]
        rb = [jnp.where(incl[g], gbk[g][c_len:, 0:gw[g]], 0.0).astype(BF16) for g in n_grp]
        akrk = [jnp.concatenate([jnp.where(strict[g], gbk[g][0:c_len, gw[g]:], 0.0),
                                 jnp.where(incl[g], gbk[g][c_len:, gw[g]:], 0.0)], axis=0).astype(BF16)
                for g in n_grp]
        lb = [x.astype(BF16) for x in lmat]
        pw = [_dot(lb[g], bd(lb[g], g)) for g in n_grp]
        tinv = [masks[gw[g]][2] + lmat[g] for g in n_grp]
        s0 = [_dot_nt(ar[g], bd(smb[g], g)) for g in n_grp]
        kv = [_dot(akrk[g], bd(v_p[g], g)) for g in n_grp]
        for _ in range(n_sq - 1):
            pb = [x.astype(BF16) for x in pw]
            sq = [_dot(jnp.concatenate([pb[g], tinv[g].astype(BF16)], axis=0), bd(pb[g], g)) for g in n_grp]
            tinv = [tinv[g] + sq[g][c_len:] for g in n_grp]
            pw = [sq[g][0:c_len] for g in n_grp]
        tinv = [tinv[g] + _dot(tinv[g], bd(pw[g].astype(BF16), g)) for g in n_grp]
        ub = [_dot(tinv[g], bd((s0[g][0:c_len] + kv[g][0:c_len]).astype(BF16), g)).astype(BF16) for g in n_grp]
        ys = [s0[g][c_len:] + _dot(rb[g], bd(ub[g], g)) + kv[g][c_len:] for g in n_grp]
        for g in n_grp:
            full = _dot_tn(jnp.concatenate([ub[g], v_p[g]], axis=0), jnp.concatenate([be_p[g], ke_p[g]], axis=0))
            full = jnp.where(masks[gw[g]][3], full, 0.0)
            upd = functools.reduce(lambda a, b: a + b, [full[u * HEAD_DIM:(u + 1) * HEAD_DIM]
                                                        for u in range(gw[g] // HEAD_DIM)])
            p_end_g = jnp.concatenate([p_end[q][:, j * V7X_LANES:(j + 1) * V7X_LANES] for q, j in groups[g]], axis=1)
            state[g, :, 0:gw[g]] = sm[g] * p_end_g + upd
        piece = {qj: ys[g][:, i * V7X_LANES:(i + 1) * V7X_LANES] for g in n_grp for i, qj in enumerate(groups[g])}
        y = jnp.concatenate([jnp.concatenate([piece[(q, j)] for j in range(width // V7X_LANES)], axis=1)
                             for q in range(nseq)], axis=0)

        inv_e = 1.0 / HEAD_DIM
        mu = _dot_xw(y, seg, terms=2) * inv_e
        yc = y - mu
        var = _dot_xw(yc * yc, seg, terms=2) * inv_e
        yn = yc * lax.rsqrt(var + GN_EPS) * lnw_ref[...] + lnb_ref[...]
        out = ((yn + bonus) * gate).astype(o_ref.dtype)
        for q in range(nseq):
            o_ref[q, sl, :] = out[rsl[q], :]
        return carry

    lax.fori_loop(0, tr // c_len, chunk, 0)


def _rwkv(z, batch, seq, prm):
    t, zin = z.shape
    width = prm["w0"].shape[-1]
    tr = min(RWKV_BLOCK, seq)
    nblk = seq // tr
    nseq = RWKV_SEQS if batch % RWKV_SEQS == 0 else 1
    ncol, per = width // V7X_LANES, RWKV_GROUP_LANES // V7X_LANES
    n_groups = nseq * (ncol // per) + -(-(nseq * (ncol % per)) // per)
    row = lambda x: x.reshape(1, -1).astype(F32)
    w2p = jnp.concatenate([prm["w2"], jnp.zeros((AAA_LORA, width), F32)], axis=0)
    a2p = jnp.concatenate([jnp.zeros((DECAY_LORA, width), F32), prm["a2"]], axis=0)
    hid = jnp.arange(width) // HEAD_DIM
    seg = (hid[:, None] == hid[None, :]).astype(BF16)
    pos = jnp.arange(nseq * RWKV_CHUNK)
    tri = ((pos[:, None] >= pos[None, :])
           & (pos[:, None] // RWKV_CHUNK == pos[None, :] // RWKV_CHUNK)).astype(BF16)
    params = [row(prm["mix"]), row(prm["w0"]), w2p, row(prm["a0"]), a2p.astype(BF16), prm["g2"].astype(BF16),
              row(prm["k_k"]), row(prm["k_a"]), row(prm["r_k"]), row(prm["ln_w"]), row(prm["ln_b"]),
              seg, tri]
    out = pl.pallas_call(
        functools.partial(_rwkv_body, tr=tr, width=width),
        grid=(batch // nseq, nblk),
        in_specs=[pl.BlockSpec((nseq, tr, zin), lambda b, n: (b, n, 0))]
                 + [_resident(p.shape) for p in params],
        out_specs=pl.BlockSpec((nseq, tr, width), lambda b, n: (b, n, 0)),
        out_shape=jax.ShapeDtypeStruct((batch, seq, width), BF16),
        scratch_shapes=[pltpu.VMEM((nseq, tr + 8, zin), F32), pltpu.VMEM((nseq, tr, zin), F32),
                        pltpu.VMEM((n_groups, HEAD_DIM, RWKV_GROUP_LANES), F32)],
        compiler_params=_cparams("arbitrary", "arbitrary"),
        name="rwkv7_time_mix",
    )(z.reshape(batch, seq, zin), *params)
    return out.reshape(t, width)


def _ssm_prep_body(are_ref, aim_ref, lstep_ref, bre_ref, bim_ref, cre_ref, cim_ref,
                   kt_ref, wre_ref, wim_ref, vre_ref, vim_ref, pre_ref, pim_ref):
    lam_re = are_ref[0]
    lam_im = aim_ref[0]
    step = jnp.exp(lstep_ref[0])
    mag = jnp.exp(lam_re * step)
    ang = lam_im * step
    abar_re, abar_im = mag * jnp.cos(ang), mag * jnp.sin(ang)
    inv = 1.0 / (lam_re * lam_re + lam_im * lam_im)
    f_re = ((abar_re - 1.0) * lam_re + abar_im * lam_im) * inv
    f_im = (abar_im * lam_re - (abar_re - 1.0) * lam_im) * inv
    b_re, b_im = bre_ref[0], bim_ref[0]
    bbar_re = f_re * b_re - f_im * b_im
    bbar_im = f_re * b_im + f_im * b_re
    c_re, c_im = cre_ref[0], cim_ref[0]

    def power(tau):
        m = jnp.exp(lam_re * step * tau)
        return m * jnp.cos(ang * tau), m * jnp.sin(ang * tau)

    nt = (((1,), (1,)), ((), ()))
    for tau in range(SSM_CHUNK):
        p_re, p_im = power(float(tau))
        cp_re = c_re * p_re - c_im * p_im
        cp_im = c_re * p_im + c_im * p_re
        kt_ref[0, tau] = _dot_hi(cp_re, bbar_re, nt) - _dot_hi(cp_im, bbar_im, nt)
        q_re, q_im = power(float(SSM_CHUNK - 1 - tau))
        wre_ref[0, tau] = q_re * bbar_re - q_im * bbar_im
        wim_ref[0, tau] = q_re * bbar_im + q_im * bbar_re
        s_re, s_im = power(float(tau + 1))
        vre_ref[0, tau] = c_re * s_re - c_im * s_im
        vim_ref[0, tau] = -(c_re * s_im + c_im * s_re)
    for lvl in range(SSM_SCAN_LEVELS):
        p_re, p_im = power(float(SSM_CHUNK * 2 ** lvl))
        pre_ref[0, lvl:lvl + 1, :] = p_re
        pim_ref[0, lvl:lvl + 1, :] = p_im


def _ssm_prep(a_re, a_im, log_step, b_re, b_im, c_re, c_im):
    g, p = a_re.shape
    c = SSM_GROUP_CH
    l = SSM_CHUNK
    grp3 = lambda shape: pl.BlockSpec((1,) + shape, lambda i: (i, 0, 0))
    grp4 = lambda shape: pl.BlockSpec((1,) + shape, lambda i: (i, 0, 0, 0))
    sds = jax.ShapeDtypeStruct
    return pl.pallas_call(
        _ssm_prep_body,
        grid=(g,),
        in_specs=[grp3((1, p)), grp3((1, p)), grp3((1, 1)), grp3((c, p)), grp3((c, p)),
                  grp3((c, p)), grp3((c, p))],
        out_specs=[grp4((l, c, c)), grp4((l, c, p)), grp4((l, c, p)), grp4((l, c, p)), grp4((l, c, p)),
                   grp3((SSM_SCAN_LEVELS, p)), grp3((SSM_SCAN_LEVELS, p))],
        out_shape=[sds((g, l, c, c), F32), sds((g, l, c, p), F32), sds((g, l, c, p), F32),
                   sds((g, l, c, p), F32), sds((g, l, c, p), F32),
                   sds((g, SSM_SCAN_LEVELS, p), F32), sds((g, SSM_SCAN_LEVELS, p), F32)],
        compiler_params=_cparams("arbitrary"),
        name="ssm_discretise",
    )(a_re.reshape(g, 1, p), a_im.reshape(g, 1, p), log_step.reshape(g, 1, 1),
      jnp.swapaxes(b_re, 1, 2), jnp.swapaxes(b_im, 1, 2), c_re, c_im)


def _ssm_scan_body(u_ref, kd_ref, wd_ref, vd_ref, mre_ref, mim_ref, d_ref, y_ref, *, n_chunks):
    l = SSM_CHUNK
    n_half = u_ref.shape[0]
    row = lax.broadcasted_iota(jnp.int32, (n_chunks, 1), 0)
    for h in range(n_half):
        x32 = [u_ref[h, pl.ds(j, n_chunks, stride=l), :] for j in range(l)]
        xb = [x.astype(BF16) for x in x32]
        z = jnp.dot(jnp.concatenate(xb, axis=1), wd_ref[h], preferred_element_type=F32)
        half = z.shape[1] // 2
        z_re, z_im = z[:, :half], z[:, half:]
        for lvl in range(SSM_SCAN_LEVELS):
            s = 2 ** lvl
            if s >= n_chunks:
                break
            s_re = jnp.where(row >= s, pltpu.roll(z_re, s, 0), 0.0)
            s_im = jnp.where(row >= s, pltpu.roll(z_im, s, 0), 0.0)
            a_re = mre_ref[h, lvl:lvl + 1, :]
            a_im = mim_ref[h, lvl:lvl + 1, :]
            z_re, z_im = z_re + s_re * a_re - s_im * a_im, z_im + s_re * a_im + s_im * a_re
        zin = jnp.concatenate([jnp.where(row >= 1, pltpu.roll(z_re, 1, 0), 0.0),
                               jnp.where(row >= 1, pltpu.roll(z_im, 1, 0), 0.0)], axis=1)
        y_state = jnp.dot(zin.astype(BF16), vd_ref[h], preferred_element_type=F32)
        dvec = d_ref[h]
        lanes = x32[0].shape[1]
        for i in range(l):
            lags = jnp.concatenate([xb[i - tau] for tau in range(i + 1)], axis=1)
            y = (jnp.dot(lags, kd_ref[h, 0:(i + 1) * lanes, :], preferred_element_type=F32)
                 + y_state[:, i * lanes:(i + 1) * lanes] + x32[i] * dvec)
            y_ref[h, pl.ds(i, n_chunks, stride=l), :] = y


def _ssm_scan(u, kd, wd, vd, m_re, m_im, dvec, batch, seq):
    n_half, t, lanes = u.shape
    n_chunks = seq // SSM_CHUNK
    weights = [kd, wd, vd, m_re, m_im, dvec]
    tok = pl.BlockSpec((n_half, seq, lanes), lambda b: (0, b, 0))
    return pl.pallas_call(
        functools.partial(_ssm_scan_body, n_chunks=n_chunks),
        grid=(batch,),
        in_specs=[tok] + [_resident(w.shape) for w in weights],
        out_specs=tok,
        out_shape=jax.ShapeDtypeStruct(u.shape, F32),
        compiler_params=_cparams("arbitrary"),
        name="ssm_chunk_scan",
    )(u, *weights)


def _ssm(u_slabs, batch, seq, prm):
    n_half, t, lanes = u_slabs.shape
    c, l, p = SSM_GROUP_CH, SSM_CHUNK, SSM_STATE
    gh = lanes // c
    kt, wre, wim, vre, vim, pre, pim = _ssm_prep(prm["a_re"], prm["a_im"], prm["log_step"],
                                                 prm["b_re"], prm["b_im"], prm["c_re"], prm["c_im"])
    eye = jnp.eye(gh, dtype=F32)
    kd = jnp.einsum('hgtcd,gk->htgdkc', kt.reshape(n_half, gh, l, c, c), eye)
    kd = kd.reshape(n_half, l * lanes, lanes).astype(BF16)
    wd = jnp.einsum('rhgjdp,gk->hjgdrkp', jnp.stack([wre, wim]).reshape(2, n_half, gh, l, c, p), eye)
    wd = wd.reshape(n_half, l * lanes, 2 * gh * p).astype(BF16)
    vd = jnp.einsum('rhgicp,gk->hrgpikc', jnp.stack([vre, vim]).reshape(2, n_half, gh, l, c, p), eye)
    vd = vd.reshape(n_half, 2 * gh * p, l * lanes).astype(BF16)
    lv = SSM_SCAN_LEVELS
    m_re = jnp.transpose(pre.reshape(n_half, gh, lv, p), (0, 2, 1, 3)).reshape(n_half, lv, gh * p)
    m_im = jnp.transpose(pim.reshape(n_half, gh, lv, p), (0, 2, 1, 3)).reshape(n_half, lv, gh * p)
    dvec = prm["d"].reshape(n_half, 1, lanes).astype(F32)
    return _ssm_scan(u_slabs, kd, wd, vd, m_re, m_im, dvec, batch, seq)


def _mix_ffn_body(oa_ref, or_ref, ys_ref, gate_ref, h_ref, wa_ref, wr_ref, ws_ref, wv_ref, wg_ref, wo_ref,
                  gain_ref, wgu_ref, wd_ref, fgain_ref, out_ref, *, hidden, final_norm):
    d = h_ref.shape[1]
    y = jnp.concatenate([ys_ref[i] for i in range(ys_ref.shape[0])], axis=1)
    zg = (0.5 * y * (1.0 + jnp.tanh(math.sqrt(2.0 / math.pi) * (y + 0.044715 * (y * y * y))))).astype(BF16)
    o_ssm = (jnp.dot(zg, wv_ref[...], preferred_element_type=F32)
             * _sigmoid(jnp.dot(zg, wg_ref[...], preferred_element_type=F32)))
    gate = lambda i: gate_ref[:, i * d:(i + 1) * d].astype(F32)
    merged = (gate(0) * _dot(oa_ref[...], wa_ref[...])
              + gate(1) * _dot(or_ref[...], wr_ref[...])
              + gate(2) * _dot(o_ssm, ws_ref[...]))
    x = h_ref[...] + _dot(merged, wo_ref[...])

    u = _rmsnorm(x, gain_ref[...]).astype(BF16)
    acc = jnp.zeros(x.shape, F32)
    for c in range(hidden // FFN_CHUNK):
        lo = c * FFN_CHUNK
        a = jnp.dot(u, wgu_ref[:, lo:lo + FFN_CHUNK], preferred_element_type=F32)
        b = jnp.dot(u, wgu_ref[:, hidden + lo:hidden + lo + FFN_CHUNK], preferred_element_type=F32)
        hm = (a * _sigmoid(a) * b).astype(BF16)
        acc = acc + jnp.dot(hm, wd_ref[lo:lo + FFN_CHUNK, :], preferred_element_type=F32)
    x = x + acc
    if final_norm:
        x = _rmsnorm(x, fgain_ref[...])
    out_ref[...] = x


def _mix_ffn(o_attn, o_rwkv, y_ssm, gates, h, mix_weights, gain, wgu, wd, final_gain, final_norm):
    t, d = h.shape
    tm = ROW_TILE
    rows = lambda a: pl.BlockSpec((tm, a.shape[1]), lambda i: (i, 0))
    slabs = lambda a: pl.BlockSpec((a.shape[0], tm, a.shape[2]), lambda i: (0, i, 0))
    weights = list(mix_weights) + [gain.reshape(1, d), wgu, wd, final_gain.reshape(1, d)]
    return pl.pallas_call(
        functools.partial(_mix_ffn_body, hidden=wd.shape[0], final_norm=final_norm),
        grid=(t // tm,),
        in_specs=[rows(o_attn), rows(o_rwkv), slabs(y_ssm), rows(gates), rows(h)]
                 + [_resident(w.shape) for w in weights],
        out_specs=rows(h),
        out_shape=jax.ShapeDtypeStruct((t, d), F32),
        compiler_params=_cparams("arbitrary"),
        name="merge_ffn",
    )(o_attn, o_rwkv, y_ssm, gates, h, *weights)


def kernel(x, norm_mix, w_in, rwkv_shift_mix, rwkv_w0, rwkv_w2, rwkv_a0, rwkv_a2, rwkv_g2, rwkv_k_k, rwkv_k_a, rwkv_r_k, rwkv_ln_w, rwkv_ln_b, ssm_a_re, ssm_a_im, ssm_log_step, ssm_b_re, ssm_b_im, ssm_c_re, ssm_c_im, ssm_d, ssm_glu_val, ssm_glu_gate, w_branch_attn, w_branch_rwkv, w_branch_ssm, w_out, norm_ffn, ffn_w_gate_up, ffn_w_down, norm_final):
    batch, seq, d = x.shape
    depth = w_in.shape[0]
    attn_w = w_branch_attn.shape[1]
    rwkv_in = rwkv_shift_mix.shape[1]
    ssm_w = w_branch_ssm.shape[1]
    widths = (3 * attn_w, rwkv_in, ssm_w, N_BRANCH * d)
    assert sum(widths) == w_in.shape[2]
    assert attn_w == len(ATTN_GROUPS) * HEADS_PER_GROUP * HEAD_DIM
    assert seq % ATTN_BLOCK == 0 or seq in (w for w, _ in ATTN_GROUPS)
    assert seq // SSM_CHUNK <= 2 ** SSM_SCAN_LEVELS and (batch * seq) % ROW_TILE == 0

    h = x.reshape(batch * seq, d)
    for l in range(depth):
        qkv, z_rwkv, u_ssm, gates = _in_proj(h, norm_mix[l], w_in[l].astype(BF16), widths)
        o_attn = _attention(qkv, batch, seq)
        o_rwkv = _rwkv(z_rwkv, batch, seq, dict(
            mix=rwkv_shift_mix[l], w0=rwkv_w0[l], w2=rwkv_w2[l], a0=rwkv_a0[l], a2=rwkv_a2[l],
            g2=rwkv_g2[l], k_k=rwkv_k_k[l], k_a=rwkv_k_a[l], r_k=rwkv_r_k[l],
            ln_w=rwkv_ln_w[l], ln_b=rwkv_ln_b[l]))
        y_ssm = _ssm(u_ssm, batch, seq, dict(
            a_re=ssm_a_re[l], a_im=ssm_a_im[l], log_step=ssm_log_step[l], b_re=ssm_b_re[l],
            b_im=ssm_b_im[l], c_re=ssm_c_re[l], c_im=ssm_c_im[l], d=ssm_d[l]))
        bf = lambda w: w.astype(BF16)
        h = _mix_ffn(o_attn, o_rwkv, y_ssm, gates, h,
                     [bf(w_branch_attn[l]), bf(w_branch_rwkv[l]), bf(w_branch_ssm[l]),
                      bf(ssm_glu_val[l]), bf(ssm_glu_gate[l]), bf(w_out[l])],
                     norm_ffn[l], bf(ffn_w_gate_up[l]), bf(ffn_w_down[l]), norm_final, l == depth - 1)
    return h.reshape(batch, seq, d)
```

```python
import functools
import math

import jax
import jax.numpy as jnp
from jax import lax
from jax.experimental import pallas as pl
from jax.experimental.pallas import tpu as pltpu

F32 = jnp.float32
BF16 = jnp.bfloat16

HEAD_DIM = 64
ATTN_GROUPS = ((128, 1), (512, 4), (2048, 16))
ATTN_N_BACK = 128
HEADS_PER_GROUP = 2
DECAY_LORA = 64
AAA_LORA = 64
GATE_LORA = 128
SSM_GROUP_CH = 16
SSM_STATE = 64
GN_EPS = 64e-5
NORM_EPS = 1e-6
N_BRANCH = 3

V7X_LANES = 128
V7X_SUBLANES = 8
V7X_VMEM_BYTES = 64 * 1024 * 1024
VMEM_LIMIT = V7X_VMEM_BYTES - 8 * 1024 * 1024

ROW_TILE = 512
COL_CHUNK = 512
FFN_CHUNK = 256
ATTN_BLOCK = 2048
ATTN_UNROLL = 4
RWKV_BLOCK = 256
RWKV_CHUNK = 64
RWKV_SEQS = 4
RWKV_GROUP_LANES = 256
SSM_CHUNK = 16
SSM_SCAN_LEVELS = 8


def _cparams(*sem):
    return pltpu.CompilerParams(dimension_semantics=sem, vmem_limit_bytes=VMEM_LIMIT)


def _resident(shape):
    nd = len(shape)
    return pl.BlockSpec(shape, lambda *_: (0,) * nd, pipeline_mode=pl.Buffered(1))


def _dot(a, b):
    return jnp.dot(a.astype(BF16), b.astype(BF16), preferred_element_type=F32)


def _dot_nt(a, b):
    return lax.dot_general(a.astype(BF16), b.astype(BF16), (((1,), (1,)), ((), ())),
                           preferred_element_type=F32)


def _dot_tn(a, b):
    return lax.dot_general(a.astype(BF16), b.astype(BF16), (((0,), (0,)), ((), ())),
                           preferred_element_type=F32)


def _split(x, terms):
    out, rest = [], x
    for _ in range(terms - 1):
        h = rest.astype(BF16)
        out.append(h)
        rest = rest - h.astype(F32)
    out.append(rest.astype(BF16))
    return out


def _dot_xw(x, w, terms=3):
    m = x.shape[0]
    y = jnp.dot(jnp.concatenate(_split(x, terms), axis=0), w, preferred_element_type=F32)
    return functools.reduce(lambda a, b: a + b, [y[i * m:(i + 1) * m] for i in range(terms)])


def _dot_wx(w, x, terms=3):
    n = x.shape[1]
    y = jnp.dot(w, jnp.concatenate(_split(x, terms), axis=1), preferred_element_type=F32)
    return functools.reduce(lambda a, b: a + b, [y[:, i * n:(i + 1) * n] for i in range(terms)])


def _dot_hi(a, b, dims=(((1,), (0,)), ((), ()))):
    ah = a.astype(BF16)
    al = (a - ah.astype(F32)).astype(BF16)
    bh = b.astype(BF16)
    bl = (b - bh.astype(F32)).astype(BF16)
    dg = functools.partial(lax.dot_general, dimension_numbers=dims, preferred_element_type=F32)
    return dg(ah, bh) + dg(ah, bl) + dg(al, bh)


def _rmsnorm(x, gain):
    ms = jnp.mean(x * x, axis=-1, keepdims=True)
    return x * lax.rsqrt(ms + NORM_EPS) * gain


def _sigmoid(x):
    return 1.0 / (1.0 + jnp.exp(-x))


def _in_proj_body(h_ref, gain_ref, w_ref, qkv_ref, rw_ref, ssm_ref, gate_ref, *, widths):
    attn_in, rwkv_in, ssm_w, gate_w = widths
    u = _rmsnorm(h_ref[...], gain_ref[...]).astype(BF16)

    rw_off = attn_in
    ssm_off = rw_off + rwkv_in
    gate_off = ssm_off + ssm_w
    total = gate_off + gate_w

    def store(col, y):
        if col < rw_off:
            qkv_ref[col // V7X_LANES] = y
        elif col < ssm_off:
            rw_ref[:, col - rw_off:col - rw_off + V7X_LANES] = y
        elif col < gate_off:
            ssm_ref[(col - ssm_off) // V7X_LANES] = y
        else:
            gate_ref[:, col - gate_off:col - gate_off + V7X_LANES] = _sigmoid(y).astype(gate_ref.dtype)

    for lo in range(0, total, COL_CHUNK):
        hi = min(lo + COL_CHUNK, total)
        y = jnp.dot(u, w_ref[:, lo:hi], preferred_element_type=F32)
        for j in range((hi - lo) // V7X_LANES):
            store(lo + j * V7X_LANES, y[:, j * V7X_LANES:(j + 1) * V7X_LANES])


def _in_proj(h, gain, w_bf16, widths):
    t, d = h.shape
    attn_in, rwkv_in, ssm_w, gate_w = widths
    n_slab = attn_in // V7X_LANES
    tm = ROW_TILE
    return pl.pallas_call(
        functools.partial(_in_proj_body, widths=widths),
        grid=(t // tm,),
        in_specs=[pl.BlockSpec((tm, d), lambda i: (i, 0)),
                  _resident((1, d)),
                  _resident(w_bf16.shape)],
        out_specs=[pl.BlockSpec((n_slab, tm, V7X_LANES), lambda i: (0, i, 0)),
                   pl.BlockSpec((tm, rwkv_in), lambda i: (i, 0)),
                   pl.BlockSpec((ssm_w // V7X_LANES, tm, V7X_LANES), lambda i: (0, i, 0)),
                   pl.BlockSpec((tm, gate_w), lambda i: (i, 0))],
        out_shape=[jax.ShapeDtypeStruct((n_slab, t, V7X_LANES), F32),
                   jax.ShapeDtypeStruct((t, rwkv_in), F32),
                   jax.ShapeDtypeStruct((ssm_w // V7X_LANES, t, V7X_LANES), F32),
                   jax.ShapeDtypeStruct((t, gate_w), BF16)],
        compiler_params=_cparams("arbitrary"),
        name="in_proj",
    )(h, gain.reshape(1, d), w_bf16)


def _attn_body(qkv_ref, o_ref, kb0, vb0, kb1, vb1, kb2, vb2, obuf, lbuf, *, tb):
    n = pl.program_id(1)
    nb = ATTN_N_BACK
    kbufs, vbufs = (kb0, kb1, kb2), (vb0, vb1, vb2)
    n_grp = len(ATTN_GROUPS)
    lane = lax.broadcasted_iota(jnp.int32, (nb, V7X_LANES), 1)
    head0 = lane < HEAD_DIM
    qi = lax.broadcasted_iota(jnp.int32, (nb, 2 * nb), 0)
    kj = lax.broadcasted_iota(jnp.int32, (nb, 2 * nb), 1)
    band = (kj >= qi) & (kj <= qi + nb)
    scale = HEAD_DIM ** -0.5

    for g, (window, dil) in enumerate(ATTN_GROUPS):
        span = window
        kb, vb = kbufs[g], vbufs[g]

        @pl.when(n == 0)
        def _():
            kb[0:span, :] = jnp.zeros((span, V7X_LANES), F32)
            vb[0:span, :] = jnp.zeros((span, V7X_LANES), F32)

        @pl.when(n > 0)
        def _():
            kb[0:span, :] = kb[tb:tb + span, :]
            vb[0:span, :] = vb[tb:tb + span, :]

        kb[span:span + tb, :] = qkv_ref[n_grp + g]
        vb[span:span + tb, :] = qkv_ref[2 * n_grp + g]
        n_sub = tb // span

        def units(it, carry, g=g, dil=dil, span=span, kb=kb, vb=vb, n_sub=n_sub):
            bases, masks, qs, k2s, v2s = [], [], [], [], []
            for k in range(ATTN_UNROLL):
                idx = it * ATTN_UNROLL + k
                j = idx // dil
                base = j * span + (idx - j * dil)
                bases.append(base)
                masks.append(band & ((kj >= nb) | ((n * n_sub + j) > 0)))
                qs.append(qkv_ref[g, pl.ds(base, nb, stride=dil), :] * scale)
                k2s.append(kb[pl.ds(base, 2 * nb, stride=dil), :].astype(BF16))
                v2s.append(vb[pl.ds(base, 2 * nb, stride=dil), :].astype(BF16))
            scores = [[_dot_nt(jnp.where(hmask, qs[k], 0.0), k2s[k]) for hmask in (head0, ~head0)]
                      for k in range(ATTN_UNROLL)]
            probs, dens, lses = [], [], []
            for k in range(ATTN_UNROLL):
                for s in scores[k]:
                    s = jnp.where(masks[k], s, -1e30)
                    mx = jnp.max(s, axis=-1, keepdims=True)
                    p = jnp.exp(s - mx)
                    den = jnp.sum(p, axis=-1, keepdims=True)
                    probs.append(p.astype(BF16))
                    dens.append(den)
                    lses.append(mx + jnp.log(den))
            for k in range(ATTN_UNROLL):
                o0 = jnp.dot(probs[2 * k], v2s[k], preferred_element_type=F32) / dens[2 * k]
                o1 = jnp.dot(probs[2 * k + 1], v2s[k], preferred_element_type=F32) / dens[2 * k + 1]
                obuf[g, pl.ds(bases[k], nb, stride=dil), :] = jnp.where(head0, o0, o1)
                lbuf[g, pl.ds(bases[k], nb, stride=dil), :] = jnp.where(head0, lses[2 * k], lses[2 * k + 1])
            return carry

        lax.fori_loop(0, n_sub * dil // ATTN_UNROLL, units, 0)

    rows = 256
    def mix(i, carry):
        sl = pl.ds(pl.multiple_of(i * rows, rows), rows)
        ls = [lbuf[g, sl, :] for g in range(n_grp)]
        m = functools.reduce(jnp.maximum, ls)
        es = [jnp.exp(l - m) for l in ls]
        tot = functools.reduce(lambda a, b: a + b, es)
        for g in range(n_grp):
            o_ref[sl, g * V7X_LANES:(g + 1) * V7X_LANES] = (obuf[g, sl, :] * (es[g] / tot)).astype(o_ref.dtype)
        return carry
    lax.fori_loop(0, tb // rows, mix, 0)


def _attention(qkv, batch, seq):
    n_slab, t, _ = qkv.shape
    tb = min(ATTN_BLOCK, seq)
    nblk = seq // tb
    n_grp = len(ATTN_GROUPS)
    scratch = []
    for window, _ in ATTN_GROUPS:
        scratch += [pltpu.VMEM((window + tb, V7X_LANES), F32)] * 2
    scratch += [pltpu.VMEM((n_grp, tb, V7X_LANES), F32)] * 2
    return pl.pallas_call(
        functools.partial(_attn_body, tb=tb),
        grid=(batch, nblk),
        in_specs=[pl.BlockSpec((n_slab, tb, V7X_LANES), lambda b, n: (0, b * nblk + n, 0))],
        out_specs=pl.BlockSpec((tb, n_grp * V7X_LANES), lambda b, n: (b * nblk + n, 0)),
        out_shape=jax.ShapeDtypeStruct((t, n_grp * V7X_LANES), BF16),
        scratch_shapes=scratch,
        compiler_params=_cparams("arbitrary", "arbitrary"),
        name="dilated_attention",
    )(qkv)


def _rwkv_body(z_ref, mix_ref, w0_ref, w2_ref, a0_ref, a2_ref, g2_ref, kk_ref, ka_ref, rk_ref,
               lnw_ref, lnb_ref, seg_ref, tri_ref, o_ref, carry, zz_s, state, *, tr, width):
    n = pl.program_id(1)
    nseq = z_ref.shape[0]
    c_len = RWKV_CHUNK
    pad = carry.shape[1]

    @pl.when(n == 0)
    def _():
        carry[...] = jnp.zeros(carry.shape, F32)
        state[...] = jnp.zeros(state.shape, F32)

    mix = mix_ref[...]
    for q in range(nseq):
        for i in range(tr // c_len):
            cur = z_ref[q, i * c_len:(i + 1) * c_len, :]
            head = carry[q] if i == 0 else z_ref[q, i * c_len - pad:i * c_len, :]
            prev = pltpu.roll(jnp.concatenate([head, cur], axis=0), 1, 0)[pad:]
            zz_s[q, i * c_len:(i + 1) * c_len, :] = cur + (prev - cur) * mix
    carry[...] = z_ref[:, tr - pad:tr, :]

    seg = seg_ref[...]
    tri = tri_ref[...]
    lora = DECAY_LORA + AAA_LORA
    n_sq = int(math.log2(c_len)) - 1
    rsl = [slice(q * c_len, (q + 1) * c_len) for q in range(nseq)]
    per = RWKV_GROUP_LANES // V7X_LANES
    cols = [(q, j) for q in range(nseq) for j in range(width // V7X_LANES)]
    whole = [cols[i:i + per] for q in range(nseq)
             for i in range(q * (width // V7X_LANES), (q + 1) * (width // V7X_LANES) - per + 1, per)]
    rest = [c for c in cols if not any(c in g for g in whole)]
    groups = whole + [rest[i:i + per] for i in range(0, len(rest), per)]
    n_grp = range(len(groups))
    gw = [len(g) * V7X_LANES for g in groups]

    def group_masks(w):
        r = lax.broadcasted_iota(jnp.int32, (c_len, w), 0)
        l = lax.broadcasted_iota(jnp.int32, (c_len, w), 1) % HEAD_DIM
        br = lax.broadcasted_iota(jnp.int32, (w, w), 0) // HEAD_DIM
        bl = lax.broadcasted_iota(jnp.int32, (w, w), 1) // HEAD_DIM
        return l < r, l <= r, (l == r).astype(F32), br == bl
    masks = {w: group_masks(w) for w in set(gw)}

    def chunk(c, carry):
        sl = pl.ds(pl.multiple_of(c * c_len, c_len), c_len)
        zz = jnp.concatenate([zz_s[q, sl, :] for q in range(nseq)], axis=0)
        r = zz[:, 0:width]
        k = zz[:, width:2 * width]
        v = zz[:, 2 * width:3 * width]
        t = zz[:, 3 * width:3 * width + lora]
        xg = zz[:, 3 * width + lora:3 * width + lora + GATE_LORA]

        wpre = w0_ref[...] + _dot_hi(jnp.tanh(t), w2_ref[...])
        nw = -wpre
        softplus = jnp.maximum(nw, 0.0) + jnp.log(1.0 + jnp.exp(-jnp.abs(nw)))
        lw = -jnp.exp(-softplus - 0.5)
        a = _sigmoid(a0_ref[...] + _dot(t, a2_ref[...]))
        gate = _dot(_sigmoid(xg), g2_ref[...])

        kk = k * kk_ref[...]
        nrm = jnp.sqrt(_dot(kk * kk, seg))
        kk = kk / jnp.maximum(nrm, 1e-12)
        k = k * (1.0 + (a - 1.0) * ka_ref[...])
        bonus = _dot(r * k * rk_ref[...], seg) * v
        avec = -kk
        bvec = kk * a

        lc = _dot_wx(tri, lw, terms=2)
        lc_last = [lc[s.stop - 1:s.stop, :] for s in rsl]
        lc_end = jnp.concatenate([jnp.broadcast_to(e, (c_len, width)) for e in lc_last], axis=0)
        e_pos = jnp.exp(lc)
        e_neg = jnp.exp(-lc)
        e_end = jnp.exp(lc_end - lc)
        rt = (r * e_pos).astype(BF16)
        at = (avec * jnp.exp(lc - lw)).astype(BF16)
        kt = (k * e_neg).astype(BF16)
        bt = (bvec * e_neg).astype(BF16)
        kt_end = (k * e_end).astype(BF16)
        bt_end = (bvec * e_end).astype(BF16)
        p_end = [jnp.exp(e) for e in lc_last]
        vb = v.astype(BF16)

        pack = lambda x, g: jnp.concatenate(
            [x[rsl[q], j * V7X_LANES:(j + 1) * V7X_LANES] for q, j in groups[g]], axis=1)
        bd = lambda x, g: jnp.where(masks[gw[g]][3], jnp.concatenate([x] * (gw[g] // HEAD_DIM), axis=0), 0)
        strict = [masks[w][0] for w in gw]
        incl = [masks[w][1] for w in gw]
        a_p, r_p, b_p, k_p, be_p, ke_p, v_p = ([pack(x, g) for g in n_grp]
                                               for x in (at, rt, bt, kt, bt_end, kt_end, vb))
        sm = [state[g, :, 0:gw[g]] for g in n_grp]
        smb = [m.astype(BF16) for m in sm]
        ar = [jnp.concatenate([a_p[g], r_p[g]], axis=0) for g in n_grp]
        gbk = [_dot_nt(ar[g], jnp.concatenate([bd(b_p[g], g), bd(k_p[g], g)], axis=0)) for g in n_grp]
        lmat = [jnp.where(strict[g], gbk[g][0:c_len, 0:gw[g]], 0.0) for g in n_grp]
        rb = [jnp.where(incl[g], gbk[g][c_len:, 0:gw[g]], 0.0).astype(BF16) for g in n_grp]
        akrk = [jnp.concatenate([jnp.where(strict[g], gbk[g][0:c_len, gw[g]:], 0.0),
                                 jnp.where(incl[g], gbk[g][c_len:, gw[g]:], 0.0)], axis=0).astype(BF16)
                for g in n_grp]
        lb = [x.astype(BF16) for x in lmat]
        pw = [_dot(lb[g], bd(lb[g], g)) for g in n_grp]
        tinv = [masks[gw[g]][2] + lmat[g] for g in n_grp]
        s0 = [_dot_nt(ar[g], bd(smb[g], g)) for g in n_grp]
        kv = [_dot(akrk[g], bd(v_p[g], g)) for g in n_grp]
        for _ in range(n_sq - 1):
            pb = [x.astype(BF16) for x in pw]
            sq = [_dot(jnp.concatenate([pb[g], tinv[g].astype(BF16)], axis=0), bd(pb[g], g)) for g in n_grp]
            tinv = [tinv[g] + sq[g][c_len:] for g in n_grp]
            pw = [sq[g][0:c_len] for g in n_grp]
        tinv = [tinv[g] + _dot(tinv[g], bd(pw[g].astype(BF16), g)) for g in n_grp]
        ub = [_dot(tinv[g], bd((s0[g][0:c_len] + kv[g][0:c_len]).astype(BF16), g)).astype(BF16) for g in n_grp]
        ys = [s0[g][c_len:] + _dot(rb[g], bd(ub[g], g)) + kv[g][c_len:] for g in n_grp]
        for g in n_grp:
            full = _dot_tn(jnp.concatenate([ub[g], v_p[g]], axis=0), jnp.concatenate([be_p[g], ke_p[g]], axis=0))
            full = jnp.where(masks[gw[g]][3], full, 0.0)
            upd = functools.reduce(lambda a, b: a + b, [full[u * HEAD_DIM:(u + 1) * HEAD_DIM]
                                                        for u in range(gw[g] // HEAD_DIM)])
            p_end_g = jnp.concatenate([p_end[q][:, j * V7X_LANES:(j + 1) * V7X_LANES] for q, j in groups[g]], axis=1)
            state[g, :, 0:gw[g]] = sm[g] * p_end_g + upd
        piece = {qj: ys[g][:, i * V7X_LANES:(i + 1) * V7X_LANES] for g in n_grp for i, qj in enumerate(groups[g])}
        y = jnp.concatenate([jnp.concatenate([piece[(q, j)] for j in range(width // V7X_LANES)], axis=1)
                             for q in range(nseq)], axis=0)

        inv_e = 1.0 / HEAD_DIM
        mu = _dot(y, seg) * inv_e
        yc = y - mu
        var = _dot(yc * yc, seg) * inv_e
        yn = yc * lax.rsqrt(var + GN_EPS) * lnw_ref[...] + lnb_ref[...]
        out = ((yn + bonus) * gate).astype(o_ref.dtype)
        for q in range(nseq):
            o_ref[q, sl, :] = out[rsl[q], :]
        return carry

    lax.fori_loop(0, tr // c_len, chunk, 0)


def _rwkv(z, batch, seq, prm):
    t, zin = z.shape
    width = prm["w0"].shape[-1]
    tr = min(RWKV_BLOCK, seq)
    nblk = seq // tr
    nseq = RWKV_SEQS if batch % RWKV_SEQS == 0 else 1
    ncol, per = width // V7X_LANES, RWKV_GROUP_LANES // V7X_LANES
    n_groups = nseq * (ncol // per) + -(-(nseq * (ncol % per)) // per)
    row = lambda x: x.reshape(1, -1).astype(F32)
    w2p = jnp.concatenate([prm["w2"], jnp.zeros((AAA_LORA, width), F32)], axis=0)
    a2p = jnp.concatenate([jnp.zeros((DECAY_LORA, width), F32), prm["a2"]], axis=0)
    hid = jnp.arange(width) // HEAD_DIM
    seg = (hid[:, None] == hid[None, :]).astype(BF16)
    pos = jnp.arange(nseq * RWKV_CHUNK)
    tri = ((pos[:, None] >= pos[None, :])
           & (pos[:, None] // RWKV_CHUNK == pos[None, :] // RWKV_CHUNK)).astype(BF16)
    params = [row(prm["mix"]), row(prm["w0"]), w2p, row(prm["a0"]), a2p.astype(BF16), prm["g2"].astype(BF16),
              row(prm["k_k"]), row(prm["k_a"]), row(prm["r_k"]), row(prm["ln_w"]), row(prm["ln_b"]),
              seg, tri]
    out = pl.pallas_call(
        functools.partial(_rwkv_body, tr=tr, width=width),
        grid=(batch // nseq, nblk),
        in_specs=[pl.BlockSpec((nseq, tr, zin), lambda b, n: (b, n, 0))]
                 + [_resident(p.shape) for p in params],
        out_specs=pl.BlockSpec((nseq, tr, width), lambda b, n: (b, n, 0)),
        out_shape=jax.ShapeDtypeStruct((batch, seq, width), BF16),
        scratch_shapes=[pltpu.VMEM((nseq, V7X_SUBLANES, zin), F32), pltpu.VMEM((nseq, tr, zin), F32),
                        pltpu.VMEM((n_groups, HEAD_DIM, RWKV_GROUP_LANES), F32)],
        compiler_params=_cparams("arbitrary", "arbitrary"),
        name="rwkv7_time_mix",
    )(z.reshape(batch, seq, zin), *params)
    return out.reshape(t, width)


def _ssm_prep_body(are_ref, aim_ref, lstep_ref, bre_ref, bim_ref, cre_ref, cim_ref,
                   kt_ref, wre_ref, wim_ref, vre_ref, vim_ref, pre_ref, pim_ref):
    lam_re = are_ref[0]
    lam_im = aim_ref[0]
    step = jnp.exp(lstep_ref[0])
    mag = jnp.exp(lam_re * step)
    ang = lam_im * step
    abar_re, abar_im = mag * jnp.cos(ang), mag * jnp.sin(ang)
    inv = 1.0 / (lam_re * lam_re + lam_im * lam_im)
    f_re = ((abar_re - 1.0) * lam_re + abar_im * lam_im) * inv
    f_im = (abar_im * lam_re - (abar_re - 1.0) * lam_im) * inv
    b_re, b_im = bre_ref[0], bim_ref[0]
    bbar_re = f_re * b_re - f_im * b_im
    bbar_im = f_re * b_im + f_im * b_re
    c_re, c_im = cre_ref[0], cim_ref[0]

    def power(tau):
        m = jnp.exp(lam_re * step * tau)
        return m * jnp.cos(ang * tau), m * jnp.sin(ang * tau)

    nt = (((1,), (1,)), ((), ()))
    for tau in range(SSM_CHUNK):
        p_re, p_im = power(float(tau))
        cp_re = c_re * p_re - c_im * p_im
        cp_im = c_re * p_im + c_im * p_re
        kt_ref[0, tau] = _dot_hi(cp_re, bbar_re, nt) - _dot_hi(cp_im, bbar_im, nt)
        q_re, q_im = power(float(SSM_CHUNK - 1 - tau))
        wre_ref[0, tau] = q_re * bbar_re - q_im * bbar_im
        wim_ref[0, tau] = q_re * bbar_im + q_im * bbar_re
        s_re, s_im = power(float(tau + 1))
        vre_ref[0, tau] = c_re * s_re - c_im * s_im
        vim_ref[0, tau] = -(c_re * s_im + c_im * s_re)
    for lvl in range(SSM_SCAN_LEVELS):
        p_re, p_im = power(float(SSM_CHUNK * 2 ** lvl))
        pre_ref[0, lvl:lvl + 1, :] = p_re
        pim_ref[0, lvl:lvl + 1, :] = p_im


def _ssm_prep(a_re, a_im, log_step, b_re, b_im, c_re, c_im):
    g, p = a_re.shape
    c = SSM_GROUP_CH
    l = SSM_CHUNK
    grp3 = lambda shape: pl.BlockSpec((1,) + shape, lambda i: (i, 0, 0))
    grp4 = lambda shape: pl.BlockSpec((1,) + shape, lambda i: (i, 0, 0, 0))
    sds = jax.ShapeDtypeStruct
    return pl.pallas_call(
        _ssm_prep_body,
        grid=(g,),
        in_specs=[grp3((1, p)), grp3((1, p)), grp3((1, 1)), grp3((c, p)), grp3((c, p)),
                  grp3((c, p)), grp3((c, p))],
        out_specs=[grp4((l, c, c)), grp4((l, c, p)), grp4((l, c, p)), grp4((l, c, p)), grp4((l, c, p)),
                   grp3((SSM_SCAN_LEVELS, p)), grp3((SSM_SCAN_LEVELS, p))],
        out_shape=[sds((g, l, c, c), F32), sds((g, l, c, p), F32), sds((g, l, c, p), F32),
                   sds((g, l, c, p), F32), sds((g, l, c, p), F32),
                   sds((g, SSM_SCAN_LEVELS, p), F32), sds((g, SSM_SCAN_LEVELS, p), F32)],
        compiler_params=_cparams("arbitrary"),
        name="ssm_discretise",
    )(a_re.reshape(g, 1, p), a_im.reshape(g, 1, p), log_step.reshape(g, 1, 1),
      jnp.swapaxes(b_re, 1, 2), jnp.swapaxes(b_im, 1, 2), c_re, c_im)


def _ssm_scan_body(u_ref, kd_ref, wd_ref, vd_ref, mre_ref, mim_ref, d_ref, y_ref, *, n_chunks):
    l = SSM_CHUNK
    n_half = u_ref.shape[0]
    row = lax.broadcasted_iota(jnp.int32, (n_chunks, 1), 0)
    for h in range(n_half):
        x32 = [u_ref[h, pl.ds(j, n_chunks, stride=l), :] for j in range(l)]
        xb = [x.astype(BF16) for x in x32]
        z = jnp.dot(jnp.concatenate(xb, axis=1), wd_ref[h], preferred_element_type=F32)
        half = z.shape[1] // 2
        z_re, z_im = z[:, :half], z[:, half:]
        for lvl in range(SSM_SCAN_LEVELS):
            s = 2 ** lvl
            if s >= n_chunks:
                break
            s_re = jnp.where(row >= s, pltpu.roll(z_re, s, 0), 0.0)
            s_im = jnp.where(row >= s, pltpu.roll(z_im, s, 0), 0.0)
            a_re = mre_ref[h, lvl:lvl + 1, :]
            a_im = mim_ref[h, lvl:lvl + 1, :]
            z_re, z_im = z_re + s_re * a_re - s_im * a_im, z_im + s_re * a_im + s_im * a_re
        zin = jnp.concatenate([jnp.where(row >= 1, pltpu.roll(z_re, 1, 0), 0.0),
                               jnp.where(row >= 1, pltpu.roll(z_im, 1, 0), 0.0)], axis=1)
        y_state = jnp.dot(zin.astype(BF16), vd_ref[h], preferred_element_type=F32)
        dvec = d_ref[h]
        lanes = x32[0].shape[1]
        for i in range(0, l, 2):
            lags = jnp.concatenate([xb[i + 1 - tau] for tau in range(i + 2)], axis=1)
            pair = jnp.dot(lags, kd_ref[h, 0:(i + 2) * lanes, :], preferred_element_type=F32)
            for k in range(2):
                y = (pair[:, k * lanes:(k + 1) * lanes]
                     + y_state[:, (i + k) * lanes:(i + k + 1) * lanes] + x32[i + k] * dvec)
                y_ref[h, pl.ds(i + k, n_chunks, stride=l), :] = y


def _ssm_scan(u, kd, wd, vd, m_re, m_im, dvec, batch, seq):
    n_half, t, lanes = u.shape
    n_chunks = seq // SSM_CHUNK
    weights = [kd, wd, vd, m_re, m_im, dvec]
    tok = pl.BlockSpec((n_half, seq, lanes), lambda b: (0, b, 0))
    return pl.pallas_call(
        functools.partial(_ssm_scan_body, n_chunks=n_chunks),
        grid=(batch,),
        in_specs=[tok] + [_resident(w.shape) for w in weights],
        out_specs=tok,
        out_shape=jax.ShapeDtypeStruct(u.shape, F32),
        compiler_params=_cparams("arbitrary"),
        name="ssm_chunk_scan",
    )(u, *weights)


def _ssm(u_slabs, batch, seq, prm):
    n_half, t, lanes = u_slabs.shape
    c, l, p = SSM_GROUP_CH, SSM_CHUNK, SSM_STATE
    gh = lanes // c
    kt, wre, wim, vre, vim, pre, pim = _ssm_prep(prm["a_re"], prm["a_im"], prm["log_step"],
                                                 prm["b_re"], prm["b_im"], prm["c_re"], prm["c_im"])
    eye = jnp.eye(gh, dtype=BF16)
    kd = jnp.einsum('hgtcd,gk->htgdkc', kt.astype(BF16).reshape(n_half, gh, l, c, c), eye)
    kd = kd.reshape(n_half, l * lanes, lanes)
    kd_down = jnp.concatenate([jnp.zeros((n_half, lanes, lanes), BF16), kd[:, :-lanes]], axis=1)
    kd = jnp.concatenate([kd_down, kd], axis=2)
    wd = jnp.einsum('rhgjdp,gk->hjgdrkp', jnp.stack([wre, wim]).astype(BF16).reshape(2, n_half, gh, l, c, p), eye)
    wd = wd.reshape(n_half, l * lanes, 2 * gh * p)
    vd = jnp.einsum('rhgicp,gk->hrgpikc', jnp.stack([vre, vim]).astype(BF16).reshape(2, n_half, gh, l, c, p), eye)
    vd = vd.reshape(n_half, 2 * gh * p, l * lanes)
    lv = SSM_SCAN_LEVELS
    m_re = jnp.transpose(pre.reshape(n_half, gh, lv, p), (0, 2, 1, 3)).reshape(n_half, lv, gh * p)
    m_im = jnp.transpose(pim.reshape(n_half, gh, lv, p), (0, 2, 1, 3)).reshape(n_half, lv, gh * p)
    dvec = prm["d"].reshape(n_half, 1, lanes).astype(F32)
    return _ssm_scan(u_slabs, kd, wd, vd, m_re, m_im, dvec, batch, seq)


def _mix_ffn_body(oa_ref, or_ref, ys_ref, gate_ref, h_ref, wa_ref, wr_ref, ws_ref, wv_ref, wg_ref, wo_ref,
                  gain_ref, wgu_ref, wd_ref, fgain_ref, out_ref, *, hidden, final_norm):
    d = h_ref.shape[1]
    y = jnp.concatenate([ys_ref[i] for i in range(ys_ref.shape[0])], axis=1)
    zg = (0.5 * y * (1.0 + jnp.tanh(math.sqrt(2.0 / math.pi) * (y + 0.044715 * (y * y * y))))).astype(BF16)
    o_ssm = (jnp.dot(zg, wv_ref[...], preferred_element_type=F32)
             * _sigmoid(jnp.dot(zg, wg_ref[...], preferred_element_type=F32)))
    gate = lambda i: gate_ref[:, i * d:(i + 1) * d].astype(F32)
    merged = (gate(0) * _dot(oa_ref[...], wa_ref[...])
              + gate(1) * _dot(or_ref[...], wr_ref[...])
              + gate(2) * _dot(o_ssm, ws_ref[...]))
    x = h_ref[...] + _dot(merged, wo_ref[...])

    u = _rmsnorm(x, gain_ref[...]).astype(BF16)
    acc = jnp.zeros(x.shape, F32)
    for c in range(hidden // FFN_CHUNK):
        lo = c * FFN_CHUNK
        a = jnp.dot(u, wgu_ref[:, lo:lo + FFN_CHUNK], preferred_element_type=F32)
        b = jnp.dot(u, wgu_ref[:, hidden + lo:hidden + lo + FFN_CHUNK], preferred_element_type=F32)
        hm = (a * _sigmoid(a) * b).astype(BF16)
        acc = acc + jnp.dot(hm, wd_ref[lo:lo + FFN_CHUNK, :], preferred_element_type=F32)
    x = x + acc
    if final_norm:
        x = _rmsnorm(x, fgain_ref[...])
    out_ref[...] = x


def _mix_ffn(o_attn, o_rwkv, y_ssm, gates, h, mix_weights, gain, wgu, wd, final_gain, final_norm):
    t, d = h.shape
    tm = ROW_TILE
    rows = lambda a: pl.BlockSpec((tm, a.shape[1]), lambda i: (i, 0))
    slabs = lambda a: pl.BlockSpec((a.shape[0], tm, a.shape[2]), lambda i: (0, i, 0))
    weights = list(mix_weights) + [gain.reshape(1, d), wgu, wd, final_gain.reshape(1, d)]
    return pl.pallas_call(
        functools.partial(_mix_ffn_body, hidden=wd.shape[0], final_norm=final_norm),
        grid=(t // tm,),
        in_specs=[rows(o_attn), rows(o_rwkv), slabs(y_ssm), rows(gates), rows(h)]
                 + [_resident(w.shape) for w in weights],
        out_specs=rows(h),
        out_shape=jax.ShapeDtypeStruct((t, d), F32),
        compiler_params=_cparams("arbitrary"),
        name="merge_ffn",
    )(o_attn, o_rwkv, y_ssm, gates, h, *weights)


def kernel(x, norm_mix, w_in, rwkv_shift_mix, rwkv_w0, rwkv_w2, rwkv_a0, rwkv_a2, rwkv_g2, rwkv_k_k, rwkv_k_a, rwkv_r_k, rwkv_ln_w, rwkv_ln_b, ssm_a_re, ssm_a_im, ssm_log_step, ssm_b_re, ssm_b_im, ssm_c_re, ssm_c_im, ssm_d, ssm_glu_val, ssm_glu_gate, w_branch_attn, w_branch_rwkv, w_branch_ssm, w_out, norm_ffn, ffn_w_gate_up, ffn_w_down, norm_final):
    batch, seq, d = x.shape
    depth = w_in.shape[0]
    attn_w = w_branch_attn.shape[1]
    rwkv_in = rwkv_shift_mix.shape[1]
    ssm_w = w_branch_ssm.shape[1]
    widths = (3 * attn_w, rwkv_in, ssm_w, N_BRANCH * d)
    assert sum(widths) == w_in.shape[2]
    assert attn_w == len(ATTN_GROUPS) * HEADS_PER_GROUP * HEAD_DIM
    assert seq % ATTN_BLOCK == 0 or seq in (w for w, _ in ATTN_GROUPS)
    assert seq // SSM_CHUNK <= 2 ** SSM_SCAN_LEVELS and (batch * seq) % ROW_TILE == 0

    h = x.reshape(batch * seq, d)
    for l in range(depth):
        qkv, z_rwkv, u_ssm, gates = _in_proj(h, norm_mix[l], w_in[l].astype(BF16), widths)
        o_attn = _attention(qkv, batch, seq)
        o_rwkv = _rwkv(z_rwkv, batch, seq, dict(
            mix=rwkv_shift_mix[l], w0=rwkv_w0[l], w2=rwkv_w2[l], a0=rwkv_a0[l], a2=rwkv_a2[l],
            g2=rwkv_g2[l], k_k=rwkv_k_k[l], k_a=rwkv_k_a[l], r_k=rwkv_r_k[l],
            ln_w=rwkv_ln_w[l], ln_b=rwkv_ln_b[l]))
        y_ssm = _ssm(u_ssm, batch, seq, dict(
            a_re=ssm_a_re[l], a_im=ssm_a_im[l], log_step=ssm_log_step[l], b_re=ssm_b_re[l],
            b_im=ssm_b_im[l], c_re=ssm_c_re[l], c_im=ssm_c_im[l], d=ssm_d[l]))
        bf = lambda w: w.astype(BF16)
        h = _mix_ffn(o_attn, o_rwkv, y_ssm, gates, h,
                     [bf(w_branch_attn[l]), bf(w_branch_rwkv[l]), bf(w_branch_ssm[l]),
                      bf(ssm_glu_val[l]), bf(ssm_glu_gate[l]), bf(w_out[l])],
                     norm_ffn[l], bf(ffn_w_gate_up[l]), bf(ffn_w_down[l]), norm_final, l == depth - 1)
    return h.reshape(batch, seq, d)
```

```python
import functools
import math

import jax
import jax.numpy as jnp
from jax import lax
from jax.experimental import pallas as pl
from jax.experimental.pallas import tpu as pltpu

F32 = jnp.float32
BF16 = jnp.bfloat16

HEAD_DIM = 64
ATTN_GROUPS = ((128, 1), (512, 4), (2048, 16))
ATTN_N_BACK = 128
HEADS_PER_GROUP = 2
DECAY_LORA = 64
AAA_LORA = 64
GATE_LORA = 128
SSM_GROUP_CH = 16
SSM_STATE = 64
GN_EPS = 64e-5
NORM_EPS = 1e-6
N_BRANCH = 3

V7X_LANES = 128
V7X_SUBLANES = 8
V7X_MXU_DIM = 256
V7X_VMEM_BYTES = 64 * 1024 * 1024
VMEM_LIMIT = V7X_VMEM_BYTES - 8 * 1024 * 1024

ROW_TILE = 512
COL_CHUNK = 512
FFN_CHUNK = 256
ATTN_BLOCK = 2048
ATTN_UNROLL = 4
RWKV_BLOCK = 128
RWKV_CHUNK = 64
RWKV_SEQS = 8
RWKV_GROUP_LANES = 256
SSM_CHUNK = 16
SSM_SCAN_LEVELS = 8


def _cparams(*sem):
    return pltpu.CompilerParams(dimension_semantics=sem, vmem_limit_bytes=VMEM_LIMIT)


def _resident(shape):
    nd = len(shape)
    return pl.BlockSpec(shape, lambda *_: (0,) * nd, pipeline_mode=pl.Buffered(1))


def _resident_layer(stacked, layer):
    nd = stacked.ndim - 1
    return pl.BlockSpec((None,) + stacked.shape[1:], lambda *_: (layer,) + (0,) * nd,
                        pipeline_mode=pl.Buffered(1))


def _dot(a, b):
    return jnp.dot(a.astype(BF16), b.astype(BF16), preferred_element_type=F32)


def _dot_nt(a, b):
    return lax.dot_general(a.astype(BF16), b.astype(BF16), (((1,), (1,)), ((), ())),
                           preferred_element_type=F32)


def _dot_tn(a, b):
    return lax.dot_general(a.astype(BF16), b.astype(BF16), (((0,), (0,)), ((), ())),
                           preferred_element_type=F32)


def _split(x, terms):
    out, rest = [], x
    for _ in range(terms - 1):
        h = rest.astype(BF16)
        out.append(h)
        rest = rest - h.astype(F32)
    out.append(rest.astype(BF16))
    return out


def _dot_xw(x, w, terms=3):
    m = x.shape[0]
    y = jnp.dot(jnp.concatenate(_split(x, terms), axis=0), w, preferred_element_type=F32)
    return functools.reduce(lambda a, b: a + b, [y[i * m:(i + 1) * m] for i in range(terms)])


def _dot_wx(w, x, terms=3):
    n = x.shape[1]
    y = jnp.dot(w, jnp.concatenate(_split(x, terms), axis=1), preferred_element_type=F32)
    return functools.reduce(lambda a, b: a + b, [y[:, i * n:(i + 1) * n] for i in range(terms)])


def _dot_hi(a, b, dims=(((1,), (0,)), ((), ()))):
    ah = a.astype(BF16)
    al = (a - ah.astype(F32)).astype(BF16)
    bh = b.astype(BF16)
    bl = (b - bh.astype(F32)).astype(BF16)
    dg = functools.partial(lax.dot_general, dimension_numbers=dims, preferred_element_type=F32)
    return dg(ah, bh) + dg(ah, bl) + dg(al, bh)


def _rmsnorm(x, gain):
    ms = jnp.mean(x * x, axis=-1, keepdims=True)
    return x * lax.rsqrt(ms + NORM_EPS) * gain


def _sigmoid(x):
    return 1.0 / (1.0 + jnp.exp(-x))


def _in_proj_body(h_ref, gain_ref, w_ref, qkv_ref, rw_ref, ssm_ref, gate_ref, *, widths):
    attn_in, rwkv_in, ssm_w, gate_w = widths
    u = _rmsnorm(h_ref[...], gain_ref[...]).astype(BF16)

    rw_off = attn_in
    ssm_off = rw_off + rwkv_in
    gate_off = ssm_off + ssm_w
    total = gate_off + gate_w

    def store(col, y):
        if col < rw_off:
            qkv_ref[col // V7X_LANES] = y
        elif col < ssm_off:
            rw_ref[:, col - rw_off:col - rw_off + V7X_LANES] = y
        elif col < gate_off:
            ssm_ref[(col - ssm_off) // V7X_LANES] = y
        else:
            gate_ref[:, col - gate_off:col - gate_off + V7X_LANES] = _sigmoid(y).astype(gate_ref.dtype)

    for lo in range(0, total, COL_CHUNK):
        hi = min(lo + COL_CHUNK, total)
        y = jnp.dot(u, w_ref[:, lo:hi], preferred_element_type=F32)
        for j in range((hi - lo) // V7X_LANES):
            store(lo + j * V7X_LANES, y[:, j * V7X_LANES:(j + 1) * V7X_LANES])


def _in_proj(h, gain, w_stack, layer, widths):
    t, d = h.shape
    attn_in, rwkv_in, ssm_w, gate_w = widths
    n_slab = attn_in // V7X_LANES
    tm = ROW_TILE
    return pl.pallas_call(
        functools.partial(_in_proj_body, widths=widths),
        grid=(t // tm,),
        in_specs=[pl.BlockSpec((tm, d), lambda i: (i, 0)),
                  _resident((1, d)),
                  _resident_layer(w_stack, layer)],
        out_specs=[pl.BlockSpec((n_slab, tm, V7X_LANES), lambda i: (0, i, 0)),
                   pl.BlockSpec((tm, rwkv_in), lambda i: (i, 0)),
                   pl.BlockSpec((ssm_w // V7X_LANES, tm, V7X_LANES), lambda i: (0, i, 0)),
                   pl.BlockSpec((tm, gate_w), lambda i: (i, 0))],
        out_shape=[jax.ShapeDtypeStruct((n_slab, t, V7X_LANES), F32),
                   jax.ShapeDtypeStruct((t, rwkv_in), F32),
                   jax.ShapeDtypeStruct((ssm_w // V7X_LANES, t, V7X_LANES), F32),
                   jax.ShapeDtypeStruct((t, gate_w), BF16)],
        compiler_params=_cparams("arbitrary"),
        name="in_proj",
    )(h, gain.reshape(1, d), w_stack)


def _attn_body(qkv_ref, o_ref, kb0, vb0, kb1, vb1, kb2, vb2, obuf, lbuf, *, tb):
    n = pl.program_id(1)
    nb = ATTN_N_BACK
    kbufs, vbufs = (kb0, kb1, kb2), (vb0, vb1, vb2)
    n_grp = len(ATTN_GROUPS)
    lane = lax.broadcasted_iota(jnp.int32, (nb, V7X_LANES), 1)
    head0 = lane < HEAD_DIM
    qi = lax.broadcasted_iota(jnp.int32, (nb, 2 * nb), 0)
    kj = lax.broadcasted_iota(jnp.int32, (nb, 2 * nb), 1)
    band = (kj >= qi) & (kj <= qi + nb)
    scale = HEAD_DIM ** -0.5

    for g, (window, dil) in enumerate(ATTN_GROUPS):
        span = window
        kb, vb = kbufs[g], vbufs[g]

        @pl.when(n == 0)
        def _():
            kb[0:span, :] = jnp.zeros((span, V7X_LANES), F32)
            vb[0:span, :] = jnp.zeros((span, V7X_LANES), F32)

        @pl.when(n > 0)
        def _():
            kb[0:span, :] = kb[tb:tb + span, :]
            vb[0:span, :] = vb[tb:tb + span, :]

        kb[span:span + tb, :] = qkv_ref[n_grp + g]
        vb[span:span + tb, :] = qkv_ref[2 * n_grp + g]
        n_sub = tb // span

        def units(it, carry, g=g, dil=dil, span=span, kb=kb, vb=vb, n_sub=n_sub):
            bases, masks, qs, k2s, v2s = [], [], [], [], []
            for k in range(ATTN_UNROLL):
                idx = it * ATTN_UNROLL + k
                j = idx // dil
                base = j * span + (idx - j * dil)
                bases.append(base)
                masks.append(band & ((kj >= nb) | ((n * n_sub + j) > 0)))
                qs.append(qkv_ref[g, pl.ds(base, nb, stride=dil), :] * scale)
                k2s.append(kb[pl.ds(base, 2 * nb, stride=dil), :].astype(BF16))
                v2s.append(vb[pl.ds(base, 2 * nb, stride=dil), :].astype(BF16))
            scores = [[_dot_nt(jnp.where(hmask, qs[k], 0.0), k2s[k]) for hmask in (head0, ~head0)]
                      for k in range(ATTN_UNROLL)]
            probs, dens, lses = [], [], []
            for k in range(ATTN_UNROLL):
                for s in scores[k]:
                    s = jnp.where(masks[k], s, -1e30)
                    mx = jnp.max(s, axis=-1, keepdims=True)
                    p = jnp.exp(s - mx)
                    den = jnp.sum(p, axis=-1, keepdims=True)
                    probs.append(p.astype(BF16))
                    dens.append(den)
                    lses.append(mx + jnp.log(den))
            for k in range(ATTN_UNROLL):
                o0 = jnp.dot(probs[2 * k], v2s[k], preferred_element_type=F32) / dens[2 * k]
                o1 = jnp.dot(probs[2 * k + 1], v2s[k], preferred_element_type=F32) / dens[2 * k + 1]
                obuf[g, pl.ds(bases[k], nb, stride=dil), :] = jnp.where(head0, o0, o1)
                lbuf[g, pl.ds(bases[k], nb, stride=dil), :] = jnp.where(head0, lses[2 * k], lses[2 * k + 1])
            return carry

        lax.fori_loop(0, n_sub * dil // ATTN_UNROLL, units, 0)

    rows = 256
    def mix(i, carry):
        sl = pl.ds(pl.multiple_of(i * rows, rows), rows)
        ls = [lbuf[g, sl, :] for g in range(n_grp)]
        m = functools.reduce(jnp.maximum, ls)
        es = [jnp.exp(l - m) for l in ls]
        tot = functools.reduce(lambda a, b: a + b, es)
        for g in range(n_grp):
            o_ref[sl, g * V7X_LANES:(g + 1) * V7X_LANES] = (obuf[g, sl, :] * (es[g] / tot)).astype(o_ref.dtype)
        return carry
    lax.fori_loop(0, tb // rows, mix, 0)


def _attention(qkv, batch, seq):
    n_slab, t, _ = qkv.shape
    tb = min(ATTN_BLOCK, seq)
    nblk = seq // tb
    n_grp = len(ATTN_GROUPS)
    scratch = []
    for window, _ in ATTN_GROUPS:
        scratch += [pltpu.VMEM((window + tb, V7X_LANES), F32)] * 2
    scratch += [pltpu.VMEM((n_grp, tb, V7X_LANES), F32)] * 2
    return pl.pallas_call(
        functools.partial(_attn_body, tb=tb),
        grid=(batch, nblk),
        in_specs=[pl.BlockSpec((n_slab, tb, V7X_LANES), lambda b, n: (0, b * nblk + n, 0))],
        out_specs=pl.BlockSpec((tb, n_grp * V7X_LANES), lambda b, n: (b * nblk + n, 0)),
        out_shape=jax.ShapeDtypeStruct((t, n_grp * V7X_LANES), BF16),
        scratch_shapes=scratch,
        compiler_params=_cparams("arbitrary", "arbitrary"),
        name="dilated_attention",
    )(qkv)


def _rwkv_body(z_ref, mix_ref, w0_ref, w2_ref, a0_ref, a2_ref, g2_ref, kk_ref, ka_ref, rk_ref,
               lnw_ref, lnb_ref, seg_ref, tri_ref, o_ref, carry, zz_s, state, *, tr, width):
    n = pl.program_id(1)
    nseq = z_ref.shape[0]
    c_len = RWKV_CHUNK
    pad = carry.shape[1]

    @pl.when(n == 0)
    def _():
        carry[...] = jnp.zeros(carry.shape, F32)
        state[...] = jnp.zeros(state.shape, F32)

    mix = mix_ref[...]
    for q in range(nseq):
        for i in range(tr // c_len):
            cur = z_ref[q, i * c_len:(i + 1) * c_len, :]
            head = carry[q] if i == 0 else z_ref[q, i * c_len - pad:i * c_len, :]
            prev = pltpu.roll(jnp.concatenate([head, cur], axis=0), 1, 0)[pad:]
            zz_s[q, i * c_len:(i + 1) * c_len, :] = cur + (prev - cur) * mix
    carry[...] = z_ref[:, tr - pad:tr, :]

    seg = seg_ref[...]
    tri = tri_ref[...]
    lora = DECAY_LORA + AAA_LORA
    n_sq = int(math.log2(c_len)) - 1
    rsl = [slice(q * c_len, (q + 1) * c_len) for q in range(nseq)]
    per = RWKV_GROUP_LANES // V7X_LANES
    cols = [(q, j) for q in range(nseq) for j in range(width // V7X_LANES)]
    whole = [cols[i:i + per] for q in range(nseq)
             for i in range(q * (width // V7X_LANES), (q + 1) * (width // V7X_LANES) - per + 1, per)]
    rest = [c for c in cols if not any(c in g for g in whole)]
    groups = whole + [rest[i:i + per] for i in range(0, len(rest), per)]
    n_grp = range(len(groups))
    gw = [len(g) * V7X_LANES for g in groups]

    def group_masks(w):
        r = lax.broadcasted_iota(jnp.int32, (c_len, w), 0)
        l = lax.broadcasted_iota(jnp.int32, (c_len, w), 1) % HEAD_DIM
        br = lax.broadcasted_iota(jnp.int32, (w, w), 0) // HEAD_DIM
        bl = lax.broadcasted_iota(jnp.int32, (w, w), 1) // HEAD_DIM
        return l < r, l <= r, (l == r).astype(F32), br == bl
    masks = {w: group_masks(w) for w in set(gw)}

    def chunk(c, carry):
        sl = pl.ds(pl.multiple_of(c * c_len, c_len), c_len)
        zz = jnp.concatenate([zz_s[q, sl, :] for q in range(nseq)], axis=0)
        r = zz[:, 0:width]
        k = zz[:, width:2 * width]
        v = zz[:, 2 * width:3 * width]
        t = zz[:, 3 * width:3 * width + lora]
        xg = zz[:, 3 * width + lora:3 * width + lora + GATE_LORA]

        wpre = w0_ref[...] + _dot_hi(jnp.tanh(t), w2_ref[...])
        nw = -wpre
        softplus = jnp.maximum(nw, 0.0) + jnp.log(1.0 + jnp.exp(-jnp.abs(nw)))
        lw = -jnp.exp(-softplus - 0.5)
        a = _sigmoid(a0_ref[...] + _dot(t, a2_ref[...]))
        gate = _dot(_sigmoid(xg), g2_ref[...])

        kk = k * kk_ref[...]
        kk = kk * lax.rsqrt(jnp.maximum(_dot(kk * kk, seg), 1e-24))
        k = k * (1.0 + (a - 1.0) * ka_ref[...])
        bonus = _dot(r * k * rk_ref[...], seg) * v
        avec = -kk
        bvec = kk * a

        lc = jnp.concatenate([_dot_wx(tri, lw[i:i + tri.shape[0]], terms=2)
                              for i in range(0, nseq * c_len, tri.shape[0])], axis=0)
        lc_last = [lc[s.stop - 1:s.stop, :] for s in rsl]
        e_pos = jnp.exp(lc)
        e_neg = jnp.exp(-lc)
        rt = (r * e_pos).astype(BF16)
        at = (avec * jnp.exp(lc - lw)).astype(BF16)
        kt = (k * e_neg).astype(BF16)
        bt = (bvec * e_neg).astype(BF16)
        p_end = [jnp.exp(e) for e in lc_last]
        vb = v.astype(BF16)

        pack = lambda x, g: jnp.concatenate(
            [x[rsl[q], j * V7X_LANES:(j + 1) * V7X_LANES] for q, j in groups[g]], axis=1)
        bd = lambda x, g: jnp.where(masks[gw[g]][3], jnp.concatenate([x] * (gw[g] // HEAD_DIM), axis=0), 0)
        strict = [masks[w][0] for w in gw]
        incl = [masks[w][1] for w in gw]
        a_p, r_p, b_p, k_p, v_p = ([pack(x, g) for g in n_grp] for x in (at, rt, bt, kt, vb))
        sm = [state[g, :, 0:gw[g]] for g in n_grp]
        smb = [m.astype(BF16) for m in sm]
        ar = [jnp.concatenate([a_p[g], r_p[g]], axis=0) for g in n_grp]
        gbk = [_dot_nt(ar[g], jnp.concatenate([bd(b_p[g], g), bd(k_p[g], g)], axis=0)) for g in n_grp]
        lmat = [jnp.where(strict[g], gbk[g][0:c_len, 0:gw[g]], 0.0) for g in n_grp]
        rb = [jnp.where(incl[g], gbk[g][c_len:, 0:gw[g]], 0.0).astype(BF16) for g in n_grp]
        akrk = [jnp.concatenate([jnp.where(strict[g], gbk[g][0:c_len, gw[g]:], 0.0),
                                 jnp.where(incl[g], gbk[g][c_len:, gw[g]:], 0.0)], axis=0).astype(BF16)
                for g in n_grp]
        lb = [x.astype(BF16) for x in lmat]
        pw = [_dot(lb[g], bd(lb[g], g)) for g in n_grp]
        tinv = [masks[gw[g]][2] + lmat[g] for g in n_grp]
        s0 = [_dot_nt(ar[g], bd(smb[g], g)) for g in n_grp]
        kv = [_dot(akrk[g], bd(v_p[g], g)) for g in n_grp]
        for _ in range(n_sq - 1):
            pb = [x.astype(BF16) for x in pw]
            sq = [_dot(jnp.concatenate([pb[g], tinv[g].astype(BF16)], axis=0), bd(pb[g], g)) for g in n_grp]
            tinv = [tinv[g] + sq[g][c_len:] for g in n_grp]
            pw = [sq[g][0:c_len] for g in n_grp]
        tinv = [tinv[g] + _dot(tinv[g], bd(pw[g].astype(BF16), g)) for g in n_grp]
        ub = [_dot(tinv[g], bd((s0[g][0:c_len] + kv[g][0:c_len]).astype(BF16), g)).astype(BF16) for g in n_grp]
        ys = [s0[g][c_len:] + _dot(rb[g], bd(ub[g], g)) + kv[g][c_len:] for g in n_grp]
        for g in n_grp:
            full = _dot_tn(jnp.concatenate([ub[g], v_p[g]], axis=0), jnp.concatenate([b_p[g], k_p[g]], axis=0))
            full = jnp.where(masks[gw[g]][3], full, 0.0)
            upd = functools.reduce(lambda a, b: a + b, [full[u * HEAD_DIM:(u + 1) * HEAD_DIM]
                                                        for u in range(gw[g] // HEAD_DIM)])
            p_end_g = jnp.concatenate([p_end[q][:, j * V7X_LANES:(j + 1) * V7X_LANES] for q, j in groups[g]], axis=1)
            state[g, :, 0:gw[g]] = (sm[g] + upd) * p_end_g
        piece = {qj: ys[g][:, i * V7X_LANES:(i + 1) * V7X_LANES] for g in n_grp for i, qj in enumerate(groups[g])}
        y = jnp.concatenate([jnp.concatenate([piece[(q, j)] for j in range(width // V7X_LANES)], axis=1)
                             for q in range(nseq)], axis=0)

        inv_e = 1.0 / HEAD_DIM
        mu = _dot(y, seg) * inv_e
        yc = y - mu
        var = _dot(yc * yc, seg) * inv_e
        yn = yc * lax.rsqrt(var + GN_EPS) * lnw_ref[...] + lnb_ref[...]
        out = ((yn + bonus) * gate).astype(o_ref.dtype)
        for q in range(nseq):
            o_ref[q, sl, :] = out[rsl[q], :]
        return carry

    lax.fori_loop(0, tr // c_len, chunk, 0)


def _rwkv(z, batch, seq, prm):
    t, zin = z.shape
    width = prm["w0"].shape[-1]
    tr = min(RWKV_BLOCK, seq)
    nblk = seq // tr
    nseq = math.gcd(batch, RWKV_SEQS)
    ncol, per = width // V7X_LANES, RWKV_GROUP_LANES // V7X_LANES
    n_groups = nseq * (ncol // per) + -(-(nseq * (ncol % per)) // per)
    row = lambda x: x.reshape(1, -1).astype(F32)
    w2p = jnp.concatenate([prm["w2"], jnp.zeros((AAA_LORA, width), F32)], axis=0)
    a2p = jnp.concatenate([jnp.zeros((DECAY_LORA, width), F32), prm["a2"]], axis=0)
    hid = jnp.arange(width) // HEAD_DIM
    seg = (hid[:, None] == hid[None, :]).astype(BF16)
    pos = jnp.arange(math.gcd(nseq * RWKV_CHUNK, V7X_MXU_DIM))
    tri = ((pos[:, None] >= pos[None, :])
           & (pos[:, None] // RWKV_CHUNK == pos[None, :] // RWKV_CHUNK)).astype(BF16)
    params = [row(prm["mix"]), row(prm["w0"]), w2p, row(prm["a0"]), a2p.astype(BF16), prm["g2"].astype(BF16),
              row(prm["k_k"]), row(prm["k_a"]), row(prm["r_k"]), row(prm["ln_w"]), row(prm["ln_b"]),
              seg, tri]
    out = pl.pallas_call(
        functools.partial(_rwkv_body, tr=tr, width=width),
        grid=(batch // nseq, nblk),
        in_specs=[pl.BlockSpec((nseq, tr, zin), lambda b, n: (b, n, 0))]
                 + [_resident(p.shape) for p in params],
        out_specs=pl.BlockSpec((nseq, tr, width), lambda b, n: (b, n, 0)),
        out_shape=jax.ShapeDtypeStruct((batch, seq, width), BF16),
        scratch_shapes=[pltpu.VMEM((nseq, V7X_SUBLANES, zin), F32), pltpu.VMEM((nseq, tr, zin), F32),
                        pltpu.VMEM((n_groups, HEAD_DIM, RWKV_GROUP_LANES), F32)],
        compiler_params=_cparams("arbitrary", "arbitrary"),
        name="rwkv7_time_mix",
    )(z.reshape(batch, seq, zin), *params)
    return out.reshape(t, width)


def _ssm_prep_body(are_ref, aim_ref, lstep_ref, bre_ref, bim_ref, cre_ref, cim_ref,
                   kt_ref, wre_ref, wim_ref, vre_ref, vim_ref, pre_ref, pim_ref):
    lam_re = are_ref[0]
    lam_im = aim_ref[0]
    step = jnp.exp(lstep_ref[0])
    mag = jnp.exp(lam_re * step)
    ang = lam_im * step
    abar_re, abar_im = mag * jnp.cos(ang), mag * jnp.sin(ang)
    inv = 1.0 / (lam_re * lam_re + lam_im * lam_im)
    f_re = ((abar_re - 1.0) * lam_re + abar_im * lam_im) * inv
    f_im = (abar_im * lam_re - (abar_re - 1.0) * lam_im) * inv
    b_re, b_im = bre_ref[0], bim_ref[0]
    bbar_re = f_re * b_re - f_im * b_im
    bbar_im = f_re * b_im + f_im * b_re
    c_re, c_im = cre_ref[0], cim_ref[0]

    def power(tau):
        m = jnp.exp(lam_re * step * tau)
        return m * jnp.cos(ang * tau), m * jnp.sin(ang * tau)

    nt = (((1,), (1,)), ((), ()))
    for tau in range(SSM_CHUNK):
        p_re, p_im = power(float(tau))
        cp_re = c_re * p_re - c_im * p_im
        cp_im = c_re * p_im + c_im * p_re
        kt_ref[0, tau] = _dot_hi(cp_re, bbar_re, nt) - _dot_hi(cp_im, bbar_im, nt)
        q_re, q_im = power(float(SSM_CHUNK - 1 - tau))
        wre_ref[0, tau] = q_re * bbar_re - q_im * bbar_im
        wim_ref[0, tau] = q_re * bbar_im + q_im * bbar_re
        s_re, s_im = power(float(tau + 1))
        vre_ref[0, tau] = c_re * s_re - c_im * s_im
        vim_ref[0, tau] = -(c_re * s_im + c_im * s_re)
    for lvl in range(SSM_SCAN_LEVELS):
        p_re, p_im = power(float(SSM_CHUNK * 2 ** lvl))
        pre_ref[0, lvl:lvl + 1, :] = p_re
        pim_ref[0, lvl:lvl + 1, :] = p_im


def _ssm_prep(a_re, a_im, log_step, b_re, b_im, c_re, c_im):
    g, p = a_re.shape
    c = SSM_GROUP_CH
    l = SSM_CHUNK
    grp3 = lambda shape: pl.BlockSpec((1,) + shape, lambda i: (i, 0, 0))
    grp4 = lambda shape: pl.BlockSpec((1,) + shape, lambda i: (i, 0, 0, 0))
    sds = jax.ShapeDtypeStruct
    return pl.pallas_call(
        _ssm_prep_body,
        grid=(g,),
        in_specs=[grp3((1, p)), grp3((1, p)), grp3((1, 1)), grp3((c, p)), grp3((c, p)),
                  grp3((c, p)), grp3((c, p))],
        out_specs=[grp4((l, c, c)), grp4((l, c, p)), grp4((l, c, p)), grp4((l, c, p)), grp4((l, c, p)),
                   grp3((SSM_SCAN_LEVELS, p)), grp3((SSM_SCAN_LEVELS, p))],
        out_shape=[sds((g, l, c, c), F32), sds((g, l, c, p), F32), sds((g, l, c, p), F32),
                   sds((g, l, c, p), F32), sds((g, l, c, p), F32),
                   sds((g, SSM_SCAN_LEVELS, p), F32), sds((g, SSM_SCAN_LEVELS, p), F32)],
        compiler_params=_cparams("arbitrary"),
        name="ssm_discretise",
    )(a_re.reshape(g, 1, p), a_im.reshape(g, 1, p), log_step.reshape(g, 1, 1),
      jnp.swapaxes(b_re, 1, 2), jnp.swapaxes(b_im, 1, 2), c_re, c_im)


def _ssm_scan_body(u_ref, kc_ref, wc_ref, vc_ref, ek_ref, ew_ref, ev_ref, mre_ref, mim_ref, d_ref, y_ref,
                   kd_s, wd_s, vd_s, *, n_chunks):
    l = SSM_CHUNK
    n_half, _, lanes = u_ref.shape
    gh = lanes // SSM_GROUP_CH

    @pl.when(pl.program_id(0) == 0)
    def _():
        def expand(src, e_ref, row_group, lane_group):
            out = jnp.dot(src, e_ref[...], preferred_element_type=F32)
            rg = row_group(lax.broadcasted_iota(jnp.int32, out.shape, 0))
            lg = lane_group(lax.broadcasted_iota(jnp.int32, out.shape, 1))
            return jnp.where(rg == lg, out, 0.0).astype(BF16)

        rows = 512
        for h in range(n_half):
            kd_s[h, 0:lanes, 0:lanes] = jnp.zeros((lanes, lanes), BF16)
            for r0 in range(0, l * lanes, rows):
                blk = expand(kc_ref[h, r0:r0 + rows, :], ek_ref,
                             lambda r: (r // SSM_GROUP_CH) % gh, lambda c: c // SSM_GROUP_CH)
                kd_s[h, r0:r0 + rows, lanes:2 * lanes] = blk
                hi = min(r0 + lanes + rows, l * lanes)
                kd_s[h, r0 + lanes:hi, 0:lanes] = blk[0:hi - r0 - lanes]
            for ri in range(2):
                half = wd_s.shape[2] // 2
                for r0 in range(0, l * lanes, rows):
                    wd_s[h, r0:r0 + rows, ri * half:(ri + 1) * half] = expand(
                        wc_ref[h, ri, r0:r0 + rows, :], ew_ref,
                        lambda r: (r // SSM_GROUP_CH) % gh, lambda c: c // SSM_STATE)
                hv = vd_s.shape[1] // 2
                vd_s[h, ri * hv:(ri + 1) * hv, :] = expand(
                    vc_ref[h, ri], ev_ref, lambda r: r // SSM_STATE, lambda c: (c // SSM_GROUP_CH) % gh)

    row = lax.broadcasted_iota(jnp.int32, (n_chunks, 1), 0)
    for h in range(n_half):
        x32 = [u_ref[h, pl.ds(j, n_chunks, stride=l), :] for j in range(l)]
        xb = [x.astype(BF16) for x in x32]
        z = jnp.dot(jnp.concatenate(xb, axis=1), wd_s[h], preferred_element_type=F32)
        half = z.shape[1] // 2
        z_re, z_im = z[:, :half], z[:, half:]
        for lvl in range(SSM_SCAN_LEVELS):
            s = 2 ** lvl
            if s >= n_chunks:
                break
            s_re = jnp.where(row >= s, pltpu.roll(z_re, s, 0), 0.0)
            s_im = jnp.where(row >= s, pltpu.roll(z_im, s, 0), 0.0)
            a_re = mre_ref[h, lvl:lvl + 1, :]
            a_im = mim_ref[h, lvl:lvl + 1, :]
            z_re, z_im = z_re + s_re * a_re - s_im * a_im, z_im + s_re * a_im + s_im * a_re
        zin = jnp.concatenate([jnp.where(row >= 1, pltpu.roll(z_re, 1, 0), 0.0),
                               jnp.where(row >= 1, pltpu.roll(z_im, 1, 0), 0.0)], axis=1)
        y_state = jnp.dot(zin.astype(BF16), vd_s[h], preferred_element_type=F32)
        dvec = d_ref[h]
        for i in range(0, l, 2):
            lags = jnp.concatenate([xb[i + 1 - tau] for tau in range(i + 2)], axis=1)
            pair = jnp.dot(lags, kd_s[h, 0:(i + 2) * lanes, :], preferred_element_type=F32)
            for k in range(2):
                y = (pair[:, k * lanes:(k + 1) * lanes]
                     + y_state[:, (i + k) * lanes:(i + k + 1) * lanes] + x32[i + k] * dvec)
                y_ref[h, pl.ds(i + k, n_chunks, stride=l), :] = y


def _ssm_scan(u, tables, batch, seq):
    n_half, t, lanes = u.shape
    l, p = SSM_CHUNK, SSM_STATE
    gh = lanes // SSM_GROUP_CH
    tok = pl.BlockSpec((n_half, seq, lanes), lambda b: (0, b, 0))
    return pl.pallas_call(
        functools.partial(_ssm_scan_body, n_chunks=seq // l),
        grid=(batch,),
        in_specs=[tok] + [_resident(w.shape) for w in tables],
        out_specs=tok,
        out_shape=jax.ShapeDtypeStruct(u.shape, F32),
        scratch_shapes=[pltpu.VMEM((n_half, l * lanes, 2 * lanes), BF16),
                        pltpu.VMEM((n_half, l * lanes, 2 * gh * p), BF16),
                        pltpu.VMEM((n_half, 2 * gh * p, l * lanes), BF16)],
        compiler_params=_cparams("arbitrary"),
        name="ssm_chunk_scan",
    )(u, *tables)


def _ssm(u_slabs, batch, seq, prm):
    n_half, t, lanes = u_slabs.shape
    c, l, p = SSM_GROUP_CH, SSM_CHUNK, SSM_STATE
    gh = lanes // c
    kt, wre, wim, vre, vim, pre, pim = _ssm_prep(prm["a_re"], prm["a_im"], prm["log_step"],
                                                 prm["b_re"], prm["b_im"], prm["c_re"], prm["c_im"])
    kc = jnp.transpose(kt.reshape(n_half, gh, l, c, c), (0, 2, 1, 4, 3)).reshape(n_half, l * lanes, c)
    wc = jnp.transpose(jnp.stack([wre, wim]).reshape(2, n_half, gh, l, c, p), (1, 0, 3, 2, 4, 5))
    wc = wc.reshape(n_half, 2, l * lanes, p)
    vc = jnp.transpose(jnp.stack([vre, vim]).reshape(2, n_half, gh, l, c, p), (1, 0, 2, 5, 3, 4))
    vc = vc.reshape(n_half, 2, gh * p, l * c)
    ek = jnp.tile(jnp.eye(c, dtype=BF16), (1, gh))
    ew = jnp.tile(jnp.eye(p, dtype=BF16), (1, gh))
    ev = jnp.broadcast_to(jnp.einsum('ij,cd->icjd', jnp.eye(l, dtype=BF16), jnp.eye(c, dtype=BF16))[:, :, :, None, :],
                          (l, c, l, gh, c)).reshape(l * c, l * lanes)
    lv = SSM_SCAN_LEVELS
    m_re = jnp.transpose(pre.reshape(n_half, gh, lv, p), (0, 2, 1, 3)).reshape(n_half, lv, gh * p)
    m_im = jnp.transpose(pim.reshape(n_half, gh, lv, p), (0, 2, 1, 3)).reshape(n_half, lv, gh * p)
    dvec = prm["d"].reshape(n_half, 1, lanes).astype(F32)
    tables = [kc.astype(BF16), wc.astype(BF16), vc.astype(BF16), ek, ew, ev, m_re, m_im, dvec]
    return _ssm_scan(u_slabs, tables, batch, seq)


def _mix_ffn_body(oa_ref, or_ref, ys_ref, gate_ref, h_ref, wa_ref, wr_ref, ws_ref, wv_ref, wg_ref, wo_ref,
                  gain_ref, wgu_ref, wd_ref, fgain_ref, out_ref, *, hidden, final_norm):
    d = h_ref.shape[1]
    y = jnp.concatenate([ys_ref[i] for i in range(ys_ref.shape[0])], axis=1)
    zg = (0.5 * y * (1.0 + jnp.tanh(math.sqrt(2.0 / math.pi) * (y + 0.044715 * (y * y * y))))).astype(BF16)
    o_ssm = (jnp.dot(zg, wv_ref[...], preferred_element_type=F32)
             * _sigmoid(jnp.dot(zg, wg_ref[...], preferred_element_type=F32)))
    gate = lambda i: gate_ref[:, i * d:(i + 1) * d].astype(F32)
    merged = (gate(0) * _dot(oa_ref[...], wa_ref[...])
              + gate(1) * _dot(or_ref[...], wr_ref[...])
              + gate(2) * _dot(o_ssm, ws_ref[...]))
    x = h_ref[...] + _dot(merged, wo_ref[...])

    u = _rmsnorm(x, gain_ref[...]).astype(BF16)
    acc = jnp.zeros(x.shape, F32)
    for c in range(hidden // FFN_CHUNK):
        lo = c * FFN_CHUNK
        a = jnp.dot(u, wgu_ref[:, lo:lo + FFN_CHUNK], preferred_element_type=F32)
        b = jnp.dot(u, wgu_ref[:, hidden + lo:hidden + lo + FFN_CHUNK], preferred_element_type=F32)
        hm = (a * _sigmoid(a) * b).astype(BF16)
        acc = acc + jnp.dot(hm, wd_ref[lo:lo + FFN_CHUNK, :], preferred_element_type=F32)
    x = x + acc
    if final_norm:
        x = _rmsnorm(x, fgain_ref[...])
    out_ref[...] = x


def _mix_ffn(o_attn, o_rwkv, y_ssm, gates, h, mix_stacks, gain, wgu_stack, wd_stack, final_gain, final_norm, layer):
    t, d = h.shape
    tm = ROW_TILE
    rows = lambda a: pl.BlockSpec((tm, a.shape[1]), lambda i: (i, 0))
    slabs = lambda a: pl.BlockSpec((a.shape[0], tm, a.shape[2]), lambda i: (0, i, 0))
    stacked = lambda w: _resident_layer(w, layer)
    whole = lambda w: _resident(w.shape)
    weights = list(mix_stacks) + [gain.reshape(1, d), wgu_stack, wd_stack, final_gain.reshape(1, d)]
    specs = [stacked(w) for w in mix_stacks] + [whole(weights[-4]), stacked(wgu_stack), stacked(wd_stack),
                                                whole(weights[-1])]
    return pl.pallas_call(
        functools.partial(_mix_ffn_body, hidden=wd_stack.shape[1], final_norm=final_norm),
        grid=(t // tm,),
        in_specs=[rows(o_attn), rows(o_rwkv), slabs(y_ssm), rows(gates), rows(h)] + specs,
        out_specs=rows(h),
        out_shape=jax.ShapeDtypeStruct((t, d), F32),
        compiler_params=_cparams("arbitrary"),
        name="merge_ffn",
    )(o_attn, o_rwkv, y_ssm, gates, h, *weights)


def kernel(x, norm_mix, w_in, rwkv_shift_mix, rwkv_w0, rwkv_w2, rwkv_a0, rwkv_a2, rwkv_g2, rwkv_k_k, rwkv_k_a, rwkv_r_k, rwkv_ln_w, rwkv_ln_b, ssm_a_re, ssm_a_im, ssm_log_step, ssm_b_re, ssm_b_im, ssm_c_re, ssm_c_im, ssm_d, ssm_glu_val, ssm_glu_gate, w_branch_attn, w_branch_rwkv, w_branch_ssm, w_out, norm_ffn, ffn_w_gate_up, ffn_w_down, norm_final):
    batch, seq, d = x.shape
    depth = w_in.shape[0]
    attn_w = w_branch_attn.shape[1]
    rwkv_in = rwkv_shift_mix.shape[1]
    ssm_w = w_branch_ssm.shape[1]
    widths = (3 * attn_w, rwkv_in, ssm_w, N_BRANCH * d)
    assert sum(widths) == w_in.shape[2]
    assert attn_w == len(ATTN_GROUPS) * HEADS_PER_GROUP * HEAD_DIM
    assert seq % ATTN_BLOCK == 0 or seq in (w for w, _ in ATTN_GROUPS)
    assert seq // SSM_CHUNK <= 2 ** SSM_SCAN_LEVELS and (batch * seq) % ROW_TILE == 0

    bf = lambda w: w.astype(BF16)
    w_in_b, wgu_b, wd_b = bf(w_in), bf(ffn_w_gate_up), bf(ffn_w_down)
    mix_b = [bf(w) for w in (w_branch_attn, w_branch_rwkv, w_branch_ssm, ssm_glu_val, ssm_glu_gate, w_out)]
    h = x.reshape(batch * seq, d)
    for l in range(depth):
        qkv, z_rwkv, u_ssm, gates = _in_proj(h, norm_mix[l], w_in_b, l, widths)
        o_attn = _attention(qkv, batch, seq)
        o_rwkv = _rwkv(z_rwkv, batch, seq, dict(
            mix=rwkv_shift_mix[l], w0=rwkv_w0[l], w2=rwkv_w2[l], a0=rwkv_a0[l], a2=rwkv_a2[l],
            g2=rwkv_g2[l], k_k=rwkv_k_k[l], k_a=rwkv_k_a[l], r_k=rwkv_r_k[l],
            ln_w=rwkv_ln_w[l], ln_b=rwkv_ln_b[l]))
        y_ssm = _ssm(u_ssm, batch, seq, dict(
            a_re=ssm_a_re[l], a_im=ssm_a_im[l], log_step=ssm_log_step[l], b_re=ssm_b_re[l],
            b_im=ssm_b_im[l], c_re=ssm_c_re[l], c_im=ssm_c_im[l], d=ssm_d[l]))
        h = _mix_ffn(o_attn, o_rwkv, y_ssm, gates, h, mix_b, norm_ffn[l], wgu_b, wd_b, norm_final,
                     l == depth - 1, l)
    return h.reshape(batch, seq, d)
```

```python
import functools
import math

import jax
import jax.numpy as jnp
from jax import lax
from jax.experimental import pallas as pl
from jax.experimental.pallas import tpu as pltpu

F32 = jnp.float32
BF16 = jnp.bfloat16

HEAD_DIM = 64
ATTN_GROUPS = ((128, 1), (512, 4), (2048, 16))
ATTN_N_BACK = 128
HEADS_PER_GROUP = 2
DECAY_LORA = 64
AAA_LORA = 64
GATE_LORA = 128
SSM_GROUP_CH = 16
SSM_STATE = 64
GN_EPS = 64e-5
NORM_EPS = 1e-6
N_BRANCH = 3

V7X_LANES = 128
V7X_SUBLANES = 8
V7X_MXU_DIM = 256
V7X_VMEM_BYTES = 64 * 1024 * 1024
VMEM_LIMIT = V7X_VMEM_BYTES - 8 * 1024 * 1024

ROW_TILE = 512
COL_CHUNK = 512
FFN_CHUNK = 256
ATTN_BLOCK = 2048
ATTN_UNROLL = 4
RWKV_BLOCK = 128
RWKV_CHUNK = 64
RWKV_SEQS = 8
RWKV_GROUP_LANES = 256
SSM_CHUNK = 16
SSM_SCAN_LEVELS = 8


def _cparams(*sem):
    return pltpu.CompilerParams(dimension_semantics=sem, vmem_limit_bytes=VMEM_LIMIT)


def _resident(shape):
    nd = len(shape)
    return pl.BlockSpec(shape, lambda *_: (0,) * nd, pipeline_mode=pl.Buffered(1))


def _resident_layer(stacked, layer):
    nd = stacked.ndim - 1
    return pl.BlockSpec((None,) + stacked.shape[1:], lambda *_: (layer,) + (0,) * nd,
                        pipeline_mode=pl.Buffered(1))


def _dot(a, b):
    return jnp.dot(a.astype(BF16), b.astype(BF16), preferred_element_type=F32)


def _dot_nt(a, b):
    return lax.dot_general(a.astype(BF16), b.astype(BF16), (((1,), (1,)), ((), ())),
                           preferred_element_type=F32)


def _dot_tn(a, b):
    return lax.dot_general(a.astype(BF16), b.astype(BF16), (((0,), (0,)), ((), ())),
                           preferred_element_type=F32)


def _split(x, terms):
    out, rest = [], x
    for _ in range(terms - 1):
        h = rest.astype(BF16)
        out.append(h)
        rest = rest - h.astype(F32)
    out.append(rest.astype(BF16))
    return out


def _dot_xw(x, w, terms=3):
    m = x.shape[0]
    y = jnp.dot(jnp.concatenate(_split(x, terms), axis=0), w, preferred_element_type=F32)
    return functools.reduce(lambda a, b: a + b, [y[i * m:(i + 1) * m] for i in range(terms)])


def _dot_wx(w, x, terms=3):
    n = x.shape[1]
    y = jnp.dot(w, jnp.concatenate(_split(x, terms), axis=1), preferred_element_type=F32)
    return functools.reduce(lambda a, b: a + b, [y[:, i * n:(i + 1) * n] for i in range(terms)])


def _dot_hi(a, b, dims=(((1,), (0,)), ((), ()))):
    ah = a.astype(BF16)
    al = (a - ah.astype(F32)).astype(BF16)
    bh = b.astype(BF16)
    bl = (b - bh.astype(F32)).astype(BF16)
    dg = functools.partial(lax.dot_general, dimension_numbers=dims, preferred_element_type=F32)
    return dg(ah, bh) + dg(ah, bl) + dg(al, bh)


def _rmsnorm(x, gain):
    ms = jnp.mean(x * x, axis=-1, keepdims=True)
    return x * lax.rsqrt(ms + NORM_EPS) * gain


def _sigmoid(x):
    return 1.0 / (1.0 + jnp.exp(-x))


def _in_proj_body(h_ref, gain_ref, w_ref, qkv_ref, rw_ref, ssm_ref, gate_ref, *, widths):
    attn_in, rwkv_in, ssm_w, gate_w = widths
    u = _rmsnorm(h_ref[...], gain_ref[...]).astype(BF16)

    rw_off = attn_in
    ssm_off = rw_off + rwkv_in
    gate_off = ssm_off + ssm_w
    total = gate_off + gate_w

    def store(col, y):
        if col < rw_off:
            qkv_ref[col // V7X_LANES] = y
        elif col < ssm_off:
            rw_ref[:, col - rw_off:col - rw_off + V7X_LANES] = y
        elif col < gate_off:
            ssm_ref[(col - ssm_off) // V7X_LANES] = y
        else:
            gate_ref[:, col - gate_off:col - gate_off + V7X_LANES] = _sigmoid(y).astype(gate_ref.dtype)

    starts = list(range(0, total, COL_CHUNK))
    plain = [lo for lo in starts if lo + COL_CHUNK <= gate_off]
    gated = [lo for lo in starts if lo + COL_CHUNK > gate_off]
    n = min(len(plain), len(gated))
    order = (gated[:len(gated) - n] + plain[:len(plain) - n]
             + [lo for pair in zip(gated[len(gated) - n:], plain[len(plain) - n:]) for lo in pair])
    for lo in order:
        hi = min(lo + COL_CHUNK, total)
        y = jnp.dot(u, w_ref[:, lo:hi], preferred_element_type=F32)
        for j in range((hi - lo) // V7X_LANES):
            store(lo + j * V7X_LANES, y[:, j * V7X_LANES:(j + 1) * V7X_LANES])


def _in_proj(h, gain, w_stack, layer, widths):
    t, d = h.shape
    attn_in, rwkv_in, ssm_w, gate_w = widths
    n_slab = attn_in // V7X_LANES
    tm = ROW_TILE
    return pl.pallas_call(
        functools.partial(_in_proj_body, widths=widths),
        grid=(t // tm,),
        in_specs=[pl.BlockSpec((tm, d), lambda i: (i, 0)),
                  _resident((1, d)),
                  _resident_layer(w_stack, layer)],
        out_specs=[pl.BlockSpec((n_slab, tm, V7X_LANES), lambda i: (0, i, 0)),
                   pl.BlockSpec((tm, rwkv_in), lambda i: (i, 0)),
                   pl.BlockSpec((ssm_w // V7X_LANES, tm, V7X_LANES), lambda i: (0, i, 0)),
                   pl.BlockSpec((tm, gate_w), lambda i: (i, 0))],
        out_shape=[jax.ShapeDtypeStruct((n_slab, t, V7X_LANES), F32),
                   jax.ShapeDtypeStruct((t, rwkv_in), F32),
                   jax.ShapeDtypeStruct((ssm_w // V7X_LANES, t, V7X_LANES), F32),
                   jax.ShapeDtypeStruct((t, gate_w), BF16)],
        compiler_params=_cparams("arbitrary"),
        name="in_proj",
    )(h, gain.reshape(1, d), w_stack)


def _attn_body(qkv_ref, o_ref, kb0, vb0, kb1, vb1, kb2, vb2, obuf, lbuf, *, tb):
    n = pl.program_id(1)
    nb = ATTN_N_BACK
    kbufs, vbufs = (kb0, kb1, kb2), (vb0, vb1, vb2)
    n_grp = len(ATTN_GROUPS)
    lane = lax.broadcasted_iota(jnp.int32, (nb, V7X_LANES), 1)
    head0 = lane < HEAD_DIM
    qi = lax.broadcasted_iota(jnp.int32, (nb, 2 * nb), 0)
    kj = lax.broadcasted_iota(jnp.int32, (nb, 2 * nb), 1)
    band = (kj >= qi) & (kj <= qi + nb)
    scale = HEAD_DIM ** -0.5

    for g, (window, dil) in enumerate(ATTN_GROUPS):
        span = window
        kb, vb = kbufs[g], vbufs[g]

        @pl.when(n == 0)
        def _():
            kb[0:span, :] = jnp.zeros((span, V7X_LANES), F32)
            vb[0:span, :] = jnp.zeros((span, V7X_LANES), F32)

        @pl.when(n > 0)
        def _():
            kb[0:span, :] = kb[tb:tb + span, :]
            vb[0:span, :] = vb[tb:tb + span, :]

        kb[span:span + tb, :] = qkv_ref[n_grp + g]
        vb[span:span + tb, :] = qkv_ref[2 * n_grp + g]
        n_sub = tb // span

        def units(it, carry, g=g, dil=dil, span=span, kb=kb, vb=vb, n_sub=n_sub):
            bases, masks, qs, k2s, v2s = [], [], [], [], []
            for k in range(ATTN_UNROLL):
                idx = it * ATTN_UNROLL + k
                j = idx // dil
                base = j * span + (idx - j * dil)
                bases.append(base)
                masks.append(band & ((kj >= nb) | ((n * n_sub + j) > 0)))
                qs.append(qkv_ref[g, pl.ds(base, nb, stride=dil), :] * scale)
                k2s.append(kb[pl.ds(base, 2 * nb, stride=dil), :].astype(BF16))
                v2s.append(vb[pl.ds(base, 2 * nb, stride=dil), :].astype(BF16))
            scores = [[_dot_nt(jnp.where(hmask, qs[k], 0.0), k2s[k]) for hmask in (head0, ~head0)]
                      for k in range(ATTN_UNROLL)]
            probs, dens, lses = [], [], []
            for k in range(ATTN_UNROLL):
                for s in scores[k]:
                    s = jnp.where(masks[k], s, -1e30)
                    mx = jnp.max(s, axis=-1, keepdims=True)
                    p = jnp.exp(s - mx)
                    den = jnp.sum(p, axis=-1, keepdims=True)
                    probs.append(p.astype(BF16))
                    dens.append(den)
                    lses.append(mx + jnp.log(den))
            for k in range(ATTN_UNROLL):
                o0 = jnp.dot(probs[2 * k], v2s[k], preferred_element_type=F32) / dens[2 * k]
                o1 = jnp.dot(probs[2 * k + 1], v2s[k], preferred_element_type=F32) / dens[2 * k + 1]
                obuf[g, pl.ds(bases[k], nb, stride=dil), :] = jnp.where(head0, o0, o1)
                lbuf[g, pl.ds(bases[k], nb, stride=dil), :] = jnp.where(head0, lses[2 * k], lses[2 * k + 1])
            return carry

        lax.fori_loop(0, n_sub * dil // ATTN_UNROLL, units, 0)

    rows = 256
    def mix(i, carry):
        sl = pl.ds(pl.multiple_of(i * rows, rows), rows)
        ls = [lbuf[g, sl, :] for g in range(n_grp)]
        m = functools.reduce(jnp.maximum, ls)
        es = [jnp.exp(l - m) for l in ls]
        tot = functools.reduce(lambda a, b: a + b, es)
        for g in range(n_grp):
            o_ref[sl, g * V7X_LANES:(g + 1) * V7X_LANES] = (obuf[g, sl, :] * (es[g] / tot)).astype(o_ref.dtype)
        return carry
    lax.fori_loop(0, tb // rows, mix, 0)


def _attention(qkv, batch, seq):
    n_slab, t, _ = qkv.shape
    tb = min(ATTN_BLOCK, seq)
    nblk = seq // tb
    n_grp = len(ATTN_GROUPS)
    scratch = []
    for window, _ in ATTN_GROUPS:
        scratch += [pltpu.VMEM((window + tb, V7X_LANES), F32)] * 2
    scratch += [pltpu.VMEM((n_grp, tb, V7X_LANES), F32)] * 2
    return pl.pallas_call(
        functools.partial(_attn_body, tb=tb),
        grid=(batch, nblk),
        in_specs=[pl.BlockSpec((n_slab, tb, V7X_LANES), lambda b, n: (0, b * nblk + n, 0))],
        out_specs=pl.BlockSpec((tb, n_grp * V7X_LANES), lambda b, n: (b * nblk + n, 0)),
        out_shape=jax.ShapeDtypeStruct((t, n_grp * V7X_LANES), BF16),
        scratch_shapes=scratch,
        compiler_params=_cparams("arbitrary", "arbitrary"),
        name="dilated_attention",
    )(qkv)


def _rwkv_body(z_ref, mix_ref, w0_ref, w2_ref, a0_ref, a2_ref, g2_ref, kk_ref, ka_ref, rk_ref,
               lnw_ref, lnb_ref, seg_ref, tri_ref, o_ref, carry, zz_s, state, *, tr, width):
    n = pl.program_id(1)
    nseq = z_ref.shape[0]
    c_len = RWKV_CHUNK
    pad = carry.shape[1]

    @pl.when(n == 0)
    def _():
        carry[...] = jnp.zeros(carry.shape, F32)
        state[...] = jnp.zeros(state.shape, F32)

    mix = mix_ref[...]
    for q in range(nseq):
        for i in range(tr // c_len):
            cur = z_ref[q, i * c_len:(i + 1) * c_len, :]
            head = carry[q] if i == 0 else z_ref[q, i * c_len - pad:i * c_len, :]
            prev = pltpu.roll(jnp.concatenate([head, cur], axis=0), 1, 0)[pad:]
            zz_s[q, i * c_len:(i + 1) * c_len, :] = cur + (prev - cur) * mix
    carry[...] = z_ref[:, tr - pad:tr, :]

    seg = seg_ref[...]

    def head_sum(x):
        xb = x.astype(BF16)
        return jnp.concatenate([jnp.dot(xb[:, c:c + V7X_LANES], seg, preferred_element_type=F32)
                                for c in range(0, width, V7X_LANES)], axis=1)
    tri = tri_ref[...]
    lora = DECAY_LORA + AAA_LORA
    n_sq = int(math.log2(c_len)) - 1
    rsl = [slice(q * c_len, (q + 1) * c_len) for q in range(nseq)]
    per = RWKV_GROUP_LANES // V7X_LANES
    cols = [(q, j) for q in range(nseq) for j in range(width // V7X_LANES)]
    whole = [cols[i:i + per] for q in range(nseq)
             for i in range(q * (width // V7X_LANES), (q + 1) * (width // V7X_LANES) - per + 1, per)]
    rest = [c for c in cols if not any(c in g for g in whole)]
    groups = whole + [rest[i:i + per] for i in range(0, len(rest), per)]
    n_grp = range(len(groups))
    gw = [len(g) * V7X_LANES for g in groups]

    def group_masks(w):
        r = lax.broadcasted_iota(jnp.int32, (c_len, w), 0)
        l = lax.broadcasted_iota(jnp.int32, (c_len, w), 1) % HEAD_DIM
        return l < r, l <= r, (l == r).astype(F32)
    masks = {w: group_masks(w) for w in set(gw)}
    low = lax.broadcasted_iota(jnp.int32, (1, V7X_LANES), 1) < HEAD_DIM

    def bd(x, g):
        ncol = gw[g] // V7X_LANES
        zero = jnp.zeros((x.shape[0], V7X_LANES), x.dtype)
        rows = []
        for u in range(gw[g] // HEAD_DIM):
            col = u // 2
            own = jnp.where(low if u % 2 == 0 else ~low, x[:, col * V7X_LANES:(col + 1) * V7X_LANES], 0)
            rows.append(jnp.concatenate([own if c == col else zero for c in range(ncol)], axis=1))
        return jnp.concatenate(rows, axis=0)

    def diag_blocks(full, g):
        cols = []
        for c in range(gw[g] // V7X_LANES):
            lanes = slice(c * V7X_LANES, (c + 1) * V7X_LANES)
            cols.append(jnp.where(low, full[2 * c * HEAD_DIM:(2 * c + 1) * HEAD_DIM, lanes],
                                  full[(2 * c + 1) * HEAD_DIM:(2 * c + 2) * HEAD_DIM, lanes]))
        return jnp.concatenate(cols, axis=1)

    def chunk(c, carry):
        sl = pl.ds(pl.multiple_of(c * c_len, c_len), c_len)
        zz = jnp.concatenate([zz_s[q, sl, :] for q in range(nseq)], axis=0)
        r = zz[:, 0:width]
        k = zz[:, width:2 * width]
        v = zz[:, 2 * width:3 * width]
        t = zz[:, 3 * width:3 * width + lora]
        xg = zz[:, 3 * width + lora:3 * width + lora + GATE_LORA]

        wpre = w0_ref[...] + _dot_hi(jnp.tanh(t), w2_ref[...])
        nw = -wpre
        softplus = jnp.maximum(nw, 0.0) + jnp.log(1.0 + jnp.exp(-jnp.abs(nw)))
        lw = -jnp.exp(-softplus - 0.5)
        a = _sigmoid(a0_ref[...] + _dot(t, a2_ref[...]))
        gate = _dot(_sigmoid(xg), g2_ref[...])

        kk = k * kk_ref[...]
        kk = kk * lax.rsqrt(jnp.maximum(head_sum(kk * kk), 1e-24))
        k = k * (1.0 + (a - 1.0) * ka_ref[...])
        bonus = head_sum(r * k * rk_ref[...]) * v
        avec = -kk
        bvec = kk * a

        lc = jnp.concatenate([_dot_wx(tri, lw[i:i + tri.shape[0]], terms=2)
                              for i in range(0, nseq * c_len, tri.shape[0])], axis=0)
        lc_last = [lc[s.stop - 1:s.stop, :] for s in rsl]
        e_pos = jnp.exp(lc)
        e_neg = jnp.exp(-lc)
        rt = (r * e_pos).astype(BF16)
        at = (avec * jnp.exp(lc - lw)).astype(BF16)
        kt = (k * e_neg).astype(BF16)
        bt = (bvec * e_neg).astype(BF16)
        p_end = [jnp.exp(e) for e in lc_last]
        vb = v.astype(BF16)

        pack = lambda x, g: jnp.concatenate(
            [x[rsl[q], j * V7X_LANES:(j + 1) * V7X_LANES] for q, j in groups[g]], axis=1)
        strict = [masks[w][0] for w in gw]
        incl = [masks[w][1] for w in gw]
        a_p, r_p, b_p, k_p, v_p = ([pack(x, g) for g in n_grp] for x in (at, rt, bt, kt, vb))
        sm = [state[g, :, 0:gw[g]] for g in n_grp]
        smb = [m.astype(BF16) for m in sm]
        ar = [jnp.concatenate([a_p[g], r_p[g]], axis=0) for g in n_grp]
        gbk = [_dot_nt(ar[g], jnp.concatenate([bd(b_p[g], g), bd(k_p[g], g)], axis=0)) for g in n_grp]
        lmat = [jnp.where(strict[g], gbk[g][0:c_len, 0:gw[g]], 0.0) for g in n_grp]
        rb = [jnp.where(incl[g], gbk[g][c_len:, 0:gw[g]], 0.0).astype(BF16) for g in n_grp]
        akrk = [jnp.concatenate([jnp.where(strict[g], gbk[g][0:c_len, gw[g]:], 0.0),
                                 jnp.where(incl[g], gbk[g][c_len:, gw[g]:], 0.0)], axis=0).astype(BF16)
                for g in n_grp]
        lb = [x.astype(BF16) for x in lmat]
        pw = [_dot(lb[g], bd(lb[g], g)) for g in n_grp]
        tinv = [masks[gw[g]][2] + lmat[g] for g in n_grp]
        s0 = [_dot_nt(ar[g], bd(smb[g], g)) for g in n_grp]
        kv = [_dot(akrk[g], bd(v_p[g], g)) for g in n_grp]
        for _ in range(n_sq - 1):
            pb = [x.astype(BF16) for x in pw]
            sq = [_dot(jnp.concatenate([pb[g], tinv[g].astype(BF16)], axis=0), bd(pb[g], g)) for g in n_grp]
            tinv = [tinv[g] + sq[g][c_len:] for g in n_grp]
            pw = [sq[g][0:c_len] for g in n_grp]
        tinv = [tinv[g] + _dot(tinv[g], bd(pw[g].astype(BF16), g)) for g in n_grp]
        ub = [_dot(tinv[g], bd((s0[g][0:c_len] + kv[g][0:c_len]).astype(BF16), g)).astype(BF16) for g in n_grp]
        ys = [s0[g][c_len:] + _dot(rb[g], bd(ub[g], g)) + kv[g][c_len:] for g in n_grp]
        for g in n_grp:
            full = _dot_tn(jnp.concatenate([ub[g], v_p[g]], axis=0), jnp.concatenate([b_p[g], k_p[g]], axis=0))
            upd = diag_blocks(full, g)
            p_end_g = jnp.concatenate([p_end[q][:, j * V7X_LANES:(j + 1) * V7X_LANES] for q, j in groups[g]], axis=1)
            state[g, :, 0:gw[g]] = (sm[g] + upd) * p_end_g
        piece = {qj: ys[g][:, i * V7X_LANES:(i + 1) * V7X_LANES] for g in n_grp for i, qj in enumerate(groups[g])}
        y = jnp.concatenate([jnp.concatenate([piece[(q, j)] for j in range(width // V7X_LANES)], axis=1)
                             for q in range(nseq)], axis=0)

        inv_e = 1.0 / HEAD_DIM
        mu = head_sum(y) * inv_e
        yc = y - mu
        var = head_sum(yc * yc) * inv_e
        yn = yc * lax.rsqrt(var + GN_EPS) * lnw_ref[...] + lnb_ref[...]
        out = ((yn + bonus) * gate).astype(o_ref.dtype)
        for q in range(nseq):
            o_ref[q, sl, :] = out[rsl[q], :]
        return carry

    lax.fori_loop(0, tr // c_len, chunk, 0)


def _rwkv(z, batch, seq, prm):
    t, zin = z.shape
    width = prm["w0"].shape[-1]
    tr = min(RWKV_BLOCK, seq)
    nblk = seq // tr
    nseq = math.gcd(batch, RWKV_SEQS)
    ncol, per = width // V7X_LANES, RWKV_GROUP_LANES // V7X_LANES
    n_groups = nseq * (ncol // per) + -(-(nseq * (ncol % per)) // per)
    row = lambda x: x.reshape(1, -1).astype(F32)
    w2p = jnp.concatenate([prm["w2"], jnp.zeros((AAA_LORA, width), F32)], axis=0)
    a2p = jnp.concatenate([jnp.zeros((DECAY_LORA, width), F32), prm["a2"]], axis=0)
    hid = jnp.arange(V7X_LANES) // HEAD_DIM
    seg = (hid[:, None] == hid[None, :]).astype(BF16)
    pos = jnp.arange(math.gcd(nseq * RWKV_CHUNK, V7X_MXU_DIM))
    tri = ((pos[:, None] >= pos[None, :])
           & (pos[:, None] // RWKV_CHUNK == pos[None, :] // RWKV_CHUNK)).astype(BF16)
    params = [row(prm["mix"]), row(prm["w0"]), w2p, row(prm["a0"]), a2p.astype(BF16), prm["g2"].astype(BF16),
              row(prm["k_k"]), row(prm["k_a"]), row(prm["r_k"]), row(prm["ln_w"]), row(prm["ln_b"]),
              seg, tri]
    out = pl.pallas_call(
        functools.partial(_rwkv_body, tr=tr, width=width),
        grid=(batch // nseq, nblk),
        in_specs=[pl.BlockSpec((nseq, tr, zin), lambda b, n: (b, n, 0))]
                 + [_resident(p.shape) for p in params],
        out_specs=pl.BlockSpec((nseq, tr, width), lambda b, n: (b, n, 0)),
        out_shape=jax.ShapeDtypeStruct((batch, seq, width), BF16),
        scratch_shapes=[pltpu.VMEM((nseq, V7X_SUBLANES, zin), F32), pltpu.VMEM((nseq, tr, zin), F32),
                        pltpu.VMEM((n_groups, HEAD_DIM, RWKV_GROUP_LANES), F32)],
        compiler_params=_cparams("arbitrary", "arbitrary"),
        name="rwkv7_time_mix",
    )(z.reshape(batch, seq, zin), *params)
    return out.reshape(t, width)


def _ssm_prep_body(are_ref, aim_ref, lstep_ref, bre_ref, bim_ref, cre_ref, cim_ref,
                   kt_ref, wre_ref, wim_ref, vre_ref, vim_ref, pre_ref, pim_ref):
    lam_re = are_ref[0]
    lam_im = aim_ref[0]
    step = jnp.exp(lstep_ref[0])
    mag = jnp.exp(lam_re * step)
    ang = lam_im * step
    abar_re, abar_im = mag * jnp.cos(ang), mag * jnp.sin(ang)
    inv = 1.0 / (lam_re * lam_re + lam_im * lam_im)
    f_re = ((abar_re - 1.0) * lam_re + abar_im * lam_im) * inv
    f_im = (abar_im * lam_re - (abar_re - 1.0) * lam_im) * inv
    b_re, b_im = bre_ref[0], bim_ref[0]
    bbar_re = f_re * b_re - f_im * b_im
    bbar_im = f_re * b_im + f_im * b_re
    c_re, c_im = cre_ref[0], cim_ref[0]

    def power(tau):
        m = jnp.exp(lam_re * step * tau)
        return m * jnp.cos(ang * tau), m * jnp.sin(ang * tau)

    nt = (((1,), (1,)), ((), ()))
    for tau in range(SSM_CHUNK):
        p_re, p_im = power(float(tau))
        cp_re = c_re * p_re - c_im * p_im
        cp_im = c_re * p_im + c_im * p_re
        kt_ref[0, tau] = _dot_hi(cp_re, bbar_re, nt) - _dot_hi(cp_im, bbar_im, nt)
        q_re, q_im = power(float(SSM_CHUNK - 1 - tau))
        wre_ref[0, tau] = q_re * bbar_re - q_im * bbar_im
        wim_ref[0, tau] = q_re * bbar_im + q_im * bbar_re
        s_re, s_im = power(float(tau + 1))
        vre_ref[0, tau] = c_re * s_re - c_im * s_im
        vim_ref[0, tau] = -(c_re * s_im + c_im * s_re)
    for lvl in range(SSM_SCAN_LEVELS):
        p_re, p_im = power(float(SSM_CHUNK * 2 ** lvl))
        pre_ref[0, lvl:lvl + 1, :] = p_re
        pim_ref[0, lvl:lvl + 1, :] = p_im


def _ssm_prep(a_re, a_im, log_step, b_re, b_im, c_re, c_im):
    g, p = a_re.shape
    c = SSM_GROUP_CH
    l = SSM_CHUNK
    grp3 = lambda shape: pl.BlockSpec((1,) + shape, lambda i: (i, 0, 0))
    grp4 = lambda shape: pl.BlockSpec((1,) + shape, lambda i: (i, 0, 0, 0))
    sds = jax.ShapeDtypeStruct
    return pl.pallas_call(
        _ssm_prep_body,
        grid=(g,),
        in_specs=[grp3((1, p)), grp3((1, p)), grp3((1, 1)), grp3((c, p)), grp3((c, p)),
                  grp3((c, p)), grp3((c, p))],
        out_specs=[grp4((l, c, c)), grp4((l, c, p)), grp4((l, c, p)), grp4((l, c, p)), grp4((l, c, p)),
                   grp3((SSM_SCAN_LEVELS, p)), grp3((SSM_SCAN_LEVELS, p))],
        out_shape=[sds((g, l, c, c), F32), sds((g, l, c, p), F32), sds((g, l, c, p), F32),
                   sds((g, l, c, p), F32), sds((g, l, c, p), F32),
                   sds((g, SSM_SCAN_LEVELS, p), F32), sds((g, SSM_SCAN_LEVELS, p), F32)],
        compiler_params=_cparams("arbitrary"),
        name="ssm_discretise",
    )(a_re.reshape(g, 1, p), a_im.reshape(g, 1, p), log_step.reshape(g, 1, 1),
      jnp.swapaxes(b_re, 1, 2), jnp.swapaxes(b_im, 1, 2), c_re, c_im)


def _ssm_scan_body(u_ref, kc_ref, wc_ref, vc_ref, ek_ref, ew_ref, ev_ref, mre_ref, mim_ref, d_ref, y_ref,
                   kd_s, wd_s, vd_s, *, n_chunks):
    l = SSM_CHUNK
    n_half, _, lanes = u_ref.shape
    gh = lanes // SSM_GROUP_CH

    @pl.when(pl.program_id(0) == 0)
    def _():
        def expand(src, e_ref, row_group, lane_group):
            out = jnp.dot(src, e_ref[...], preferred_element_type=F32)
            rg = row_group(lax.broadcasted_iota(jnp.int32, out.shape, 0))
            lg = lane_group(lax.broadcasted_iota(jnp.int32, out.shape, 1))
            return jnp.where(rg == lg, out, 0.0).astype(BF16)

        rows = 512
        for h in range(n_half):
            kd_s[h, 0:lanes, 0:lanes] = jnp.zeros((lanes, lanes), BF16)
            for r0 in range(0, l * lanes, rows):
                blk = expand(kc_ref[h, r0:r0 + rows, :], ek_ref,
                             lambda r: (r // SSM_GROUP_CH) % gh, lambda c: c // SSM_GROUP_CH)
                kd_s[h, r0:r0 + rows, lanes:2 * lanes] = blk
                hi = min(r0 + lanes + rows, l * lanes)
                kd_s[h, r0 + lanes:hi, 0:lanes] = blk[0:hi - r0 - lanes]
            for ri in range(2):
                half = wd_s.shape[2] // 2
                for r0 in range(0, l * lanes, rows):
                    wd_s[h, r0:r0 + rows, ri * half:(ri + 1) * half] = expand(
                        wc_ref[h, ri, r0:r0 + rows, :], ew_ref,
                        lambda r: (r // SSM_GROUP_CH) % gh, lambda c: c // SSM_STATE)
                hv = vd_s.shape[1] // 2
                vd_s[h, ri * hv:(ri + 1) * hv, :] = expand(
                    vc_ref[h, ri], ev_ref, lambda r: r // SSM_STATE, lambda c: (c // SSM_GROUP_CH) % gh)

    row = lax.broadcasted_iota(jnp.int32, (n_chunks, 1), 0)
    for h in range(n_half):
        x32 = [u_ref[h, pl.ds(j, n_chunks, stride=l), :] for j in range(l)]
        xb = [x.astype(BF16) for x in x32]
        z = jnp.dot(jnp.concatenate(xb, axis=1), wd_s[h], preferred_element_type=F32)
        half = z.shape[1] // 2
        z_re, z_im = z[:, :half], z[:, half:]
        for lvl in range(SSM_SCAN_LEVELS):
            s = 2 ** lvl
            if s >= n_chunks:
                break
            s_re = jnp.where(row >= s, pltpu.roll(z_re, s, 0), 0.0)
            s_im = jnp.where(row >= s, pltpu.roll(z_im, s, 0), 0.0)
            a_re = mre_ref[h, lvl:lvl + 1, :]
            a_im = mim_ref[h, lvl:lvl + 1, :]
            z_re, z_im = z_re + s_re * a_re - s_im * a_im, z_im + s_re * a_im + s_im * a_re
        zin = jnp.concatenate([jnp.where(row >= 1, pltpu.roll(z_re, 1, 0), 0.0),
                               jnp.where(row >= 1, pltpu.roll(z_im, 1, 0), 0.0)], axis=1)
        y_state = jnp.dot(zin.astype(BF16), vd_s[h], preferred_element_type=F32)
        dvec = d_ref[h]
        for i in range(0, l, 2):
            lags = jnp.concatenate([xb[i + 1 - tau] for tau in range(i + 2)], axis=1)
            pair = jnp.dot(lags, kd_s[h, 0:(i + 2) * lanes, :], preferred_element_type=F32)
            for k in range(2):
                y = (pair[:, k * lanes:(k + 1) * lanes]
                     + y_state[:, (i + k) * lanes:(i + k + 1) * lanes] + x32[i + k] * dvec)
                y_ref[h, pl.ds(i + k, n_chunks, stride=l), :] = y


def _ssm_scan(u, tables, batch, seq):
    n_half, t, lanes = u.shape
    l, p = SSM_CHUNK, SSM_STATE
    gh = lanes // SSM_GROUP_CH
    tok = pl.BlockSpec((n_half, seq, lanes), lambda b: (0, b, 0))
    return pl.pallas_call(
        functools.partial(_ssm_scan_body, n_chunks=seq // l),
        grid=(batch,),
        in_specs=[tok] + [_resident(w.shape) for w in tables],
        out_specs=tok,
        out_shape=jax.ShapeDtypeStruct(u.shape, F32),
        scratch_shapes=[pltpu.VMEM((n_half, l * lanes, 2 * lanes), BF16),
                        pltpu.VMEM((n_half, l * lanes, 2 * gh * p), BF16),
                        pltpu.VMEM((n_half, 2 * gh * p, l * lanes), BF16)],
        compiler_params=_cparams("arbitrary"),
        name="ssm_chunk_scan",
    )(u, *tables)


def _ssm(u_slabs, batch, seq, prm):
    n_half, t, lanes = u_slabs.shape
    c, l, p = SSM_GROUP_CH, SSM_CHUNK, SSM_STATE
    gh = lanes // c
    kt, wre, wim, vre, vim, pre, pim = _ssm_prep(prm["a_re"], prm["a_im"], prm["log_step"],
                                                 prm["b_re"], prm["b_im"], prm["c_re"], prm["c_im"])
    kc = jnp.transpose(kt.reshape(n_half, gh, l, c, c), (0, 2, 1, 4, 3)).reshape(n_half, l * lanes, c)
    wc = jnp.transpose(jnp.stack([wre, wim]).reshape(2, n_half, gh, l, c, p), (1, 0, 3, 2, 4, 5))
    wc = wc.reshape(n_half, 2, l * lanes, p)
    vc = jnp.transpose(jnp.stack([vre, vim]).reshape(2, n_half, gh, l, c, p), (1, 0, 2, 5, 3, 4))
    vc = vc.reshape(n_half, 2, gh * p, l * c)
    ek = jnp.tile(jnp.eye(c, dtype=BF16), (1, gh))
    ew = jnp.tile(jnp.eye(p, dtype=BF16), (1, gh))
    ev = jnp.broadcast_to(jnp.einsum('ij,cd->icjd', jnp.eye(l, dtype=BF16), jnp.eye(c, dtype=BF16))[:, :, :, None, :],
                          (l, c, l, gh, c)).reshape(l * c, l * lanes)
    lv = SSM_SCAN_LEVELS
    m_re = jnp.transpose(pre.reshape(n_half, gh, lv, p), (0, 2, 1, 3)).reshape(n_half, lv, gh * p)
    m_im = jnp.transpose(pim.reshape(n_half, gh, lv, p), (0, 2, 1, 3)).reshape(n_half, lv, gh * p)
    dvec = prm["d"].reshape(n_half, 1, lanes).astype(F32)
    tables = [kc.astype(BF16), wc.astype(BF16), vc.astype(BF16), ek, ew, ev, m_re, m_im, dvec]
    return _ssm_scan(u_slabs, tables, batch, seq)


def _mix_ffn_body(oa_ref, or_ref, ys_ref, gate_ref, h_ref, wa_ref, wr_ref, ws_ref, wv_ref, wg_ref, wo_ref,
                  gain_ref, wgu_ref, wd_ref, fgain_ref, out_ref, *, hidden, final_norm):
    d = h_ref.shape[1]
    y = jnp.concatenate([ys_ref[i] for i in range(ys_ref.shape[0])], axis=1)
    zg = (0.5 * y * (1.0 + jnp.tanh(math.sqrt(2.0 / math.pi) * (y + 0.044715 * (y * y * y))))).astype(BF16)
    o_ssm = (jnp.dot(zg, wv_ref[...], preferred_element_type=F32)
             * _sigmoid(jnp.dot(zg, wg_ref[...], preferred_element_type=F32)))
    gate = lambda i: gate_ref[:, i * d:(i + 1) * d].astype(F32)
    merged = (gate(0) * _dot(oa_ref[...], wa_ref[...])
              + gate(1) * _dot(or_ref[...], wr_ref[...])
              + gate(2) * _dot(o_ssm, ws_ref[...]))
    x = h_ref[...] + _dot(merged, wo_ref[...])

    u = _rmsnorm(x, gain_ref[...]).astype(BF16)
    acc = jnp.zeros(x.shape, F32)
    for c in range(hidden // FFN_CHUNK):
        lo = c * FFN_CHUNK
        a = jnp.dot(u, wgu_ref[:, lo:lo + FFN_CHUNK], preferred_element_type=F32)
        b = jnp.dot(u, wgu_ref[:, hidden + lo:hidden + lo + FFN_CHUNK], preferred_element_type=F32)
        hm = (a * _sigmoid(a) * b).astype(BF16)
        acc = acc + jnp.dot(hm, wd_ref[lo:lo + FFN_CHUNK, :], preferred_element_type=F32)
    x = x + acc
    if final_norm:
        x = _rmsnorm(x, fgain_ref[...])
    out_ref[...] = x


def _mix_ffn(o_attn, o_rwkv, y_ssm, gates, h, mix_stacks, gain, wgu_stack, wd_stack, final_gain, final_norm, layer):
    t, d = h.shape
    tm = ROW_TILE
    rows = lambda a: pl.BlockSpec((tm, a.shape[1]), lambda i: (i, 0))
    slabs = lambda a: pl.BlockSpec((a.shape[0], tm, a.shape[2]), lambda i: (0, i, 0))
    stacked = lambda w: _resident_layer(w, layer)
    whole = lambda w: _resident(w.shape)
    weights = list(mix_stacks) + [gain.reshape(1, d), wgu_stack, wd_stack, final_gain.reshape(1, d)]
    specs = [stacked(w) for w in mix_stacks] + [whole(weights[-4]), stacked(wgu_stack), stacked(wd_stack),
                                                whole(weights[-1])]
    return pl.pallas_call(
        functools.partial(_mix_ffn_body, hidden=wd_stack.shape[1], final_norm=final_norm),
        grid=(t // tm,),
        in_specs=[rows(o_attn), rows(o_rwkv), slabs(y_ssm), rows(gates), rows(h)] + specs,
        out_specs=rows(h),
        out_shape=jax.ShapeDtypeStruct((t, d), F32),
        compiler_params=_cparams("arbitrary"),
        name="merge_ffn",
    )(o_attn, o_rwkv, y_ssm, gates, h, *weights)


def kernel(x, norm_mix, w_in, rwkv_shift_mix, rwkv_w0, rwkv_w2, rwkv_a0, rwkv_a2, rwkv_g2, rwkv_k_k, rwkv_k_a, rwkv_r_k, rwkv_ln_w, rwkv_ln_b, ssm_a_re, ssm_a_im, ssm_log_step, ssm_b_re, ssm_b_im, ssm_c_re, ssm_c_im, ssm_d, ssm_glu_val, ssm_glu_gate, w_branch_attn, w_branch_rwkv, w_branch_ssm, w_out, norm_ffn, ffn_w_gate_up, ffn_w_down, norm_final):
    batch, seq, d = x.shape
    depth = w_in.shape[0]
    attn_w = w_branch_attn.shape[1]
    rwkv_in = rwkv_shift_mix.shape[1]
    ssm_w = w_branch_ssm.shape[1]
    widths = (3 * attn_w, rwkv_in, ssm_w, N_BRANCH * d)
    assert sum(widths) == w_in.shape[2]
    assert attn_w == len(ATTN_GROUPS) * HEADS_PER_GROUP * HEAD_DIM
    assert seq % ATTN_BLOCK == 0 or seq in (w for w, _ in ATTN_GROUPS)
    assert seq // SSM_CHUNK <= 2 ** SSM_SCAN_LEVELS and (batch * seq) % ROW_TILE == 0

    bf = lambda w: w.astype(BF16)
    w_in_b, wgu_b, wd_b = bf(w_in), bf(ffn_w_gate_up), bf(ffn_w_down)
    mix_b = [bf(w) for w in (w_branch_attn, w_branch_rwkv, w_branch_ssm, ssm_glu_val, ssm_glu_gate, w_out)]
    h = x.reshape(batch * seq, d)
    for l in range(depth):
        qkv, z_rwkv, u_ssm, gates = _in_proj(h, norm_mix[l], w_in_b, l, widths)
        o_attn = _attention(qkv, batch, seq)
        o_rwkv = _rwkv(z_rwkv, batch, seq, dict(
            mix=rwkv_shift_mix[l], w0=rwkv_w0[l], w2=rwkv_w2[l], a0=rwkv_a0[l], a2=rwkv_a2[l],
            g2=rwkv_g2[l], k_k=rwkv_k_k[l], k_a=rwkv_k_a[l], r_k=rwkv_r_k[l],
            ln_w=rwkv_ln_w[l], ln_b=rwkv_ln_b[l]))
        y_ssm = _ssm(u_ssm, batch, seq, dict(
            a_re=ssm_a_re[l], a_im=ssm_a_im[l], log_step=ssm_log_step[l], b_re=ssm_b_re[l],
            b_im=ssm_b_im[l], c_re=ssm_c_re[l], c_im=ssm_c_im[l], d=ssm_d[l]))
        h = _mix_ffn(o_attn, o_rwkv, y_ssm, gates, h, mix_b, norm_ffn[l], wgu_b, wd_b, norm_final,
                     l == depth - 1, l)
    return h.reshape(batch, seq, d)
```

```python
import functools
import math

import jax
import jax.numpy as jnp
from jax import lax
from jax.experimental import pallas as pl
from jax.experimental.pallas import tpu as pltpu

F32 = jnp.float32
BF16 = jnp.bfloat16

HEAD_DIM = 64
ATTN_GROUPS = ((128, 1), (512, 4), (2048, 16))
ATTN_N_BACK = 128
HEADS_PER_GROUP = 2
DECAY_LORA = 64
AAA_LORA = 64
GATE_LORA = 128
SSM_GROUP_CH = 16
SSM_STATE = 64
GN_EPS = 64e-5
NORM_EPS = 1e-6
N_BRANCH = 3

V7X_LANES = 128
V7X_SUBLANES = 8
V7X_MXU_DIM = 256
V7X_VMEM_BYTES = 64 * 1024 * 1024
VMEM_LIMIT = V7X_VMEM_BYTES - 8 * 1024 * 1024

ROW_TILE = 512
COL_CHUNK = 256
FFN_CHUNK = 256
ATTN_BLOCK = 2048
ATTN_UNROLL = 4
RWKV_BLOCK = 128
RWKV_CHUNK = 64
RWKV_SEQS = 8
RWKV_GROUP_LANES = 256
SSM_CHUNK = 16
SSM_SCAN_LEVELS = 8


def _cparams(*sem):
    return pltpu.CompilerParams(dimension_semantics=sem, vmem_limit_bytes=VMEM_LIMIT)


def _resident(shape):
    nd = len(shape)
    return pl.BlockSpec(shape, lambda *_: (0,) * nd, pipeline_mode=pl.Buffered(1))


def _resident_layer(stacked, layer):
    nd = stacked.ndim - 1
    return pl.BlockSpec((None,) + stacked.shape[1:], lambda *_: (layer,) + (0,) * nd,
                        pipeline_mode=pl.Buffered(1))


def _dot(a, b):
    return jnp.dot(a.astype(BF16), b.astype(BF16), preferred_element_type=F32)


def _dot_nt(a, b):
    return lax.dot_general(a.astype(BF16), b.astype(BF16), (((1,), (1,)), ((), ())),
                           preferred_element_type=F32)


def _dot_tn(a, b):
    return lax.dot_general(a.astype(BF16), b.astype(BF16), (((0,), (0,)), ((), ())),
                           preferred_element_type=F32)


def _split(x, terms):
    out, rest = [], x
    for _ in range(terms - 1):
        h = rest.astype(BF16)
        out.append(h)
        rest = rest - h.astype(F32)
    out.append(rest.astype(BF16))
    return out


def _dot_xw(x, w, terms=3):
    m = x.shape[0]
    y = jnp.dot(jnp.concatenate(_split(x, terms), axis=0), w, preferred_element_type=F32)
    return functools.reduce(lambda a, b: a + b, [y[i * m:(i + 1) * m] for i in range(terms)])


def _dot_wx(w, x, terms=3):
    n = x.shape[1]
    y = jnp.dot(w, jnp.concatenate(_split(x, terms), axis=1), preferred_element_type=F32)
    return functools.reduce(lambda a, b: a + b, [y[:, i * n:(i + 1) * n] for i in range(terms)])


def _dot_hi(a, b, dims=(((1,), (0,)), ((), ()))):
    ah = a.astype(BF16)
    al = (a - ah.astype(F32)).astype(BF16)
    bh = b.astype(BF16)
    bl = (b - bh.astype(F32)).astype(BF16)
    dg = functools.partial(lax.dot_general, dimension_numbers=dims, preferred_element_type=F32)
    return dg(ah, bh) + dg(ah, bl) + dg(al, bh)


def _rmsnorm(x, gain):
    ms = jnp.mean(x * x, axis=-1, keepdims=True)
    return x * lax.rsqrt(ms + NORM_EPS) * gain


def _sigmoid(x):
    return 1.0 / (1.0 + jnp.exp(-x))


def _in_proj_body(h_ref, gain_ref, w_ref, qkv_ref, rw_ref, ssm_ref, gate_ref, *, widths):
    attn_in, rwkv_in, ssm_w, gate_w = widths
    u = _rmsnorm(h_ref[...], gain_ref[...]).astype(BF16)

    rw_off = attn_in
    ssm_off = rw_off + rwkv_in
    gate_off = ssm_off + ssm_w
    total = gate_off + gate_w

    def store(col, y):
        if col < rw_off:
            qkv_ref[col // V7X_LANES] = y
        elif col < ssm_off:
            rw_ref[:, col - rw_off:col - rw_off + V7X_LANES] = y
        elif col < gate_off:
            ssm_ref[(col - ssm_off) // V7X_LANES] = y
        else:
            gate_ref[:, col - gate_off:col - gate_off + V7X_LANES] = _sigmoid(y).astype(gate_ref.dtype)

    starts = list(range(0, total, COL_CHUNK))
    plain = [lo for lo in starts if lo + COL_CHUNK <= gate_off]
    gated = [lo for lo in starts if lo + COL_CHUNK > gate_off]
    n = min(len(plain), len(gated))
    order = (gated[:len(gated) - n] + plain[:len(plain) - n]
             + [lo for pair in zip(gated[len(gated) - n:], plain[len(plain) - n:]) for lo in pair])
    for lo in order:
        hi = min(lo + COL_CHUNK, total)
        y = jnp.dot(u, w_ref[:, lo:hi], preferred_element_type=F32)
        for j in range((hi - lo) // V7X_LANES):
            store(lo + j * V7X_LANES, y[:, j * V7X_LANES:(j + 1) * V7X_LANES])


def _in_proj(h, gain, w_stack, layer, widths):
    t, d = h.shape
    attn_in, rwkv_in, ssm_w, gate_w = widths
    n_slab = attn_in // V7X_LANES
    tm = ROW_TILE
    return pl.pallas_call(
        functools.partial(_in_proj_body, widths=widths),
        grid=(t // tm,),
        in_specs=[pl.BlockSpec((tm, d), lambda i: (i, 0)),
                  _resident((1, d)),
                  _resident_layer(w_stack, layer)],
        out_specs=[pl.BlockSpec((n_slab, tm, V7X_LANES), lambda i: (0, i, 0)),
                   pl.BlockSpec((tm, rwkv_in), lambda i: (i, 0)),
                   pl.BlockSpec((ssm_w // V7X_LANES, tm, V7X_LANES), lambda i: (0, i, 0)),
                   pl.BlockSpec((tm, gate_w), lambda i: (i, 0))],
        out_shape=[jax.ShapeDtypeStruct((n_slab, t, V7X_LANES), F32),
                   jax.ShapeDtypeStruct((t, rwkv_in), F32),
                   jax.ShapeDtypeStruct((ssm_w // V7X_LANES, t, V7X_LANES), F32),
                   jax.ShapeDtypeStruct((t, gate_w), BF16)],
        compiler_params=_cparams("arbitrary"),
        name="in_proj",
    )(h, gain.reshape(1, d), w_stack)


def _attn_body(qkv_ref, o_ref, kb0, vb0, kb1, vb1, kb2, vb2, obuf, lbuf, *, tb):
    n = pl.program_id(1)
    nb = ATTN_N_BACK
    kbufs, vbufs = (kb0, kb1, kb2), (vb0, vb1, vb2)
    n_grp = len(ATTN_GROUPS)
    lane = lax.broadcasted_iota(jnp.int32, (nb, V7X_LANES), 1)
    head0 = lane < HEAD_DIM
    qi = lax.broadcasted_iota(jnp.int32, (nb, 2 * nb), 0)
    kj = lax.broadcasted_iota(jnp.int32, (nb, 2 * nb), 1)
    band = (kj >= qi) & (kj <= qi + nb)
    scale = HEAD_DIM ** -0.5

    for g, (window, dil) in enumerate(ATTN_GROUPS):
        span = window
        kb, vb = kbufs[g], vbufs[g]

        @pl.when(n == 0)
        def _():
            kb[0:span, :] = jnp.zeros((span, V7X_LANES), F32)
            vb[0:span, :] = jnp.zeros((span, V7X_LANES), F32)

        @pl.when(n > 0)
        def _():
            kb[0:span, :] = kb[tb:tb + span, :]
            vb[0:span, :] = vb[tb:tb + span, :]

        kb[span:span + tb, :] = qkv_ref[n_grp + g]
        vb[span:span + tb, :] = qkv_ref[2 * n_grp + g]
        n_sub = tb // span

        def units(it, carry, g=g, dil=dil, span=span, kb=kb, vb=vb, n_sub=n_sub):
            bases, masks, qs, k2s, v2s = [], [], [], [], []
            for k in range(ATTN_UNROLL):
                idx = it * ATTN_UNROLL + k
                j = idx // dil
                base = j * span + (idx - j * dil)
                bases.append(base)
                masks.append(band & ((kj >= nb) | ((n * n_sub + j) > 0)))
                qs.append(qkv_ref[g, pl.ds(base, nb, stride=dil), :] * scale)
                k2s.append(kb[pl.ds(base, 2 * nb, stride=dil), :].astype(BF16))
                v2s.append(vb[pl.ds(base, 2 * nb, stride=dil), :].astype(BF16))
            scores = [[_dot_nt(jnp.where(hmask, qs[k], 0.0), k2s[k]) for hmask in (head0, ~head0)]
                      for k in range(ATTN_UNROLL)]
            probs, dens, lses = [], [], []
            for k in range(ATTN_UNROLL):
                for s in scores[k]:
                    s = jnp.where(masks[k], s, -1e30)
                    mx = jnp.max(s, axis=-1, keepdims=True)
                    p = jnp.exp(s - mx)
                    den = jnp.sum(p, axis=-1, keepdims=True)
                    probs.append(p.astype(BF16))
                    dens.append(den)
                    lses.append(mx + jnp.log(den))
            for k in range(ATTN_UNROLL):
                o0 = jnp.dot(probs[2 * k], v2s[k], preferred_element_type=F32) / dens[2 * k]
                o1 = jnp.dot(probs[2 * k + 1], v2s[k], preferred_element_type=F32) / dens[2 * k + 1]
                obuf[g, pl.ds(bases[k], nb, stride=dil), :] = jnp.where(head0, o0, o1)
                lbuf[g, pl.ds(bases[k], nb, stride=dil), :] = jnp.where(head0, lses[2 * k], lses[2 * k + 1])
            return carry

        lax.fori_loop(0, n_sub * dil // ATTN_UNROLL, units, 0)

    rows = 256
    def mix(i, carry):
        sl = pl.ds(pl.multiple_of(i * rows, rows), rows)
        ls = [lbuf[g, sl, :] for g in range(n_grp)]
        m = functools.reduce(jnp.maximum, ls)
        es = [jnp.exp(l - m) for l in ls]
        tot = functools.reduce(lambda a, b: a + b, es)
        for g in range(n_grp):
            o_ref[sl, g * V7X_LANES:(g + 1) * V7X_LANES] = (obuf[g, sl, :] * (es[g] / tot)).astype(o_ref.dtype)
        return carry
    lax.fori_loop(0, tb // rows, mix, 0)


def _attention(qkv, batch, seq):
    n_slab, t, _ = qkv.shape
    tb = min(ATTN_BLOCK, seq)
    nblk = seq // tb
    n_grp = len(ATTN_GROUPS)
    scratch = []
    for window, _ in ATTN_GROUPS:
        scratch += [pltpu.VMEM((window + tb, V7X_LANES), F32)] * 2
    scratch += [pltpu.VMEM((n_grp, tb, V7X_LANES), F32)] * 2
    return pl.pallas_call(
        functools.partial(_attn_body, tb=tb),
        grid=(batch, nblk),
        in_specs=[pl.BlockSpec((n_slab, tb, V7X_LANES), lambda b, n: (0, b * nblk + n, 0))],
        out_specs=pl.BlockSpec((tb, n_grp * V7X_LANES), lambda b, n: (b * nblk + n, 0)),
        out_shape=jax.ShapeDtypeStruct((t, n_grp * V7X_LANES), BF16),
        scratch_shapes=scratch,
        compiler_params=_cparams("arbitrary", "arbitrary"),
        name="dilated_attention",
    )(qkv)


def _rwkv_body(z_ref, mix_ref, w0_ref, w2_ref, a0_ref, a2_ref, g2_ref, kk_ref, ka_ref, rk_ref,
               lnw_ref, lnb_ref, seg_ref, tri_ref, o_ref, carry, zz_s, state, *, tr, width):
    n = pl.program_id(1)
    nseq = z_ref.shape[0]
    c_len = RWKV_CHUNK
    pad = carry.shape[1]

    @pl.when(n == 0)
    def _():
        carry[...] = jnp.zeros(carry.shape, F32)
        state[...] = jnp.zeros(state.shape, F32)

    mix = mix_ref[...]
    for q in range(nseq):
        for i in range(tr // c_len):
            cur = z_ref[q, i * c_len:(i + 1) * c_len, :]
            head = carry[q] if i == 0 else z_ref[q, i * c_len - pad:i * c_len, :]
            prev = pltpu.roll(jnp.concatenate([head, cur], axis=0), 1, 0)[pad:]
            zz_s[q, i * c_len:(i + 1) * c_len, :] = cur + (prev - cur) * mix
    carry[...] = z_ref[:, tr - pad:tr, :]

    seg = seg_ref[...]

    def head_sum(x):
        xb = x.astype(BF16)
        return jnp.concatenate([jnp.dot(xb[:, c:c + V7X_LANES], seg, preferred_element_type=F32)
                                for c in range(0, width, V7X_LANES)], axis=1)
    tri = tri_ref[...]
    lora = DECAY_LORA + AAA_LORA
    n_sq = int(math.log2(c_len)) - 1
    rsl = [slice(q * c_len, (q + 1) * c_len) for q in range(nseq)]
    per = RWKV_GROUP_LANES // V7X_LANES
    cols = [(q, j) for q in range(nseq) for j in range(width // V7X_LANES)]
    whole = [cols[i:i + per] for q in range(nseq)
             for i in range(q * (width // V7X_LANES), (q + 1) * (width // V7X_LANES) - per + 1, per)]
    rest = [c for c in cols if not any(c in g for g in whole)]
    groups = whole + [rest[i:i + per] for i in range(0, len(rest), per)]
    n_grp = range(len(groups))
    gw = [len(g) * V7X_LANES for g in groups]

    def group_masks(w):
        r = lax.broadcasted_iota(jnp.int32, (c_len, w), 0)
        l = lax.broadcasted_iota(jnp.int32, (c_len, w), 1) % HEAD_DIM
        return l < r, l <= r, (l == r).astype(F32)
    masks = {w: group_masks(w) for w in set(gw)}
    low = lax.broadcasted_iota(jnp.int32, (1, V7X_LANES), 1) < HEAD_DIM

    def bd(x, g):
        ncol = gw[g] // V7X_LANES
        zero = jnp.zeros((x.shape[0], V7X_LANES), x.dtype)
        rows = []
        for u in range(gw[g] // HEAD_DIM):
            col = u // 2
            own = jnp.where(low if u % 2 == 0 else ~low, x[:, col * V7X_LANES:(col + 1) * V7X_LANES], 0)
            rows.append(jnp.concatenate([own if c == col else zero for c in range(ncol)], axis=1))
        return jnp.concatenate(rows, axis=0)

    def diag_blocks(full, g):
        cols = []
        for c in range(gw[g] // V7X_LANES):
            lanes = slice(c * V7X_LANES, (c + 1) * V7X_LANES)
            cols.append(jnp.where(low, full[2 * c * HEAD_DIM:(2 * c + 1) * HEAD_DIM, lanes],
                                  full[(2 * c + 1) * HEAD_DIM:(2 * c + 2) * HEAD_DIM, lanes]))
        return jnp.concatenate(cols, axis=1)

    def chunk(c, carry):
        sl = pl.ds(pl.multiple_of(c * c_len, c_len), c_len)
        zz = jnp.concatenate([zz_s[q, sl, :] for q in range(nseq)], axis=0)
        r = zz[:, 0:width]
        k = zz[:, width:2 * width]
        v = zz[:, 2 * width:3 * width]
        t = zz[:, 3 * width:3 * width + lora]
        xg = zz[:, 3 * width + lora:3 * width + lora + GATE_LORA]

        wpre = w0_ref[...] + _dot_hi(jnp.tanh(t), w2_ref[...])
        nw = -wpre
        softplus = jnp.maximum(nw, 0.0) + jnp.log(1.0 + jnp.exp(-jnp.abs(nw)))
        lw = -jnp.exp(-softplus - 0.5)
        a = _sigmoid(a0_ref[...] + _dot(t, a2_ref[...]))
        gate = _dot(_sigmoid(xg), g2_ref[...])

        kk = k * kk_ref[...]
        kk = kk * lax.rsqrt(jnp.maximum(head_sum(kk * kk), 1e-24))
        k = k * (1.0 + (a - 1.0) * ka_ref[...])
        bonus = head_sum(r * k * rk_ref[...]) * v
        avec = -kk
        bvec = kk * a

        lc = jnp.concatenate([_dot_wx(tri, lw[i:i + tri.shape[0]], terms=2)
                              for i in range(0, nseq * c_len, tri.shape[0])], axis=0)
        lc_last = [lc[s.stop - 1:s.stop, :] for s in rsl]
        e_pos = jnp.exp(lc)
        e_neg = jnp.exp(-lc)
        rt = (r * e_pos).astype(BF16)
        at = (avec * jnp.exp(lc - lw)).astype(BF16)
        kt = (k * e_neg).astype(BF16)
        bt = (bvec * e_neg).astype(BF16)
        p_end = [jnp.exp(e) for e in lc_last]
        vb = v.astype(BF16)

        pack = lambda x, g: jnp.concatenate(
            [x[rsl[q], j * V7X_LANES:(j + 1) * V7X_LANES] for q, j in groups[g]], axis=1)
        strict = [masks[w][0] for w in gw]
        incl = [masks[w][1] for w in gw]
        a_p, r_p, b_p, k_p, v_p = ([pack(x, g) for g in n_grp] for x in (at, rt, bt, kt, vb))
        sm = [state[g, :, 0:gw[g]] for g in n_grp]
        smb = [m.astype(BF16) for m in sm]
        ar = [jnp.concatenate([a_p[g], r_p[g]], axis=0) for g in n_grp]
        gbk = [_dot_nt(ar[g], jnp.concatenate([bd(b_p[g], g), bd(k_p[g], g)], axis=0)) for g in n_grp]
        lmat = [jnp.where(strict[g], gbk[g][0:c_len, 0:gw[g]], 0.0) for g in n_grp]
        rb = [jnp.where(incl[g], gbk[g][c_len:, 0:gw[g]], 0.0).astype(BF16) for g in n_grp]
        akrk = [jnp.concatenate([jnp.where(strict[g], gbk[g][0:c_len, gw[g]:], 0.0),
                                 jnp.where(incl[g], gbk[g][c_len:, gw[g]:], 0.0)], axis=0).astype(BF16)
                for g in n_grp]
        lb = [x.astype(BF16) for x in lmat]
        pw = [_dot(lb[g], bd(lb[g], g)) for g in n_grp]
        tinv = [masks[gw[g]][2] + lmat[g] for g in n_grp]
        s0 = [_dot_nt(ar[g], bd(smb[g], g)) for g in n_grp]
        kv = [_dot(akrk[g], bd(v_p[g], g)) for g in n_grp]
        for _ in range(n_sq - 1):
            pb = [x.astype(BF16) for x in pw]
            sq = [_dot(jnp.concatenate([pb[g], tinv[g].astype(BF16)], axis=0), bd(pb[g], g)) for g in n_grp]
            tinv = [tinv[g] + sq[g][c_len:] for g in n_grp]
            pw = [sq[g][0:c_len] for g in n_grp]
        tinv = [tinv[g] + _dot(tinv[g], bd(pw[g].astype(BF16), g)) for g in n_grp]
        ub = [_dot(tinv[g], bd((s0[g][0:c_len] + kv[g][0:c_len]).astype(BF16), g)).astype(BF16) for g in n_grp]
        ys = [s0[g][c_len:] + _dot(rb[g], bd(ub[g], g)) + kv[g][c_len:] for g in n_grp]
        for g in n_grp:
            full = _dot_tn(jnp.concatenate([ub[g], v_p[g]], axis=0), jnp.concatenate([b_p[g], k_p[g]], axis=0))
            upd = diag_blocks(full, g)
            p_end_g = jnp.concatenate([p_end[q][:, j * V7X_LANES:(j + 1) * V7X_LANES] for q, j in groups[g]], axis=1)
            state[g, :, 0:gw[g]] = (sm[g] + upd) * p_end_g
        piece = {qj: ys[g][:, i * V7X_LANES:(i + 1) * V7X_LANES] for g in n_grp for i, qj in enumerate(groups[g])}
        y = jnp.concatenate([jnp.concatenate([piece[(q, j)] for j in range(width // V7X_LANES)], axis=1)
                             for q in range(nseq)], axis=0)

        inv_e = 1.0 / HEAD_DIM
        mu = head_sum(y) * inv_e
        yc = y - mu
        var = head_sum(yc * yc) * inv_e
        yn = yc * lax.rsqrt(var + GN_EPS) * lnw_ref[...] + lnb_ref[...]
        out = ((yn + bonus) * gate).astype(o_ref.dtype)
        for q in range(nseq):
            o_ref[q, sl, :] = out[rsl[q], :]
        return carry

    lax.fori_loop(0, tr // c_len, chunk, 0)


def _rwkv(z, batch, seq, prm):
    t, zin = z.shape
    width = prm["w0"].shape[-1]
    tr = min(RWKV_BLOCK, seq)
    nblk = seq // tr
    nseq = math.gcd(batch, RWKV_SEQS)
    ncol, per = width // V7X_LANES, RWKV_GROUP_LANES // V7X_LANES
    n_groups = nseq * (ncol // per) + -(-(nseq * (ncol % per)) // per)
    row = lambda x: x.reshape(1, -1).astype(F32)
    w2p = jnp.concatenate([prm["w2"], jnp.zeros((AAA_LORA, width), F32)], axis=0)
    a2p = jnp.concatenate([jnp.zeros((DECAY_LORA, width), F32), prm["a2"]], axis=0)
    hid = jnp.arange(V7X_LANES) // HEAD_DIM
    seg = (hid[:, None] == hid[None, :]).astype(BF16)
    pos = jnp.arange(math.gcd(nseq * RWKV_CHUNK, V7X_MXU_DIM))
    tri = ((pos[:, None] >= pos[None, :])
           & (pos[:, None] // RWKV_CHUNK == pos[None, :] // RWKV_CHUNK)).astype(BF16)
    params = [row(prm["mix"]), row(prm["w0"]), w2p, row(prm["a0"]), a2p.astype(BF16), prm["g2"].astype(BF16),
              row(prm["k_k"]), row(prm["k_a"]), row(prm["r_k"]), row(prm["ln_w"]), row(prm["ln_b"]),
              seg, tri]
    out = pl.pallas_call(
        functools.partial(_rwkv_body, tr=tr, width=width),
        grid=(batch // nseq, nblk),
        in_specs=[pl.BlockSpec((nseq, tr, zin), lambda b, n: (b, n, 0))]
                 + [_resident(p.shape) for p in params],
        out_specs=pl.BlockSpec((nseq, tr, width), lambda b, n: (b, n, 0)),
        out_shape=jax.ShapeDtypeStruct((batch, seq, width), BF16),
        scratch_shapes=[pltpu.VMEM((nseq, V7X_SUBLANES, zin), F32), pltpu.VMEM((nseq, tr, zin), F32),
                        pltpu.VMEM((n_groups, HEAD_DIM, RWKV_GROUP_LANES), F32)],
        compiler_params=_cparams("arbitrary", "arbitrary"),
        name="rwkv7_time_mix",
    )(z.reshape(batch, seq, zin), *params)
    return out.reshape(t, width)


def _ssm_prep_body(are_ref, aim_ref, lstep_ref, bre_ref, bim_ref, cre_ref, cim_ref,
                   kt_ref, wre_ref, wim_ref, vre_ref, vim_ref, pre_ref, pim_ref):
    lam_re = are_ref[0]
    lam_im = aim_ref[0]
    step = jnp.exp(lstep_ref[0])
    mag = jnp.exp(lam_re * step)
    ang = lam_im * step
    abar_re, abar_im = mag * jnp.cos(ang), mag * jnp.sin(ang)
    inv = 1.0 / (lam_re * lam_re + lam_im * lam_im)
    f_re = ((abar_re - 1.0) * lam_re + abar_im * lam_im) * inv
    f_im = (abar_im * lam_re - (abar_re - 1.0) * lam_im) * inv
    b_re, b_im = bre_ref[0], bim_ref[0]
    bbar_re = f_re * b_re - f_im * b_im
    bbar_im = f_re * b_im + f_im * b_re
    c_re, c_im = cre_ref[0], cim_ref[0]

    def power(tau):
        m = jnp.exp(lam_re * step * tau)
        return m * jnp.cos(ang * tau), m * jnp.sin(ang * tau)

    nt = (((1,), (1,)), ((), ()))
    for tau in range(SSM_CHUNK):
        p_re, p_im = power(float(tau))
        cp_re = c_re * p_re - c_im * p_im
        cp_im = c_re * p_im + c_im * p_re
        kt_ref[0, tau] = _dot_hi(cp_re, bbar_re, nt) - _dot_hi(cp_im, bbar_im, nt)
        q_re, q_im = power(float(SSM_CHUNK - 1 - tau))
        wre_ref[0, tau] = q_re * bbar_re - q_im * bbar_im
        wim_ref[0, tau] = q_re * bbar_im + q_im * bbar_re
        s_re, s_im = power(float(tau + 1))
        vre_ref[0, tau] = c_re * s_re - c_im * s_im
        vim_ref[0, tau] = -(c_re * s_im + c_im * s_re)
    for lvl in range(SSM_SCAN_LEVELS):
        p_re, p_im = power(float(SSM_CHUNK * 2 ** lvl))
        pre_ref[0, lvl:lvl + 1, :] = p_re
        pim_ref[0, lvl:lvl + 1, :] = p_im


def _ssm_prep(a_re, a_im, log_step, b_re, b_im, c_re, c_im):
    g, p = a_re.shape
    c = SSM_GROUP_CH
    l = SSM_CHUNK
    grp3 = lambda shape: pl.BlockSpec((1,) + shape, lambda i: (i, 0, 0))
    grp4 = lambda shape: pl.BlockSpec((1,) + shape, lambda i: (i, 0, 0, 0))
    sds = jax.ShapeDtypeStruct
    return pl.pallas_call(
        _ssm_prep_body,
        grid=(g,),
        in_specs=[grp3((1, p)), grp3((1, p)), grp3((1, 1)), grp3((c, p)), grp3((c, p)),
                  grp3((c, p)), grp3((c, p))],
        out_specs=[grp4((l, c, c)), grp4((l, c, p)), grp4((l, c, p)), grp4((l, c, p)), grp4((l, c, p)),
                   grp3((SSM_SCAN_LEVELS, p)), grp3((SSM_SCAN_LEVELS, p))],
        out_shape=[sds((g, l, c, c), F32), sds((g, l, c, p), F32), sds((g, l, c, p), F32),
                   sds((g, l, c, p), F32), sds((g, l, c, p), F32),
                   sds((g, SSM_SCAN_LEVELS, p), F32), sds((g, SSM_SCAN_LEVELS, p), F32)],
        compiler_params=_cparams("arbitrary"),
        name="ssm_discretise",
    )(a_re.reshape(g, 1, p), a_im.reshape(g, 1, p), log_step.reshape(g, 1, 1),
      jnp.swapaxes(b_re, 1, 2), jnp.swapaxes(b_im, 1, 2), c_re, c_im)


def _ssm_scan_body(u_ref, kc_ref, wc_ref, vc_ref, ek_ref, ew_ref, ev_ref, mre_ref, mim_ref, d_ref, y_ref,
                   kd_s, wd_s, vd_s, *, n_chunks):
    l = SSM_CHUNK
    n_half, _, lanes = u_ref.shape
    gh = lanes // SSM_GROUP_CH

    @pl.when(pl.program_id(0) == 0)
    def _():
        def expand(src, e_ref, row_group, lane_group):
            out = jnp.dot(src, e_ref[...], preferred_element_type=F32)
            rg = row_group(lax.broadcasted_iota(jnp.int32, out.shape, 0))
            lg = lane_group(lax.broadcasted_iota(jnp.int32, out.shape, 1))
            return jnp.where(rg == lg, out, 0.0).astype(BF16)

        rows = 512
        for h in range(n_half):
            kd_s[h, 0:lanes, 0:lanes] = jnp.zeros((lanes, lanes), BF16)
            for r0 in range(0, l * lanes, rows):
                blk = expand(kc_ref[h, r0:r0 + rows, :], ek_ref,
                             lambda r: (r // SSM_GROUP_CH) % gh, lambda c: c // SSM_GROUP_CH)
                kd_s[h, r0:r0 + rows, lanes:2 * lanes] = blk
                hi = min(r0 + lanes + rows, l * lanes)
                kd_s[h, r0 + lanes:hi, 0:lanes] = blk[0:hi - r0 - lanes]
            for ri in range(2):
                half = wd_s.shape[2] // 2
                for r0 in range(0, l * lanes, rows):
                    wd_s[h, r0:r0 + rows, ri * half:(ri + 1) * half] = expand(
                        wc_ref[h, ri, r0:r0 + rows, :], ew_ref,
                        lambda r: (r // SSM_GROUP_CH) % gh, lambda c: c // SSM_STATE)
                hv = vd_s.shape[1] // 2
                vd_s[h, ri * hv:(ri + 1) * hv, :] = expand(
                    vc_ref[h, ri], ev_ref, lambda r: r // SSM_STATE, lambda c: (c // SSM_GROUP_CH) % gh)

    row = lax.broadcasted_iota(jnp.int32, (n_chunks, 1), 0)
    halves = range(n_half)
    x32 = [[u_ref[h, pl.ds(j, n_chunks, stride=l), :] for j in range(l)] for h in halves]
    xb = [[x.astype(BF16) for x in x32[h]] for h in halves]
    z = [jnp.dot(jnp.concatenate(xb[h], axis=1), wd_s[h], preferred_element_type=F32) for h in halves]
    pairs = [[jnp.dot(jnp.concatenate([xb[h][i + 1 - tau] for tau in range(i + 2)], axis=1),
                      kd_s[h, 0:(i + 2) * lanes, :], preferred_element_type=F32) for i in range(0, l, 2)]
             for h in halves]
    y_state = []
    for h in halves:
        half = z[h].shape[1] // 2
        z_re, z_im = z[h][:, :half], z[h][:, half:]
        for lvl in range(SSM_SCAN_LEVELS):
            s = 2 ** lvl
            if s >= n_chunks:
                break
            s_re = jnp.where(row >= s, pltpu.roll(z_re, s, 0), 0.0)
            s_im = jnp.where(row >= s, pltpu.roll(z_im, s, 0), 0.0)
            a_re = mre_ref[h, lvl:lvl + 1, :]
            a_im = mim_ref[h, lvl:lvl + 1, :]
            z_re, z_im = z_re + s_re * a_re - s_im * a_im, z_im + s_re * a_im + s_im * a_re
        zin = jnp.concatenate([jnp.where(row >= 1, pltpu.roll(z_re, 1, 0), 0.0),
                               jnp.where(row >= 1, pltpu.roll(z_im, 1, 0), 0.0)], axis=1)
        y_state.append(jnp.dot(zin.astype(BF16), vd_s[h], preferred_element_type=F32))
    for h in halves:
        dvec = d_ref[h]
        for i in range(l):
            y = (pairs[h][i // 2][:, (i % 2) * lanes:(i % 2 + 1) * lanes]
                 + y_state[h][:, i * lanes:(i + 1) * lanes] + x32[h][i] * dvec)
            y_ref[h, pl.ds(i, n_chunks, stride=l), :] = y


def _ssm_scan(u, tables, batch, seq):
    n_half, t, lanes = u.shape
    l, p = SSM_CHUNK, SSM_STATE
    gh = lanes // SSM_GROUP_CH
    tok = pl.BlockSpec((n_half, seq, lanes), lambda b: (0, b, 0))
    return pl.pallas_call(
        functools.partial(_ssm_scan_body, n_chunks=seq // l),
        grid=(batch,),
        in_specs=[tok] + [_resident(w.shape) for w in tables],
        out_specs=tok,
        out_shape=jax.ShapeDtypeStruct(u.shape, F32),
        scratch_shapes=[pltpu.VMEM((n_half, l * lanes, 2 * lanes), BF16),
                        pltpu.VMEM((n_half, l * lanes, 2 * gh * p), BF16),
                        pltpu.VMEM((n_half, 2 * gh * p, l * lanes), BF16)],
        compiler_params=_cparams("arbitrary"),
        name="ssm_chunk_scan",
    )(u, *tables)


def _ssm(u_slabs, batch, seq, prm):
    n_half, t, lanes = u_slabs.shape
    c, l, p = SSM_GROUP_CH, SSM_CHUNK, SSM_STATE
    gh = lanes // c
    kt, wre, wim, vre, vim, pre, pim = _ssm_prep(prm["a_re"], prm["a_im"], prm["log_step"],
                                                 prm["b_re"], prm["b_im"], prm["c_re"], prm["c_im"])
    kc = jnp.transpose(kt.reshape(n_half, gh, l, c, c), (0, 2, 1, 4, 3)).reshape(n_half, l * lanes, c)
    wc = jnp.transpose(jnp.stack([wre, wim]).reshape(2, n_half, gh, l, c, p), (1, 0, 3, 2, 4, 5))
    wc = wc.reshape(n_half, 2, l * lanes, p)
    vc = jnp.transpose(jnp.stack([vre, vim]).reshape(2, n_half, gh, l, c, p), (1, 0, 2, 5, 3, 4))
    vc = vc.reshape(n_half, 2, gh * p, l * c)
    ek = jnp.tile(jnp.eye(c, dtype=BF16), (1, gh))
    ew = jnp.tile(jnp.eye(p, dtype=BF16), (1, gh))
    ev = jnp.broadcast_to(jnp.einsum('ij,cd->icjd', jnp.eye(l, dtype=BF16), jnp.eye(c, dtype=BF16))[:, :, :, None, :],
                          (l, c, l, gh, c)).reshape(l * c, l * lanes)
    lv = SSM_SCAN_LEVELS
    m_re = jnp.transpose(pre.reshape(n_half, gh, lv, p), (0, 2, 1, 3)).reshape(n_half, lv, gh * p)
    m_im = jnp.transpose(pim.reshape(n_half, gh, lv, p), (0, 2, 1, 3)).reshape(n_half, lv, gh * p)
    dvec = prm["d"].reshape(n_half, 1, lanes).astype(F32)
    tables = [kc.astype(BF16), wc.astype(BF16), vc.astype(BF16), ek, ew, ev, m_re, m_im, dvec]
    return _ssm_scan(u_slabs, tables, batch, seq)


def _mix_ffn_body(oa_ref, or_ref, ys_ref, gate_ref, h_ref, wa_ref, wr_ref, ws_ref, wv_ref, wg_ref, wo_ref,
                  gain_ref, wgu_ref, wd_ref, fgain_ref, out_ref, *, hidden, final_norm):
    d = h_ref.shape[1]
    y = jnp.concatenate([ys_ref[i] for i in range(ys_ref.shape[0])], axis=1)
    zg = (0.5 * y * (1.0 + jnp.tanh(math.sqrt(2.0 / math.pi) * (y + 0.044715 * (y * y * y))))).astype(BF16)
    o_ssm = (jnp.dot(zg, wv_ref[...], preferred_element_type=F32)
             * _sigmoid(jnp.dot(zg, wg_ref[...], preferred_element_type=F32)))
    gate = lambda i: gate_ref[:, i * d:(i + 1) * d].astype(F32)
    merged = (gate(0) * _dot(oa_ref[...], wa_ref[...])
              + gate(1) * _dot(or_ref[...], wr_ref[...])
              + gate(2) * _dot(o_ssm, ws_ref[...]))
    x = h_ref[...] + _dot(merged, wo_ref[...])

    u = _rmsnorm(x, gain_ref[...]).astype(BF16)
    acc = jnp.zeros(x.shape, F32)
    for c in range(hidden // FFN_CHUNK):
        lo = c * FFN_CHUNK
        a = jnp.dot(u, wgu_ref[:, lo:lo + FFN_CHUNK], preferred_element_type=F32)
        b = jnp.dot(u, wgu_ref[:, hidden + lo:hidden + lo + FFN_CHUNK], preferred_element_type=F32)
        hm = (a * _sigmoid(a) * b).astype(BF16)
        acc = acc + jnp.dot(hm, wd_ref[lo:lo + FFN_CHUNK, :], preferred_element_type=F32)
    x = x + acc
    if final_norm:
        x = _rmsnorm(x, fgain_ref[...])
    out_ref[...] = x


def _mix_ffn(o_attn, o_rwkv, y_ssm, gates, h, mix_stacks, gain, wgu_stack, wd_stack, final_gain, final_norm, layer):
    t, d = h.shape
    tm = ROW_TILE
    rows = lambda a: pl.BlockSpec((tm, a.shape[1]), lambda i: (i, 0))
    slabs = lambda a: pl.BlockSpec((a.shape[0], tm, a.shape[2]), lambda i: (0, i, 0))
    stacked = lambda w: _resident_layer(w, layer)
    whole = lambda w: _resident(w.shape)
    weights = list(mix_stacks) + [gain.reshape(1, d), wgu_stack, wd_stack, final_gain.reshape(1, d)]
    specs = [stacked(w) for w in mix_stacks] + [whole(weights[-4]), stacked(wgu_stack), stacked(wd_stack),
                                                whole(weights[-1])]
    return pl.pallas_call(
        functools.partial(_mix_ffn_body, hidden=wd_stack.shape[1], final_norm=final_norm),
        grid=(t // tm,),
        in_specs=[rows(o_attn), rows(o_rwkv), slabs(y_ssm), rows(gates), rows(h)] + specs,
        out_specs=rows(h),
        out_shape=jax.ShapeDtypeStruct((t, d), F32),
        compiler_params=_cparams("arbitrary"),
        name="merge_ffn",
    )(o_attn, o_rwkv, y_ssm, gates, h, *weights)


def kernel(x, norm_mix, w_in, rwkv_shift_mix, rwkv_w0, rwkv_w2, rwkv_a0, rwkv_a2, rwkv_g2, rwkv_k_k, rwkv_k_a, rwkv_r_k, rwkv_ln_w, rwkv_ln_b, ssm_a_re, ssm_a_im, ssm_log_step, ssm_b_re, ssm_b_im, ssm_c_re, ssm_c_im, ssm_d, ssm_glu_val, ssm_glu_gate, w_branch_attn, w_branch_rwkv, w_branch_ssm, w_out, norm_ffn, ffn_w_gate_up, ffn_w_down, norm_final):
    batch, seq, d = x.shape
    depth = w_in.shape[0]
    attn_w = w_branch_attn.shape[1]
    rwkv_in = rwkv_shift_mix.shape[1]
    ssm_w = w_branch_ssm.shape[1]
    widths = (3 * attn_w, rwkv_in, ssm_w, N_BRANCH * d)
    assert sum(widths) == w_in.shape[2]
    assert attn_w == len(ATTN_GROUPS) * HEADS_PER_GROUP * HEAD_DIM
    assert seq % ATTN_BLOCK == 0 or seq in (w for w, _ in ATTN_GROUPS)
    assert seq // SSM_CHUNK <= 2 ** SSM_SCAN_LEVELS and (batch * seq) % ROW_TILE == 0

    bf = lambda w: w.astype(BF16)
    w_in_b, wgu_b, wd_b = bf(w_in), bf(ffn_w_gate_up), bf(ffn_w_down)
    mix_b = [bf(w) for w in (w_branch_attn, w_branch_rwkv, w_branch_ssm, ssm_glu_val, ssm_glu_gate, w_out)]
    h = x.reshape(batch * seq, d)
    for l in range(depth):
        qkv, z_rwkv, u_ssm, gates = _in_proj(h, norm_mix[l], w_in_b, l, widths)
        o_attn = _attention(qkv, batch, seq)
        o_rwkv = _rwkv(z_rwkv, batch, seq, dict(
            mix=rwkv_shift_mix[l], w0=rwkv_w0[l], w2=rwkv_w2[l], a0=rwkv_a0[l], a2=rwkv_a2[l],
            g2=rwkv_g2[l], k_k=rwkv_k_k[l], k_a=rwkv_k_a[l], r_k=rwkv_r_k[l],
            ln_w=rwkv_ln_w[l], ln_b=rwkv_ln_b[l]))
        y_ssm = _ssm(u_ssm, batch, seq, dict(
            a_re=ssm_a_re[l], a_im=ssm_a_im[l], log_step=ssm_log_step[l], b_re=ssm_b_re[l],
            b_im=ssm_b_im[l], c_re=ssm_c_re[l], c_im=ssm_c_im[l], d=ssm_d[l]))
        h = _mix_ffn(o_attn, o_rwkv, y_ssm, gates, h, mix_b, norm_ffn[l], wgu_b, wd_b, norm_final,
                     l == depth - 1, l)
    return h.reshape(batch, seq, d)
```

```python
import functools
import math

import jax
import jax.numpy as jnp
from jax import lax
from jax.experimental import pallas as pl
from jax.experimental.pallas import tpu as pltpu

F32 = jnp.float32
BF16 = jnp.bfloat16

HEAD_DIM = 64
ATTN_GROUPS = ((128, 1), (512, 4), (2048, 16))
ATTN_N_BACK = 128
HEADS_PER_GROUP = 2
DECAY_LORA = 64
AAA_LORA = 64
GATE_LORA = 128
SSM_GROUP_CH = 16
SSM_STATE = 64
GN_EPS = 64e-5
NORM_EPS = 1e-6
N_BRANCH = 3

V7X_LANES = 128
V7X_SUBLANES = 8
V7X_MXU_DIM = 256
V7X_VMEM_BYTES = 64 * 1024 * 1024
VMEM_LIMIT = V7X_VMEM_BYTES - 8 * 1024 * 1024

ROW_TILE = 512
COL_CHUNK = 256
FFN_CHUNK = 256
ATTN_BLOCK = 2048
ATTN_UNROLL = 4
RWKV_BLOCK = 128
RWKV_CHUNK = 64
RWKV_SEQS = 8
RWKV_GROUP_LANES = 256
SSM_CHUNK = 16
SSM_SCAN_LEVELS = 8


def _cparams(*sem):
    return pltpu.CompilerParams(dimension_semantics=sem, vmem_limit_bytes=VMEM_LIMIT)


def _resident(shape):
    nd = len(shape)
    return pl.BlockSpec(shape, lambda *_: (0,) * nd, pipeline_mode=pl.Buffered(1))


def _resident_layer(stacked, layer):
    nd = stacked.ndim - 1
    return pl.BlockSpec((None,) + stacked.shape[1:], lambda *_: (layer,) + (0,) * nd,
                        pipeline_mode=pl.Buffered(1))


def _dot(a, b):
    return jnp.dot(a.astype(BF16), b.astype(BF16), preferred_element_type=F32)


def _dot_nt(a, b):
    return lax.dot_general(a.astype(BF16), b.astype(BF16), (((1,), (1,)), ((), ())),
                           preferred_element_type=F32)


def _dot_tn(a, b):
    return lax.dot_general(a.astype(BF16), b.astype(BF16), (((0,), (0,)), ((), ())),
                           preferred_element_type=F32)


def _split(x, terms):
    out, rest = [], x
    for _ in range(terms - 1):
        h = rest.astype(BF16)
        out.append(h)
        rest = rest - h.astype(F32)
    out.append(rest.astype(BF16))
    return out


def _dot_xw(x, w, terms=3):
    m = x.shape[0]
    y = jnp.dot(jnp.concatenate(_split(x, terms), axis=0), w, preferred_element_type=F32)
    return functools.reduce(lambda a, b: a + b, [y[i * m:(i + 1) * m] for i in range(terms)])


def _dot_wx(w, x, terms=3):
    n = x.shape[1]
    y = jnp.dot(w, jnp.concatenate(_split(x, terms), axis=1), preferred_element_type=F32)
    return functools.reduce(lambda a, b: a + b, [y[:, i * n:(i + 1) * n] for i in range(terms)])


def _dot_hi(a, b, dims=(((1,), (0,)), ((), ()))):
    ah = a.astype(BF16)
    al = (a - ah.astype(F32)).astype(BF16)
    bh = b.astype(BF16)
    bl = (b - bh.astype(F32)).astype(BF16)
    dg = functools.partial(lax.dot_general, dimension_numbers=dims, preferred_element_type=F32)
    ca, cb = dims[0][0][0], dims[0][1][0]
    if 2 * a.shape[ca] <= V7X_MXU_DIM:
        return dg(jnp.concatenate([ah, al], axis=ca), jnp.concatenate([bh, bh], axis=cb)) + dg(ah, bl)
    return dg(ah, bh) + dg(al, bh) + dg(ah, bl)


def _rmsnorm(x, gain):
    ms = jnp.mean(x * x, axis=-1, keepdims=True)
    return x * lax.rsqrt(ms + NORM_EPS) * gain


def _sigmoid(x):
    return 1.0 / (1.0 + jnp.exp(-x))


def _in_proj_body(h_ref, gain_ref, w_ref, qkv_ref, rw_ref, ssm_ref, gate_ref, *, widths):
    attn_in, rwkv_in, ssm_w, gate_w = widths
    u = _rmsnorm(h_ref[...], gain_ref[...]).astype(BF16)

    rw_off = attn_in
    ssm_off = rw_off + rwkv_in
    gate_off = ssm_off + ssm_w
    total = gate_off + gate_w

    def store(col, y):
        if col < rw_off:
            qkv_ref[col // V7X_LANES] = y
        elif col < ssm_off:
            rw_ref[:, col - rw_off:col - rw_off + V7X_LANES] = y
        elif col < gate_off:
            ssm_ref[(col - ssm_off) // V7X_LANES] = y
        else:
            gate_ref[:, col - gate_off:col - gate_off + V7X_LANES] = _sigmoid(y).astype(gate_ref.dtype)

    starts = list(range(0, total, COL_CHUNK))
    plain = [lo for lo in starts if lo + COL_CHUNK <= gate_off]
    gated = [lo for lo in starts if lo + COL_CHUNK > gate_off]
    n = min(len(plain), len(gated))
    order = (gated[:len(gated) - n] + plain[:len(plain) - n]
             + [lo for pair in zip(gated[len(gated) - n:], plain[len(plain) - n:]) for lo in pair])
    for lo in order:
        hi = min(lo + COL_CHUNK, total)
        y = jnp.dot(u, w_ref[:, lo:hi], preferred_element_type=F32)
        for j in range((hi - lo) // V7X_LANES):
            store(lo + j * V7X_LANES, y[:, j * V7X_LANES:(j + 1) * V7X_LANES])


def _in_proj(h, gain, w_stack, layer, widths):
    t, d = h.shape
    attn_in, rwkv_in, ssm_w, gate_w = widths
    n_slab = attn_in // V7X_LANES
    tm = ROW_TILE
    return pl.pallas_call(
        functools.partial(_in_proj_body, widths=widths),
        grid=(t // tm,),
        in_specs=[pl.BlockSpec((tm, d), lambda i: (i, 0)),
                  _resident((1, d)),
                  _resident_layer(w_stack, layer)],
        out_specs=[pl.BlockSpec((n_slab, tm, V7X_LANES), lambda i: (0, i, 0)),
                   pl.BlockSpec((tm, rwkv_in), lambda i: (i, 0)),
                   pl.BlockSpec((ssm_w // V7X_LANES, tm, V7X_LANES), lambda i: (0, i, 0)),
                   pl.BlockSpec((tm, gate_w), lambda i: (i, 0))],
        out_shape=[jax.ShapeDtypeStruct((n_slab, t, V7X_LANES), F32),
                   jax.ShapeDtypeStruct((t, rwkv_in), F32),
                   jax.ShapeDtypeStruct((ssm_w // V7X_LANES, t, V7X_LANES), F32),
                   jax.ShapeDtypeStruct((t, gate_w), BF16)],
        compiler_params=_cparams("arbitrary"),
        name="in_proj",
    )(h, gain.reshape(1, d), w_stack)


def _attn_body(qkv_ref, o_ref, kb0, vb0, kb1, vb1, kb2, vb2, obuf, lbuf, *, tb):
    n = pl.program_id(1)
    nb = ATTN_N_BACK
    kbufs, vbufs = (kb0, kb1, kb2), (vb0, vb1, vb2)
    n_grp = len(ATTN_GROUPS)
    lane = lax.broadcasted_iota(jnp.int32, (nb, V7X_LANES), 1)
    head0 = lane < HEAD_DIM
    qi = lax.broadcasted_iota(jnp.int32, (nb, 2 * nb), 0)
    kj = lax.broadcasted_iota(jnp.int32, (nb, 2 * nb), 1)
    band = (kj >= qi) & (kj <= qi + nb)
    scale = HEAD_DIM ** -0.5

    for g, (window, dil) in enumerate(ATTN_GROUPS):
        span = window
        kb, vb = kbufs[g], vbufs[g]

        @pl.when(n == 0)
        def _():
            kb[0:span, :] = jnp.zeros((span, V7X_LANES), F32)
            vb[0:span, :] = jnp.zeros((span, V7X_LANES), F32)

        @pl.when(n > 0)
        def _():
            kb[0:span, :] = kb[tb:tb + span, :]
            vb[0:span, :] = vb[tb:tb + span, :]

        kb[span:span + tb, :] = qkv_ref[n_grp + g]
        vb[span:span + tb, :] = qkv_ref[2 * n_grp + g]
        n_sub = tb // span

        def units(it, carry, g=g, dil=dil, span=span, kb=kb, vb=vb, n_sub=n_sub):
            bases, masks, qs, k2s, v2s = [], [], [], [], []
            for k in range(ATTN_UNROLL):
                idx = it * ATTN_UNROLL + k
                j = idx // dil
                base = j * span + (idx - j * dil)
                bases.append(base)
                masks.append(band & ((kj >= nb) | ((n * n_sub + j) > 0)))
                qs.append(qkv_ref[g, pl.ds(base, nb, stride=dil), :] * scale)
                k2s.append(kb[pl.ds(base, 2 * nb, stride=dil), :].astype(BF16))
                v2s.append(vb[pl.ds(base, 2 * nb, stride=dil), :].astype(BF16))
            scores = [[_dot_nt(jnp.where(hmask, qs[k], 0.0), k2s[k]) for hmask in (head0, ~head0)]
                      for k in range(ATTN_UNROLL)]
            probs, dens, lses = [], [], []
            for k in range(ATTN_UNROLL):
                for s in scores[k]:
                    s = jnp.where(masks[k], s, -1e30)
                    mx = jnp.max(s, axis=-1, keepdims=True)
                    p = jnp.exp(s - mx)
                    den = jnp.sum(p, axis=-1, keepdims=True)
                    probs.append(p.astype(BF16))
                    dens.append(den)
                    lses.append(mx + jnp.log(den))
            for k in range(ATTN_UNROLL):
                o0 = jnp.dot(probs[2 * k], v2s[k], preferred_element_type=F32) / dens[2 * k]
                o1 = jnp.dot(probs[2 * k + 1], v2s[k], preferred_element_type=F32) / dens[2 * k + 1]
                obuf[g, pl.ds(bases[k], nb, stride=dil), :] = jnp.where(head0, o0, o1)
                lbuf[g, pl.ds(bases[k], nb, stride=dil), :] = jnp.where(head0, lses[2 * k], lses[2 * k + 1])
            return carry

        lax.fori_loop(0, n_sub * dil // ATTN_UNROLL, units, 0)

    rows = 256
    def mix(i, carry):
        sl = pl.ds(pl.multiple_of(i * rows, rows), rows)
        ls = [lbuf[g, sl, :] for g in range(n_grp)]
        m = functools.reduce(jnp.maximum, ls)
        es = [jnp.exp(l - m) for l in ls]
        tot = functools.reduce(lambda a, b: a + b, es)
        for g in range(n_grp):
            o_ref[sl, g * V7X_LANES:(g + 1) * V7X_LANES] = (obuf[g, sl, :] * (es[g] / tot)).astype(o_ref.dtype)
        return carry
    lax.fori_loop(0, tb // rows, mix, 0)


def _attention(qkv, batch, seq):
    n_slab, t, _ = qkv.shape
    tb = min(ATTN_BLOCK, seq)
    nblk = seq // tb
    n_grp = len(ATTN_GROUPS)
    scratch = []
    for window, _ in ATTN_GROUPS:
        scratch += [pltpu.VMEM((window + tb, V7X_LANES), F32)] * 2
    scratch += [pltpu.VMEM((n_grp, tb, V7X_LANES), F32)] * 2
    return pl.pallas_call(
        functools.partial(_attn_body, tb=tb),
        grid=(batch, nblk),
        in_specs=[pl.BlockSpec((n_slab, tb, V7X_LANES), lambda b, n: (0, b * nblk + n, 0))],
        out_specs=pl.BlockSpec((tb, n_grp * V7X_LANES), lambda b, n: (b * nblk + n, 0)),
        out_shape=jax.ShapeDtypeStruct((t, n_grp * V7X_LANES), BF16),
        scratch_shapes=scratch,
        compiler_params=_cparams("arbitrary", "arbitrary"),
        name="dilated_attention",
    )(qkv)


def _rwkv_body(z_ref, mix_ref, w0_ref, w2_ref, a0_ref, a2_ref, g2_ref, kk_ref, ka_ref, rk_ref,
               lnw_ref, lnb_ref, seg_ref, tri_ref, o_ref, carry, zz_s, state, *, tr, width):
    n = pl.program_id(1)
    nseq = z_ref.shape[0]
    c_len = RWKV_CHUNK
    pad = carry.shape[1]

    @pl.when(n == 0)
    def _():
        carry[...] = jnp.zeros(carry.shape, F32)
        state[...] = jnp.zeros(state.shape, F32)

    mix = mix_ref[...]
    for q in range(nseq):
        for i in range(tr // c_len):
            cur = z_ref[q, i * c_len:(i + 1) * c_len, :]
            head = carry[q] if i == 0 else z_ref[q, i * c_len - pad:i * c_len, :]
            prev = pltpu.roll(jnp.concatenate([head, cur], axis=0), 1, 0)[pad:]
            zz_s[q, i * c_len:(i + 1) * c_len, :] = cur + (prev - cur) * mix
    carry[...] = z_ref[:, tr - pad:tr, :]

    seg = seg_ref[...]

    def head_sum(x):
        xb = x.astype(BF16)
        return jnp.concatenate([jnp.dot(xb[:, c:c + V7X_LANES], seg, preferred_element_type=F32)
                                for c in range(0, width, V7X_LANES)], axis=1)
    tri = tri_ref[...]
    lora = DECAY_LORA + AAA_LORA
    n_sq = int(math.log2(c_len)) - 1
    rsl = [slice(q * c_len, (q + 1) * c_len) for q in range(nseq)]
    per = RWKV_GROUP_LANES // V7X_LANES
    cols = [(q, j) for q in range(nseq) for j in range(width // V7X_LANES)]
    whole = [cols[i:i + per] for q in range(nseq)
             for i in range(q * (width // V7X_LANES), (q + 1) * (width // V7X_LANES) - per + 1, per)]
    rest = [c for c in cols if not any(c in g for g in whole)]
    groups = whole + [rest[i:i + per] for i in range(0, len(rest), per)]
    n_grp = range(len(groups))
    gw = [len(g) * V7X_LANES for g in groups]

    def group_masks(w):
        r = lax.broadcasted_iota(jnp.int32, (c_len, w), 0)
        l = lax.broadcasted_iota(jnp.int32, (c_len, w), 1) % HEAD_DIM
        return l < r, l <= r, (l == r).astype(F32)
    masks = {w: group_masks(w) for w in set(gw)}
    low = lax.broadcasted_iota(jnp.int32, (1, V7X_LANES), 1) < HEAD_DIM

    def bd(x, g):
        ncol = gw[g] // V7X_LANES
        zero = jnp.zeros((x.shape[0], V7X_LANES), x.dtype)
        rows = []
        for u in range(gw[g] // HEAD_DIM):
            col = u // 2
            own = jnp.where(low if u % 2 == 0 else ~low, x[:, col * V7X_LANES:(col + 1) * V7X_LANES], 0)
            rows.append(jnp.concatenate([own if c == col else zero for c in range(ncol)], axis=1))
        return jnp.concatenate(rows, axis=0)

    def diag_blocks(full, g):
        cols = []
        for c in range(gw[g] // V7X_LANES):
            lanes = slice(c * V7X_LANES, (c + 1) * V7X_LANES)
            cols.append(jnp.where(low, full[2 * c * HEAD_DIM:(2 * c + 1) * HEAD_DIM, lanes],
                                  full[(2 * c + 1) * HEAD_DIM:(2 * c + 2) * HEAD_DIM, lanes]))
        return jnp.concatenate(cols, axis=1)

    def chunk(c, carry):
        sl = pl.ds(pl.multiple_of(c * c_len, c_len), c_len)
        zz = jnp.concatenate([zz_s[q, sl, :] for q in range(nseq)], axis=0)
        r = zz[:, 0:width]
        k = zz[:, width:2 * width]
        v = zz[:, 2 * width:3 * width]
        t = zz[:, 3 * width:3 * width + lora]
        xg = zz[:, 3 * width + lora:3 * width + lora + GATE_LORA]

        wpre = w0_ref[...] + _dot_hi(jnp.tanh(t), w2_ref[...])
        lw = -math.exp(-0.5) * _sigmoid(wpre)
        a = _sigmoid(a0_ref[...] + _dot(t, a2_ref[...]))
        gate = _dot(_sigmoid(xg), g2_ref[...])

        kk = k * kk_ref[...]
        kk = kk * lax.rsqrt(jnp.maximum(head_sum(kk * kk), 1e-24))
        k = k * (1.0 + (a - 1.0) * ka_ref[...])
        bonus = head_sum(r * k * rk_ref[...]) * v
        avec = -kk
        bvec = kk * a

        lc = jnp.concatenate([_dot_wx(tri, lw[i:i + tri.shape[0]], terms=2)
                              for i in range(0, nseq * c_len, tri.shape[0])], axis=0)
        lc_last = [lc[s.stop - 1:s.stop, :] for s in rsl]
        e_pos = jnp.exp(lc)
        e_neg = jnp.exp(-lc)
        rt = (r * e_pos).astype(BF16)
        at = (avec * jnp.exp(lc - lw)).astype(BF16)
        kt = (k * e_neg).astype(BF16)
        bt = (bvec * e_neg).astype(BF16)
        p_end = [jnp.exp(e) for e in lc_last]
        vb = v.astype(BF16)

        pack = lambda x, g: jnp.concatenate(
            [x[rsl[q], j * V7X_LANES:(j + 1) * V7X_LANES] for q, j in groups[g]], axis=1)
        strict = [masks[w][0] for w in gw]
        incl = [masks[w][1] for w in gw]
        a_p, r_p, b_p, k_p, v_p = ([pack(x, g) for g in n_grp] for x in (at, rt, bt, kt, vb))
        sm = [state[g, :, 0:gw[g]] for g in n_grp]
        smb = [m.astype(BF16) for m in sm]
        ar = [jnp.concatenate([a_p[g], r_p[g]], axis=0) for g in n_grp]
        gbk = [_dot_nt(ar[g], jnp.concatenate([bd(b_p[g], g), bd(k_p[g], g)], axis=0)) for g in n_grp]
        lmat = [jnp.where(strict[g], gbk[g][0:c_len, 0:gw[g]], 0.0) for g in n_grp]
        rb = [jnp.where(incl[g], gbk[g][c_len:, 0:gw[g]], 0.0).astype(BF16) for g in n_grp]
        akrk = [jnp.concatenate([jnp.where(strict[g], gbk[g][0:c_len, gw[g]:], 0.0),
                                 jnp.where(incl[g], gbk[g][c_len:, gw[g]:], 0.0)], axis=0).astype(BF16)
                for g in n_grp]
        lb = [x.astype(BF16) for x in lmat]
        pw = [_dot(lb[g], bd(lb[g], g)) for g in n_grp]
        tinv = [masks[gw[g]][2] + lmat[g] for g in n_grp]
        s0 = [_dot_nt(ar[g], bd(smb[g], g)) for g in n_grp]
        kv = [_dot(akrk[g], bd(v_p[g], g)) for g in n_grp]
        for _ in range(n_sq - 1):
            pb = [x.astype(BF16) for x in pw]
            sq = [_dot(jnp.concatenate([pb[g], tinv[g].astype(BF16)], axis=0), bd(pb[g], g)) for g in n_grp]
            tinv = [tinv[g] + sq[g][c_len:] for g in n_grp]
            pw = [sq[g][0:c_len] for g in n_grp]
        tinv = [tinv[g] + _dot(tinv[g], bd(pw[g].astype(BF16), g)) for g in n_grp]
        ub = [_dot(tinv[g], bd((s0[g][0:c_len] + kv[g][0:c_len]).astype(BF16), g)).astype(BF16) for g in n_grp]
        ys = [s0[g][c_len:] + _dot(rb[g], bd(ub[g], g)) + kv[g][c_len:] for g in n_grp]
        for g in n_grp:
            full = _dot_tn(jnp.concatenate([ub[g], v_p[g]], axis=0), jnp.concatenate([b_p[g], k_p[g]], axis=0))
            upd = diag_blocks(full, g)
            p_end_g = jnp.concatenate([p_end[q][:, j * V7X_LANES:(j + 1) * V7X_LANES] for q, j in groups[g]], axis=1)
            state[g, :, 0:gw[g]] = (sm[g] + upd) * p_end_g
        piece = {qj: ys[g][:, i * V7X_LANES:(i + 1) * V7X_LANES] for g in n_grp for i, qj in enumerate(groups[g])}
        y = jnp.concatenate([jnp.concatenate([piece[(q, j)] for j in range(width // V7X_LANES)], axis=1)
                             for q in range(nseq)], axis=0)

        inv_e = 1.0 / HEAD_DIM
        mu = head_sum(y) * inv_e
        yc = y - mu
        var = head_sum(yc * yc) * inv_e
        yn = yc * lax.rsqrt(var + GN_EPS) * lnw_ref[...] + lnb_ref[...]
        out = ((yn + bonus) * gate).astype(o_ref.dtype)
        for q in range(nseq):
            o_ref[q, sl, :] = out[rsl[q], :]
        return carry

    lax.fori_loop(0, tr // c_len, chunk, 0)


def _rwkv(z, batch, seq, prm):
    t, zin = z.shape
    width = prm["w0"].shape[-1]
    tr = min(RWKV_BLOCK, seq)
    nblk = seq // tr
    nseq = math.gcd(batch, RWKV_SEQS)
    ncol, per = width // V7X_LANES, RWKV_GROUP_LANES // V7X_LANES
    n_groups = nseq * (ncol // per) + -(-(nseq * (ncol % per)) // per)
    row = lambda x: x.reshape(1, -1).astype(F32)
    w2p = jnp.concatenate([prm["w2"], jnp.zeros((AAA_LORA, width), F32)], axis=0)
    a2p = jnp.concatenate([jnp.zeros((DECAY_LORA, width), F32), prm["a2"]], axis=0)
    hid = jnp.arange(V7X_LANES) // HEAD_DIM
    seg = (hid[:, None] == hid[None, :]).astype(BF16)
    pos = jnp.arange(math.gcd(nseq * RWKV_CHUNK, V7X_MXU_DIM))
    tri = ((pos[:, None] >= pos[None, :])
           & (pos[:, None] // RWKV_CHUNK == pos[None, :] // RWKV_CHUNK)).astype(BF16)
    params = [row(prm["mix"]), row(prm["w0"]), w2p, row(prm["a0"]), a2p.astype(BF16), prm["g2"].astype(BF16),
              row(prm["k_k"]), row(prm["k_a"]), row(prm["r_k"]), row(prm["ln_w"]), row(prm["ln_b"]),
              seg, tri]
    out = pl.pallas_call(
        functools.partial(_rwkv_body, tr=tr, width=width),
        grid=(batch // nseq, nblk),
        in_specs=[pl.BlockSpec((nseq, tr, zin), lambda b, n: (b, n, 0))]
                 + [_resident(p.shape) for p in params],
        out_specs=pl.BlockSpec((nseq, tr, width), lambda b, n: (b, n, 0)),
        out_shape=jax.ShapeDtypeStruct((batch, seq, width), BF16),
        scratch_shapes=[pltpu.VMEM((nseq, V7X_SUBLANES, zin), F32), pltpu.VMEM((nseq, tr, zin), F32),
                        pltpu.VMEM((n_groups, HEAD_DIM, RWKV_GROUP_LANES), F32)],
        compiler_params=_cparams("arbitrary", "arbitrary"),
        name="rwkv7_time_mix",
    )(z.reshape(batch, seq, zin), *params)
    return out.reshape(t, width)


def _ssm_prep_body(are_ref, aim_ref, lstep_ref, bre_ref, bim_ref, cre_ref, cim_ref,
                   kt_ref, wre_ref, wim_ref, vre_ref, vim_ref, pre_ref, pim_ref):
    lam_re = are_ref[0]
    lam_im = aim_ref[0]
    step = jnp.exp(lstep_ref[0])
    mag = jnp.exp(lam_re * step)
    ang = lam_im * step
    abar_re, abar_im = mag * jnp.cos(ang), mag * jnp.sin(ang)
    inv = 1.0 / (lam_re * lam_re + lam_im * lam_im)
    f_re = ((abar_re - 1.0) * lam_re + abar_im * lam_im) * inv
    f_im = (abar_im * lam_re - (abar_re - 1.0) * lam_im) * inv
    b_re, b_im = bre_ref[0], bim_ref[0]
    bbar_re = f_re * b_re - f_im * b_im
    bbar_im = f_re * b_im + f_im * b_re
    c_re, c_im = cre_ref[0], cim_ref[0]

    def power(tau):
        m = jnp.exp(lam_re * step * tau)
        return m * jnp.cos(ang * tau), m * jnp.sin(ang * tau)

    nt = (((1,), (1,)), ((), ()))
    for tau in range(SSM_CHUNK):
        p_re, p_im = power(float(tau))
        cp_re = c_re * p_re - c_im * p_im
        cp_im = c_re * p_im + c_im * p_re
        kt_ref[0, tau] = _dot_hi(cp_re, bbar_re, nt) - _dot_hi(cp_im, bbar_im, nt)
        q_re, q_im = power(float(SSM_CHUNK - 1 - tau))
        wre_ref[0, tau] = q_re * bbar_re - q_im * bbar_im
        wim_ref[0, tau] = q_re * bbar_im + q_im * bbar_re
        s_re, s_im = power(float(tau + 1))
        vre_ref[0, tau] = c_re * s_re - c_im * s_im
        vim_ref[0, tau] = -(c_re * s_im + c_im * s_re)
    for lvl in range(SSM_SCAN_LEVELS):
        p_re, p_im = power(float(SSM_CHUNK * 2 ** lvl))
        pre_ref[0, lvl:lvl + 1, :] = p_re
        pim_ref[0, lvl:lvl + 1, :] = p_im


def _ssm_prep(a_re, a_im, log_step, b_re, b_im, c_re, c_im):
    g, p = a_re.shape
    c = SSM_GROUP_CH
    l = SSM_CHUNK
    grp3 = lambda shape: pl.BlockSpec((1,) + shape, lambda i: (i, 0, 0))
    grp4 = lambda shape: pl.BlockSpec((1,) + shape, lambda i: (i, 0, 0, 0))
    sds = jax.ShapeDtypeStruct
    return pl.pallas_call(
        _ssm_prep_body,
        grid=(g,),
        in_specs=[grp3((1, p)), grp3((1, p)), grp3((1, 1)), grp3((c, p)), grp3((c, p)),
                  grp3((c, p)), grp3((c, p))],
        out_specs=[grp4((l, c, c)), grp4((l, c, p)), grp4((l, c, p)), grp4((l, c, p)), grp4((l, c, p)),
                   grp3((SSM_SCAN_LEVELS, p)), grp3((SSM_SCAN_LEVELS, p))],
        out_shape=[sds((g, l, c, c), F32), sds((g, l, c, p), F32), sds((g, l, c, p), F32),
                   sds((g, l, c, p), F32), sds((g, l, c, p), F32),
                   sds((g, SSM_SCAN_LEVELS, p), F32), sds((g, SSM_SCAN_LEVELS, p), F32)],
        compiler_params=_cparams("arbitrary"),
        name="ssm_discretise",
    )(a_re.reshape(g, 1, p), a_im.reshape(g, 1, p), log_step.reshape(g, 1, 1),
      jnp.swapaxes(b_re, 1, 2), jnp.swapaxes(b_im, 1, 2), c_re, c_im)


def _ssm_scan_body(u_ref, kc_ref, wc_ref, vc_ref, ek_ref, ew_ref, ev_ref, mre_ref, mim_ref, d_ref, y_ref,
                   kd_s, wd_s, vd_s, *, n_chunks):
    l = SSM_CHUNK
    n_half, _, lanes = u_ref.shape
    gh = lanes // SSM_GROUP_CH

    @pl.when(pl.program_id(0) == 0)
    def _():
        def expand(src, e_ref, row_group, lane_group):
            out = jnp.dot(src, e_ref[...], preferred_element_type=F32)
            rg = row_group(lax.broadcasted_iota(jnp.int32, out.shape, 0))
            lg = lane_group(lax.broadcasted_iota(jnp.int32, out.shape, 1))
            return jnp.where(rg == lg, out, 0.0).astype(BF16)

        rows = 512
        for h in range(n_half):
            kd_s[h, 0:lanes, 0:lanes] = jnp.zeros((lanes, lanes), BF16)
            for r0 in range(0, l * lanes, rows):
                blk = expand(kc_ref[h, r0:r0 + rows, :], ek_ref,
                             lambda r: (r // SSM_GROUP_CH) % gh, lambda c: c // SSM_GROUP_CH)
                kd_s[h, r0:r0 + rows, lanes:2 * lanes] = blk
                hi = min(r0 + lanes + rows, l * lanes)
                kd_s[h, r0 + lanes:hi, 0:lanes] = blk[0:hi - r0 - lanes]
            for ri in range(2):
                half = wd_s.shape[2] // 2
                for r0 in range(0, l * lanes, rows):
                    wd_s[h, r0:r0 + rows, ri * half:(ri + 1) * half] = expand(
                        wc_ref[h, ri, r0:r0 + rows, :], ew_ref,
                        lambda r: (r // SSM_GROUP_CH) % gh, lambda c: c // SSM_STATE)
                hv = vd_s.shape[1] // 2
                vd_s[h, ri * hv:(ri + 1) * hv, :] = expand(
                    vc_ref[h, ri], ev_ref, lambda r: r // SSM_STATE, lambda c: (c // SSM_GROUP_CH) % gh)

    row = lax.broadcasted_iota(jnp.int32, (n_chunks, 1), 0)
    halves = range(n_half)
    x32 = [[u_ref[h, pl.ds(j, n_chunks, stride=l), :] for j in range(l)] for h in halves]
    xb = [[x.astype(BF16) for x in x32[h]] for h in halves]
    z = [jnp.dot(jnp.concatenate(xb[h], axis=1), wd_s[h], preferred_element_type=F32) for h in halves]
    pairs = [[jnp.dot(jnp.concatenate([xb[h][i + 1 - tau] for tau in range(i + 2)], axis=1),
                      kd_s[h, 0:(i + 2) * lanes, :], preferred_element_type=F32) for i in range(0, l, 2)]
             for h in halves]
    y_state = []
    for h in halves:
        half = z[h].shape[1] // 2
        z_re, z_im = z[h][:, :half], z[h][:, half:]
        for lvl in range(SSM_SCAN_LEVELS):
            s = 2 ** lvl
            if s >= n_chunks:
                break
            s_re = jnp.where(row >= s, pltpu.roll(z_re, s, 0), 0.0)
            s_im = jnp.where(row >= s, pltpu.roll(z_im, s, 0), 0.0)
            a_re = mre_ref[h, lvl:lvl + 1, :]
            a_im = mim_ref[h, lvl:lvl + 1, :]
            z_re, z_im = z_re + s_re * a_re - s_im * a_im, z_im + s_re * a_im + s_im * a_re
        zin = jnp.concatenate([jnp.where(row >= 1, pltpu.roll(z_re, 1, 0), 0.0),
                               jnp.where(row >= 1, pltpu.roll(z_im, 1, 0), 0.0)], axis=1)
        y_state.append(jnp.dot(zin.astype(BF16), vd_s[h], preferred_element_type=F32))
    for h in halves:
        dvec = d_ref[h]
        for i in range(l):
            y = (pairs[h][i // 2][:, (i % 2) * lanes:(i % 2 + 1) * lanes]
                 + y_state[h][:, i * lanes:(i + 1) * lanes] + x32[h][i] * dvec)
            y_ref[h, pl.ds(i, n_chunks, stride=l), :] = y


def _ssm_scan(u, tables, batch, seq):
    n_half, t, lanes = u.shape
    l, p = SSM_CHUNK, SSM_STATE
    gh = lanes // SSM_GROUP_CH
    tok = pl.BlockSpec((n_half, seq, lanes), lambda b: (0, b, 0))
    return pl.pallas_call(
        functools.partial(_ssm_scan_body, n_chunks=seq // l),
        grid=(batch,),
        in_specs=[tok] + [_resident(w.shape) for w in tables],
        out_specs=tok,
        out_shape=jax.ShapeDtypeStruct(u.shape, F32),
        scratch_shapes=[pltpu.VMEM((n_half, l * lanes, 2 * lanes), BF16),
                        pltpu.VMEM((n_half, l * lanes, 2 * gh * p), BF16),
                        pltpu.VMEM((n_half, 2 * gh * p, l * lanes), BF16)],
        compiler_params=_cparams("arbitrary"),
        name="ssm_chunk_scan",
    )(u, *tables)


def _ssm(u_slabs, batch, seq, prm):
    n_half, t, lanes = u_slabs.shape
    c, l, p = SSM_GROUP_CH, SSM_CHUNK, SSM_STATE
    gh = lanes // c
    kt, wre, wim, vre, vim, pre, pim = _ssm_prep(prm["a_re"], prm["a_im"], prm["log_step"],
                                                 prm["b_re"], prm["b_im"], prm["c_re"], prm["c_im"])
    kc = jnp.transpose(kt.reshape(n_half, gh, l, c, c), (0, 2, 1, 4, 3)).reshape(n_half, l * lanes, c)
    wc = jnp.transpose(jnp.stack([wre, wim]).reshape(2, n_half, gh, l, c, p), (1, 0, 3, 2, 4, 5))
    wc = wc.reshape(n_half, 2, l * lanes, p)
    vc = jnp.transpose(jnp.stack([vre, vim]).reshape(2, n_half, gh, l, c, p), (1, 0, 2, 5, 3, 4))
    vc = vc.reshape(n_half, 2, gh * p, l * c)
    ek = jnp.tile(jnp.eye(c, dtype=BF16), (1, gh))
    ew = jnp.tile(jnp.eye(p, dtype=BF16), (1, gh))
    ev = jnp.broadcast_to(jnp.einsum('ij,cd->icjd', jnp.eye(l, dtype=BF16), jnp.eye(c, dtype=BF16))[:, :, :, None, :],
                          (l, c, l, gh, c)).reshape(l * c, l * lanes)
    lv = SSM_SCAN_LEVELS
    m_re = jnp.transpose(pre.reshape(n_half, gh, lv, p), (0, 2, 1, 3)).reshape(n_half, lv, gh * p)
    m_im = jnp.transpose(pim.reshape(n_half, gh, lv, p), (0, 2, 1, 3)).reshape(n_half, lv, gh * p)
    dvec = prm["d"].reshape(n_half, 1, lanes).astype(F32)
    tables = [kc.astype(BF16), wc.astype(BF16), vc.astype(BF16), ek, ew, ev, m_re, m_im, dvec]
    return _ssm_scan(u_slabs, tables, batch, seq)


def _mix_ffn_body(oa_ref, or_ref, ys_ref, gate_ref, h_ref, wa_ref, wr_ref, ws_ref, wv_ref, wg_ref, wo_ref,
                  gain_ref, wgu_ref, wd_ref, fgain_ref, out_ref, *, hidden, final_norm):
    d = h_ref.shape[1]
    y = jnp.concatenate([ys_ref[i] for i in range(ys_ref.shape[0])], axis=1)
    zg = (0.5 * y * (1.0 + jnp.tanh(math.sqrt(2.0 / math.pi) * (y + 0.044715 * (y * y * y))))).astype(BF16)
    o_ssm = (jnp.dot(zg, wv_ref[...], preferred_element_type=F32)
             * _sigmoid(jnp.dot(zg, wg_ref[...], preferred_element_type=F32)))
    gate = lambda i: gate_ref[:, i * d:(i + 1) * d].astype(F32)
    merged = (gate(0) * _dot(oa_ref[...], wa_ref[...])
              + gate(1) * _dot(or_ref[...], wr_ref[...])
              + gate(2) * _dot(o_ssm, ws_ref[...]))
    x = h_ref[...] + _dot(merged, wo_ref[...])

    u = _rmsnorm(x, gain_ref[...]).astype(BF16)
    acc = jnp.zeros(x.shape, F32)
    for c in range(hidden // FFN_CHUNK):
        lo = c * FFN_CHUNK
        a = jnp.dot(u, wgu_ref[:, lo:lo + FFN_CHUNK], preferred_element_type=F32)
        b = jnp.dot(u, wgu_ref[:, hidden + lo:hidden + lo + FFN_CHUNK], preferred_element_type=F32)
        hm = (a * _sigmoid(a) * b).astype(BF16)
        acc = acc + jnp.dot(hm, wd_ref[lo:lo + FFN_CHUNK, :], preferred_element_type=F32)
    x = x + acc
    if final_norm:
        x = _rmsnorm(x, fgain_ref[...])
    out_ref[...] = x


def _mix_ffn(o_attn, o_rwkv, y_ssm, gates, h, mix_stacks, gain, wgu_stack, wd_stack, final_gain, final_norm, layer):
    t, d = h.shape
    tm = ROW_TILE
    rows = lambda a: pl.BlockSpec((tm, a.shape[1]), lambda i: (i, 0))
    slabs = lambda a: pl.BlockSpec((a.shape[0], tm, a.shape[2]), lambda i: (0, i, 0))
    stacked = lambda w: _resident_layer(w, layer)
    whole = lambda w: _resident(w.shape)
    weights = list(mix_stacks) + [gain.reshape(1, d), wgu_stack, wd_stack, final_gain.reshape(1, d)]
    specs = [stacked(w) for w in mix_stacks] + [whole(weights[-4]), stacked(wgu_stack), stacked(wd_stack),
                                                whole(weights[-1])]
    return pl.pallas_call(
        functools.partial(_mix_ffn_body, hidden=wd_stack.shape[1], final_norm=final_norm),
        grid=(t // tm,),
        in_specs=[rows(o_attn), rows(o_rwkv), slabs(y_ssm), rows(gates), rows(h)] + specs,
        out_specs=rows(h),
        out_shape=jax.ShapeDtypeStruct((t, d), F32),
        compiler_params=_cparams("arbitrary"),
        name="merge_ffn",
    )(o_attn, o_rwkv, y_ssm, gates, h, *weights)


def kernel(x, norm_mix, w_in, rwkv_shift_mix, rwkv_w0, rwkv_w2, rwkv_a0, rwkv_a2, rwkv_g2, rwkv_k_k, rwkv_k_a, rwkv_r_k, rwkv_ln_w, rwkv_ln_b, ssm_a_re, ssm_a_im, ssm_log_step, ssm_b_re, ssm_b_im, ssm_c_re, ssm_c_im, ssm_d, ssm_glu_val, ssm_glu_gate, w_branch_attn, w_branch_rwkv, w_branch_ssm, w_out, norm_ffn, ffn_w_gate_up, ffn_w_down, norm_final):
    batch, seq, d = x.shape
    depth = w_in.shape[0]
    attn_w = w_branch_attn.shape[1]
    rwkv_in = rwkv_shift_mix.shape[1]
    ssm_w = w_branch_ssm.shape[1]
    widths = (3 * attn_w, rwkv_in, ssm_w, N_BRANCH * d)
    assert sum(widths) == w_in.shape[2]
    assert attn_w == len(ATTN_GROUPS) * HEADS_PER_GROUP * HEAD_DIM
    assert seq % ATTN_BLOCK == 0 or seq in (w for w, _ in ATTN_GROUPS)
    assert seq // SSM_CHUNK <= 2 ** SSM_SCAN_LEVELS and (batch * seq) % ROW_TILE == 0

    bf = lambda w: w.astype(BF16)
    w_in_b, wgu_b, wd_b = bf(w_in), bf(ffn_w_gate_up), bf(ffn_w_down)
    mix_b = [bf(w) for w in (w_branch_attn, w_branch_rwkv, w_branch_ssm, ssm_glu_val, ssm_glu_gate, w_out)]
    h = x.reshape(batch * seq, d)
    for l in range(depth):
        qkv, z_rwkv, u_ssm, gates = _in_proj(h, norm_mix[l], w_in_b, l, widths)
        o_attn = _attention(qkv, batch, seq)
        o_rwkv = _rwkv(z_rwkv, batch, seq, dict(
            mix=rwkv_shift_mix[l], w0=rwkv_w0[l], w2=rwkv_w2[l], a0=rwkv_a0[l], a2=rwkv_a2[l],
            g2=rwkv_g2[l], k_k=rwkv_k_k[l], k_a=rwkv_k_a[l], r_k=rwkv_r_k[l],
            ln_w=rwkv_ln_w[l], ln_b=rwkv_ln_b[l]))
        y_ssm = _ssm(u_ssm, batch, seq, dict(
            a_re=ssm_a_re[l], a_im=ssm_a_im[l], log_step=ssm_log_step[l], b_re=ssm_b_re[l],
            b_im=ssm_b_im[l], c_re=ssm_c_re[l], c_im=ssm_c_im[l], d=ssm_d[l]))
        h = _mix_ffn(o_attn, o_rwkv, y_ssm, gates, h, mix_b, norm_ffn[l], wgu_b, wd_b, norm_final,
                     l == depth - 1, l)
    return h.reshape(batch, seq, d)
```

```python
import functools
import math

import jax
import jax.numpy as jnp
from jax import lax
from jax.experimental import pallas as pl
from jax.experimental.pallas import tpu as pltpu

F32 = jnp.float32
BF16 = jnp.bfloat16

HEAD_DIM = 64
ATTN_GROUPS = ((128, 1), (512, 4), (2048, 16))
ATTN_N_BACK = 128
HEADS_PER_GROUP = 2
DECAY_LORA = 64
AAA_LORA = 64
GATE_LORA = 128
SSM_GROUP_CH = 16
SSM_STATE = 64
GN_EPS = 64e-5
NORM_EPS = 1e-6
N_BRANCH = 3

V7X_LANES = 128
V7X_SUBLANES = 8
V7X_MXU_DIM = 256
V7X_VMEM_BYTES = 64 * 1024 * 1024
VMEM_LIMIT = V7X_VMEM_BYTES - 8 * 1024 * 1024

ROW_TILE = 512
COL_CHUNK = 256
FFN_CHUNK = 256
ATTN_BLOCK = 2048
ATTN_UNROLL = 4
RWKV_BLOCK = 128
RWKV_CHUNK = 64
RWKV_SEQS = 8
RWKV_GROUP_LANES = 256
SSM_CHUNK = 16
SSM_SCAN_LEVELS = 8


def _cparams(*sem):
    return pltpu.CompilerParams(dimension_semantics=sem, vmem_limit_bytes=VMEM_LIMIT)


def _resident(shape):
    nd = len(shape)
    return pl.BlockSpec(shape, lambda *_: (0,) * nd, pipeline_mode=pl.Buffered(1))


def _resident_layer(stacked, layer):
    nd = stacked.ndim - 1
    return pl.BlockSpec((None,) + stacked.shape[1:], lambda *_: (layer,) + (0,) * nd,
                        pipeline_mode=pl.Buffered(1))


def _dot(a, b):
    return jnp.dot(a.astype(BF16), b.astype(BF16), preferred_element_type=F32)


def _dot_nt(a, b):
    return lax.dot_general(a.astype(BF16), b.astype(BF16), (((1,), (1,)), ((), ())),
                           preferred_element_type=F32)


def _dot_tn(a, b):
    return lax.dot_general(a.astype(BF16), b.astype(BF16), (((0,), (0,)), ((), ())),
                           preferred_element_type=F32)


def _split(x, terms):
    out, rest = [], x
    for _ in range(terms - 1):
        h = rest.astype(BF16)
        out.append(h)
        rest = rest - h.astype(F32)
    out.append(rest.astype(BF16))
    return out


def _dot_xw(x, w, terms=3):
    m = x.shape[0]
    y = jnp.dot(jnp.concatenate(_split(x, terms), axis=0), w, preferred_element_type=F32)
    return functools.reduce(lambda a, b: a + b, [y[i * m:(i + 1) * m] for i in range(terms)])


def _dot_wx(w, x, terms=3):
    n = x.shape[1]
    y = jnp.dot(w, jnp.concatenate(_split(x, terms), axis=1), preferred_element_type=F32)
    return functools.reduce(lambda a, b: a + b, [y[:, i * n:(i + 1) * n] for i in range(terms)])


def _dot_hi(a, b, dims=(((1,), (0,)), ((), ()))):
    ah = a.astype(BF16)
    al = (a - ah.astype(F32)).astype(BF16)
    bh = b.astype(BF16)
    bl = (b - bh.astype(F32)).astype(BF16)
    dg = functools.partial(lax.dot_general, dimension_numbers=dims, preferred_element_type=F32)
    ca, cb = dims[0][0][0], dims[0][1][0]
    if 2 * a.shape[ca] <= V7X_MXU_DIM:
        return dg(jnp.concatenate([ah, al], axis=ca), jnp.concatenate([bh, bh], axis=cb)) + dg(ah, bl)
    return dg(ah, bh) + dg(al, bh) + dg(ah, bl)


def _rmsnorm(x, gain):
    ms = jnp.mean(x * x, axis=-1, keepdims=True)
    return x * lax.rsqrt(ms + NORM_EPS) * gain


def _sigmoid(x):
    return 1.0 / (1.0 + jnp.exp(-x))


def _in_proj_body(h_ref, gain_ref, w_ref, qkv_ref, rw_ref, ssm_ref, gate_ref, *, widths):
    attn_in, rwkv_in, ssm_w, gate_w = widths
    half = h_ref.shape[0] // 2
    row_sets = [slice(0, half), slice(half, 2 * half)]
    us = [_rmsnorm(h_ref[rows, :], gain_ref[...]).astype(BF16) for rows in row_sets]

    rw_off = attn_in
    ssm_off = rw_off + rwkv_in
    gate_off = ssm_off + ssm_w
    total = gate_off + gate_w

    def store(rows, col, y):
        if col < rw_off:
            qkv_ref[col // V7X_LANES, rows, :] = y
        elif col < ssm_off:
            rw_ref[rows, col - rw_off:col - rw_off + V7X_LANES] = y
        elif col < gate_off:
            ssm_ref[(col - ssm_off) // V7X_LANES, rows, :] = y
        else:
            gate_ref[rows, col - gate_off:col - gate_off + V7X_LANES] = _sigmoid(y).astype(gate_ref.dtype)

    starts = list(range(0, total, COL_CHUNK))
    plain = [lo for lo in starts if lo + COL_CHUNK <= gate_off]
    gated = [lo for lo in starts if lo + COL_CHUNK > gate_off]
    n = min(len(plain), len(gated))
    order = (gated[:len(gated) - n] + plain[:len(plain) - n]
             + [lo for pair in zip(gated[len(gated) - n:], plain[len(plain) - n:]) for lo in pair])
    for rows, u in zip(row_sets, us):
        for lo in order:
            hi = min(lo + COL_CHUNK, total)
            y = jnp.dot(u, w_ref[:, lo:hi], preferred_element_type=F32)
            for j in range((hi - lo) // V7X_LANES):
                store(rows, lo + j * V7X_LANES, y[:, j * V7X_LANES:(j + 1) * V7X_LANES])


def _in_proj(h, gain, w_stack, layer, widths):
    t, d = h.shape
    attn_in, rwkv_in, ssm_w, gate_w = widths
    n_slab = attn_in // V7X_LANES
    tm = ROW_TILE
    return pl.pallas_call(
        functools.partial(_in_proj_body, widths=widths),
        grid=(t // tm,),
        in_specs=[pl.BlockSpec((tm, d), lambda i: (i, 0)),
                  _resident((1, d)),
                  _resident_layer(w_stack, layer)],
        out_specs=[pl.BlockSpec((n_slab, tm, V7X_LANES), lambda i: (0, i, 0)),
                   pl.BlockSpec((tm, rwkv_in), lambda i: (i, 0)),
                   pl.BlockSpec((ssm_w // V7X_LANES, tm, V7X_LANES), lambda i: (0, i, 0)),
                   pl.BlockSpec((tm, gate_w), lambda i: (i, 0))],
        out_shape=[jax.ShapeDtypeStruct((n_slab, t, V7X_LANES), F32),
                   jax.ShapeDtypeStruct((t, rwkv_in), F32),
                   jax.ShapeDtypeStruct((ssm_w // V7X_LANES, t, V7X_LANES), F32),
                   jax.ShapeDtypeStruct((t, gate_w), BF16)],
        compiler_params=_cparams("arbitrary"),
        name="in_proj",
    )(h, gain.reshape(1, d), w_stack)


def _attn_body(qkv_ref, o_ref, kb0, vb0, kb1, vb1, kb2, vb2, obuf, lbuf, *, tb):
    n = pl.program_id(1)
    nb = ATTN_N_BACK
    kbufs, vbufs = (kb0, kb1, kb2), (vb0, vb1, vb2)
    n_grp = len(ATTN_GROUPS)
    lane = lax.broadcasted_iota(jnp.int32, (nb, V7X_LANES), 1)
    head0 = lane < HEAD_DIM
    qi = lax.broadcasted_iota(jnp.int32, (nb, 2 * nb), 0)
    kj = lax.broadcasted_iota(jnp.int32, (nb, 2 * nb), 1)
    band = (kj >= qi) & (kj <= qi + nb)
    scale = HEAD_DIM ** -0.5

    for g, (window, dil) in enumerate(ATTN_GROUPS):
        span = window
        kb, vb = kbufs[g], vbufs[g]

        @pl.when(n == 0)
        def _():
            kb[0:span, :] = jnp.zeros((span, V7X_LANES), F32)
            vb[0:span, :] = jnp.zeros((span, V7X_LANES), F32)

        @pl.when(n > 0)
        def _():
            kb[0:span, :] = kb[tb:tb + span, :]
            vb[0:span, :] = vb[tb:tb + span, :]

        kb[span:span + tb, :] = qkv_ref[n_grp + g]
        vb[span:span + tb, :] = qkv_ref[2 * n_grp + g]
        n_sub = tb // span

        def units(it, carry, g=g, dil=dil, span=span, kb=kb, vb=vb, n_sub=n_sub):
            bases, masks, qs, k2s, v2s = [], [], [], [], []
            for k in range(ATTN_UNROLL):
                idx = it * ATTN_UNROLL + k
                j = idx // dil
                base = j * span + (idx - j * dil)
                bases.append(base)
                masks.append(band & ((kj >= nb) | ((n * n_sub + j) > 0)))
                qs.append(qkv_ref[g, pl.ds(base, nb, stride=dil), :] * scale)
                k2s.append(kb[pl.ds(base, 2 * nb, stride=dil), :].astype(BF16))
                v2s.append(vb[pl.ds(base, 2 * nb, stride=dil), :].astype(BF16))
            scores = [[_dot_nt(jnp.where(hmask, qs[k], 0.0), k2s[k]) for hmask in (head0, ~head0)]
                      for k in range(ATTN_UNROLL)]
            probs, dens, lses = [], [], []
            for k in range(ATTN_UNROLL):
                for s in scores[k]:
                    s = jnp.where(masks[k], s, -1e30)
                    mx = jnp.max(s, axis=-1, keepdims=True)
                    p = jnp.exp(s - mx)
                    den = jnp.sum(p, axis=-1, keepdims=True)
                    probs.append(p.astype(BF16))
                    dens.append(den)
                    lses.append(mx + jnp.log(den))
            for k in range(ATTN_UNROLL):
                o0 = jnp.dot(probs[2 * k], v2s[k], preferred_element_type=F32) / dens[2 * k]
                o1 = jnp.dot(probs[2 * k + 1], v2s[k], preferred_element_type=F32) / dens[2 * k + 1]
                obuf[g, pl.ds(bases[k], nb, stride=dil), :] = jnp.where(head0, o0, o1)
                lbuf[g, pl.ds(bases[k], nb, stride=dil), :] = jnp.where(head0, lses[2 * k], lses[2 * k + 1])
            return carry

        lax.fori_loop(0, n_sub * dil // ATTN_UNROLL, units, 0)

    rows = 256
    def mix(i, carry):
        sl = pl.ds(pl.multiple_of(i * rows, rows), rows)
        ls = [lbuf[g, sl, :] for g in range(n_grp)]
        m = functools.reduce(jnp.maximum, ls)
        es = [jnp.exp(l - m) for l in ls]
        tot = functools.reduce(lambda a, b: a + b, es)
        for g in range(n_grp):
            o_ref[sl, g * V7X_LANES:(g + 1) * V7X_LANES] = (obuf[g, sl, :] * (es[g] / tot)).astype(o_ref.dtype)
        return carry
    lax.fori_loop(0, tb // rows, mix, 0)


def _attention(qkv, batch, seq):
    n_slab, t, _ = qkv.shape
    tb = min(ATTN_BLOCK, seq)
    nblk = seq // tb
    n_grp = len(ATTN_GROUPS)
    scratch = []
    for window, _ in ATTN_GROUPS:
        scratch += [pltpu.VMEM((window + tb, V7X_LANES), F32)] * 2
    scratch += [pltpu.VMEM((n_grp, tb, V7X_LANES), F32)] * 2
    return pl.pallas_call(
        functools.partial(_attn_body, tb=tb),
        grid=(batch, nblk),
        in_specs=[pl.BlockSpec((n_slab, tb, V7X_LANES), lambda b, n: (0, b * nblk + n, 0))],
        out_specs=pl.BlockSpec((tb, n_grp * V7X_LANES), lambda b, n: (b * nblk + n, 0)),
        out_shape=jax.ShapeDtypeStruct((t, n_grp * V7X_LANES), BF16),
        scratch_shapes=scratch,
        compiler_params=_cparams("arbitrary", "arbitrary"),
        name="dilated_attention",
    )(qkv)


def _rwkv_body(z_ref, mix_ref, w0_ref, w2_ref, a0_ref, a2_ref, g2_ref, kk_ref, ka_ref, rk_ref,
               lnw_ref, lnb_ref, seg_ref, tri_ref, o_ref, carry, zz_s, state, *, tr, width):
    n = pl.program_id(1)
    nseq = z_ref.shape[0]
    c_len = RWKV_CHUNK
    pad = carry.shape[1]

    @pl.when(n == 0)
    def _():
        carry[...] = jnp.zeros(carry.shape, F32)
        state[...] = jnp.zeros(state.shape, F32)

    mix = mix_ref[...]
    for q in range(nseq):
        for i in range(tr // c_len):
            cur = z_ref[q, i * c_len:(i + 1) * c_len, :]
            head = carry[q] if i == 0 else z_ref[q, i * c_len - pad:i * c_len, :]
            prev = pltpu.roll(jnp.concatenate([head, cur], axis=0), 1, 0)[pad:]
            zz_s[q, i * c_len:(i + 1) * c_len, :] = cur + (prev - cur) * mix
    carry[...] = z_ref[:, tr - pad:tr, :]

    seg = seg_ref[...]

    def head_sum(x):
        xb = x.astype(BF16)
        return jnp.concatenate([jnp.dot(xb[:, c:c + V7X_LANES], seg, preferred_element_type=F32)
                                for c in range(0, width, V7X_LANES)], axis=1)
    tri = tri_ref[...]
    lora = DECAY_LORA + AAA_LORA
    n_sq = int(math.log2(c_len)) - 1
    rsl = [slice(q * c_len, (q + 1) * c_len) for q in range(nseq)]
    per = RWKV_GROUP_LANES // V7X_LANES
    cols = [(q, j) for q in range(nseq) for j in range(width // V7X_LANES)]
    whole = [cols[i:i + per] for q in range(nseq)
             for i in range(q * (width // V7X_LANES), (q + 1) * (width // V7X_LANES) - per + 1, per)]
    rest = [c for c in cols if not any(c in g for g in whole)]
    groups = whole + [rest[i:i + per] for i in range(0, len(rest), per)]
    n_grp = range(len(groups))
    gw = [len(g) * V7X_LANES for g in groups]

    def group_masks(w):
        r = lax.broadcasted_iota(jnp.int32, (c_len, w), 0)
        l = lax.broadcasted_iota(jnp.int32, (c_len, w), 1) % HEAD_DIM
        return l < r, l <= r, (l == r).astype(F32)
    masks = {w: group_masks(w) for w in set(gw)}
    low = lax.broadcasted_iota(jnp.int32, (1, V7X_LANES), 1) < HEAD_DIM

    def bd(x, g):
        ncol = gw[g] // V7X_LANES
        zero = jnp.zeros((x.shape[0], V7X_LANES), x.dtype)
        rows = []
        for u in range(gw[g] // HEAD_DIM):
            col = u // 2
            own = jnp.where(low if u % 2 == 0 else ~low, x[:, col * V7X_LANES:(col + 1) * V7X_LANES], 0)
            rows.append(jnp.concatenate([own if c == col else zero for c in range(ncol)], axis=1))
        return jnp.concatenate(rows, axis=0)

    def diag_blocks(full, g):
        cols = []
        for c in range(gw[g] // V7X_LANES):
            lanes = slice(c * V7X_LANES, (c + 1) * V7X_LANES)
            cols.append(jnp.where(low, full[2 * c * HEAD_DIM:(2 * c + 1) * HEAD_DIM, lanes],
                                  full[(2 * c + 1) * HEAD_DIM:(2 * c + 2) * HEAD_DIM, lanes]))
        return jnp.concatenate(cols, axis=1)

    def chunk(c, carry):
        sl = pl.ds(pl.multiple_of(c * c_len, c_len), c_len)
        zz = jnp.concatenate([zz_s[q, sl, :] for q in range(nseq)], axis=0)
        r = zz[:, 0:width]
        k = zz[:, width:2 * width]
        v = zz[:, 2 * width:3 * width]
        t = zz[:, 3 * width:3 * width + lora]
        xg = zz[:, 3 * width + lora:3 * width + lora + GATE_LORA]

        wpre = w0_ref[...] + _dot_hi(jnp.tanh(t), w2_ref[...])
        lw = -math.exp(-0.5) * _sigmoid(wpre)
        a = _sigmoid(a0_ref[...] + _dot(t, a2_ref[...]))
        gate = _dot(_sigmoid(xg), g2_ref[...])

        kk = k * kk_ref[...]
        kk = kk * lax.rsqrt(jnp.maximum(head_sum(kk * kk), 1e-24))
        k = k * (1.0 + (a - 1.0) * ka_ref[...])
        bonus = head_sum(r * k * rk_ref[...]) * v
        avec = -kk
        bvec = kk * a

        lc = jnp.concatenate([_dot_wx(tri, lw[i:i + tri.shape[0]], terms=2)
                              for i in range(0, nseq * c_len, tri.shape[0])], axis=0)
        lc_last = [lc[s.stop - 1:s.stop, :] for s in rsl]
        e_pos = jnp.exp(lc)
        e_neg = jnp.exp(-lc)
        rt = (r * e_pos).astype(BF16)
        at = (avec * jnp.exp(lc - lw)).astype(BF16)
        kt = (k * e_neg).astype(BF16)
        bt = (bvec * e_neg).astype(BF16)
        p_end = [jnp.exp(e) for e in lc_last]
        vb = v.astype(BF16)

        pack = lambda x, g: jnp.concatenate(
            [x[rsl[q], j * V7X_LANES:(j + 1) * V7X_LANES] for q, j in groups[g]], axis=1)
        strict = [masks[w][0] for w in gw]
        incl = [masks[w][1] for w in gw]
        a_p, r_p, b_p, k_p, v_p = ([pack(x, g) for g in n_grp] for x in (at, rt, bt, kt, vb))
        sm = [state[g, :, 0:gw[g]] for g in n_grp]
        smb = [m.astype(BF16) for m in sm]
        ar = [jnp.concatenate([a_p[g], r_p[g]], axis=0) for g in n_grp]
        gbk = [_dot_nt(ar[g], jnp.concatenate([bd(b_p[g], g), bd(k_p[g], g)], axis=0)) for g in n_grp]
        lmat = [jnp.where(strict[g], gbk[g][0:c_len, 0:gw[g]], 0.0) for g in n_grp]
        rb = [jnp.where(incl[g], gbk[g][c_len:, 0:gw[g]], 0.0).astype(BF16) for g in n_grp]
        akrk = [jnp.concatenate([jnp.where(strict[g], gbk[g][0:c_len, gw[g]:], 0.0),
                                 jnp.where(incl[g], gbk[g][c_len:, gw[g]:], 0.0)], axis=0).astype(BF16)
                for g in n_grp]
        lb = [x.astype(BF16) for x in lmat]
        pw = [_dot(lb[g], bd(lb[g], g)) for g in n_grp]
        tinv = [masks[gw[g]][2] + lmat[g] for g in n_grp]
        s0 = [_dot_nt(ar[g], bd(smb[g], g)) for g in n_grp]
        kv = [_dot(akrk[g], bd(v_p[g], g)) for g in n_grp]
        for _ in range(n_sq - 1):
            pb = [x.astype(BF16) for x in pw]
            sq = [_dot(jnp.concatenate([pb[g], tinv[g].astype(BF16)], axis=0), bd(pb[g], g)) for g in n_grp]
            tinv = [tinv[g] + sq[g][c_len:] for g in n_grp]
            pw = [sq[g][0:c_len] for g in n_grp]
        tinv = [tinv[g] + _dot(tinv[g], bd(pw[g].astype(BF16), g)) for g in n_grp]
        ub = [_dot(tinv[g], bd((s0[g][0:c_len] + kv[g][0:c_len]).astype(BF16), g)).astype(BF16) for g in n_grp]
        ys = [s0[g][c_len:] + _dot(rb[g], bd(ub[g], g)) + kv[g][c_len:] for g in n_grp]
        for g in n_grp:
            full = _dot_tn(jnp.concatenate([ub[g], v_p[g]], axis=0), jnp.concatenate([b_p[g], k_p[g]], axis=0))
            upd = diag_blocks(full, g)
            p_end_g = jnp.concatenate([p_end[q][:, j * V7X_LANES:(j + 1) * V7X_LANES] for q, j in groups[g]], axis=1)
            state[g, :, 0:gw[g]] = (sm[g] + upd) * p_end_g
        piece = {qj: ys[g][:, i * V7X_LANES:(i + 1) * V7X_LANES] for g in n_grp for i, qj in enumerate(groups[g])}
        y = jnp.concatenate([jnp.concatenate([piece[(q, j)] for j in range(width // V7X_LANES)], axis=1)
                             for q in range(nseq)], axis=0)

        inv_e = 1.0 / HEAD_DIM
        mu = head_sum(y) * inv_e
        yc = y - mu
        var = head_sum(yc * yc) * inv_e
        yn = yc * lax.rsqrt(var + GN_EPS) * lnw_ref[...] + lnb_ref[...]
        out = ((yn + bonus) * gate).astype(o_ref.dtype)
        for q in range(nseq):
            o_ref[q, sl, :] = out[rsl[q], :]
        return carry

    lax.fori_loop(0, tr // c_len, chunk, 0)


def _rwkv(z, batch, seq, prm):
    t, zin = z.shape
    width = prm["w0"].shape[-1]
    tr = min(RWKV_BLOCK, seq)
    nblk = seq // tr
    nseq = math.gcd(batch, RWKV_SEQS)
    ncol, per = width // V7X_LANES, RWKV_GROUP_LANES // V7X_LANES
    n_groups = nseq * (ncol // per) + -(-(nseq * (ncol % per)) // per)
    row = lambda x: x.reshape(1, -1).astype(F32)
    w2p = jnp.concatenate([prm["w2"], jnp.zeros((AAA_LORA, width), F32)], axis=0)
    a2p = jnp.concatenate([jnp.zeros((DECAY_LORA, width), F32), prm["a2"]], axis=0)
    hid = jnp.arange(V7X_LANES) // HEAD_DIM
    seg = (hid[:, None] == hid[None, :]).astype(BF16)
    pos = jnp.arange(math.gcd(nseq * RWKV_CHUNK, V7X_MXU_DIM))
    tri = ((pos[:, None] >= pos[None, :])
           & (pos[:, None] // RWKV_CHUNK == pos[None, :] // RWKV_CHUNK)).astype(BF16)
    params = [row(prm["mix"]), row(prm["w0"]), w2p, row(prm["a0"]), a2p.astype(BF16), prm["g2"].astype(BF16),
              row(prm["k_k"]), row(prm["k_a"]), row(prm["r_k"]), row(prm["ln_w"]), row(prm["ln_b"]),
              seg, tri]
    out = pl.pallas_call(
        functools.partial(_rwkv_body, tr=tr, width=width),
        grid=(batch // nseq, nblk),
        in_specs=[pl.BlockSpec((nseq, tr, zin), lambda b, n: (b, n, 0))]
                 + [_resident(p.shape) for p in params],
        out_specs=pl.BlockSpec((nseq, tr, width), lambda b, n: (b, n, 0)),
        out_shape=jax.ShapeDtypeStruct((batch, seq, width), BF16),
        scratch_shapes=[pltpu.VMEM((nseq, V7X_SUBLANES, zin), F32), pltpu.VMEM((nseq, tr, zin), F32),
                        pltpu.VMEM((n_groups, HEAD_DIM, RWKV_GROUP_LANES), F32)],
        compiler_params=_cparams("arbitrary", "arbitrary"),
        name="rwkv7_time_mix",
    )(z.reshape(batch, seq, zin), *params)
    return out.reshape(t, width)


def _ssm_prep_body(are_ref, aim_ref, lstep_ref, bre_ref, bim_ref, cre_ref, cim_ref,
                   kt_ref, wre_ref, wim_ref, vre_ref, vim_ref, pre_ref, pim_ref):
    lam_re = are_ref[0]
    lam_im = aim_ref[0]
    step = jnp.exp(lstep_ref[0])
    mag = jnp.exp(lam_re * step)
    ang = lam_im * step
    abar_re, abar_im = mag * jnp.cos(ang), mag * jnp.sin(ang)
    inv = 1.0 / (lam_re * lam_re + lam_im * lam_im)
    f_re = ((abar_re - 1.0) * lam_re + abar_im * lam_im) * inv
    f_im = (abar_im * lam_re - (abar_re - 1.0) * lam_im) * inv
    b_re, b_im = bre_ref[0], bim_ref[0]
    bbar_re = f_re * b_re - f_im * b_im
    bbar_im = f_re * b_im + f_im * b_re
    c_re, c_im = cre_ref[0], cim_ref[0]

    def power(tau):
        m = jnp.exp(lam_re * step * tau)
        return m * jnp.cos(ang * tau), m * jnp.sin(ang * tau)

    nt = (((1,), (1,)), ((), ()))
    for tau in range(SSM_CHUNK):
        p_re, p_im = power(float(tau))
        cp_re = c_re * p_re - c_im * p_im
        cp_im = c_re * p_im + c_im * p_re
        kt_ref[0, tau] = _dot_hi(cp_re, bbar_re, nt) - _dot_hi(cp_im, bbar_im, nt)
        q_re, q_im = power(float(SSM_CHUNK - 1 - tau))
        wre_ref[0, tau] = q_re * bbar_re - q_im * bbar_im
        wim_ref[0, tau] = q_re * bbar_im + q_im * bbar_re
        s_re, s_im = power(float(tau + 1))
        vre_ref[0, tau] = c_re * s_re - c_im * s_im
        vim_ref[0, tau] = -(c_re * s_im + c_im * s_re)
    for lvl in range(SSM_SCAN_LEVELS):
        p_re, p_im = power(float(SSM_CHUNK * 2 ** lvl))
        pre_ref[0, lvl:lvl + 1, :] = p_re
        pim_ref[0, lvl:lvl + 1, :] = p_im


def _ssm_prep(a_re, a_im, log_step, b_re, b_im, c_re, c_im):
    g, p = a_re.shape
    c = SSM_GROUP_CH
    l = SSM_CHUNK
    grp3 = lambda shape: pl.BlockSpec((1,) + shape, lambda i: (i, 0, 0))
    grp4 = lambda shape: pl.BlockSpec((1,) + shape, lambda i: (i, 0, 0, 0))
    sds = jax.ShapeDtypeStruct
    return pl.pallas_call(
        _ssm_prep_body,
        grid=(g,),
        in_specs=[grp3((1, p)), grp3((1, p)), grp3((1, 1)), grp3((c, p)), grp3((c, p)),
                  grp3((c, p)), grp3((c, p))],
        out_specs=[grp4((l, c, c)), grp4((l, c, p)), grp4((l, c, p)), grp4((l, c, p)), grp4((l, c, p)),
                   grp3((SSM_SCAN_LEVELS, p)), grp3((SSM_SCAN_LEVELS, p))],
        out_shape=[sds((g, l, c, c), F32), sds((g, l, c, p), F32), sds((g, l, c, p), F32),
                   sds((g, l, c, p), F32), sds((g, l, c, p), F32),
                   sds((g, SSM_SCAN_LEVELS, p), F32), sds((g, SSM_SCAN_LEVELS, p), F32)],
        compiler_params=_cparams("arbitrary"),
        name="ssm_discretise",
    )(a_re.reshape(g, 1, p), a_im.reshape(g, 1, p), log_step.reshape(g, 1, 1),
      jnp.swapaxes(b_re, 1, 2), jnp.swapaxes(b_im, 1, 2), c_re, c_im)


def _ssm_scan_body(u_ref, kc_ref, wc_ref, vc_ref, ek_ref, ew_ref, ev_ref, mre_ref, mim_ref, d_ref, y_ref,
                   kd_s, wd_s, vd_s, *, n_chunks):
    l = SSM_CHUNK
    n_half, _, lanes = u_ref.shape
    gh = lanes // SSM_GROUP_CH

    @pl.when(pl.program_id(0) == 0)
    def _():
        def expand(src, e_ref, row_group, lane_group):
            out = jnp.dot(src, e_ref[...], preferred_element_type=F32)
            rg = row_group(lax.broadcasted_iota(jnp.int32, out.shape, 0))
            lg = lane_group(lax.broadcasted_iota(jnp.int32, out.shape, 1))
            return jnp.where(rg == lg, out, 0.0).astype(BF16)

        rows = 512
        for h in range(n_half):
            kd_s[h, 0:lanes, 0:lanes] = jnp.zeros((lanes, lanes), BF16)
            for r0 in range(0, l * lanes, rows):
                blk = expand(kc_ref[h, r0:r0 + rows, :], ek_ref,
                             lambda r: (r // SSM_GROUP_CH) % gh, lambda c: c // SSM_GROUP_CH)
                kd_s[h, r0:r0 + rows, lanes:2 * lanes] = blk
                hi = min(r0 + lanes + rows, l * lanes)
                kd_s[h, r0 + lanes:hi, 0:lanes] = blk[0:hi - r0 - lanes]
            for ri in range(2):
                half = wd_s.shape[2] // 2
                for r0 in range(0, l * lanes, rows):
                    wd_s[h, r0:r0 + rows, ri * half:(ri + 1) * half] = expand(
                        wc_ref[h, ri, r0:r0 + rows, :], ew_ref,
                        lambda r: (r // SSM_GROUP_CH) % gh, lambda c: c // SSM_STATE)
                hv = vd_s.shape[1] // 2
                vd_s[h, ri * hv:(ri + 1) * hv, :] = expand(
                    vc_ref[h, ri], ev_ref, lambda r: r // SSM_STATE, lambda c: (c // SSM_GROUP_CH) % gh)

    row = lax.broadcasted_iota(jnp.int32, (n_chunks, 1), 0)
    halves = range(n_half)
    x32 = [[u_ref[h, pl.ds(j, n_chunks, stride=l), :] for j in range(l)] for h in halves]
    xb = [[x.astype(BF16) for x in x32[h]] for h in halves]
    z = [jnp.dot(jnp.concatenate(xb[h], axis=1), wd_s[h], preferred_element_type=F32) for h in halves]
    pairs = [[jnp.dot(jnp.concatenate([xb[h][i + 1 - tau] for tau in range(i + 2)], axis=1),
                      kd_s[h, 0:(i + 2) * lanes, :], preferred_element_type=F32) for i in range(0, l, 2)]
             for h in halves]
    y_state = []
    for h in halves:
        half = z[h].shape[1] // 2
        z_re, z_im = z[h][:, :half], z[h][:, half:]
        for lvl in range(SSM_SCAN_LEVELS):
            s = 2 ** lvl
            if s >= n_chunks:
                break
            s_re = jnp.where(row >= s, pltpu.roll(z_re, s, 0), 0.0)
            s_im = jnp.where(row >= s, pltpu.roll(z_im, s, 0), 0.0)
            a_re = mre_ref[h, lvl:lvl + 1, :]
            a_im = mim_ref[h, lvl:lvl + 1, :]
            z_re, z_im = z_re + s_re * a_re - s_im * a_im, z_im + s_re * a_im + s_im * a_re
        zin = jnp.concatenate([jnp.where(row >= 1, pltpu.roll(z_re, 1, 0), 0.0),
                               jnp.where(row >= 1, pltpu.roll(z_im, 1, 0), 0.0)], axis=1)
        y_state.append(jnp.dot(zin.astype(BF16), vd_s[h], preferred_element_type=F32))
    for h in halves:
        dvec = d_ref[h]
        for i in range(l):
            y = (pairs[h][i // 2][:, (i % 2) * lanes:(i % 2 + 1) * lanes]
                 + y_state[h][:, i * lanes:(i + 1) * lanes] + x32[h][i] * dvec)
            y_ref[h, pl.ds(i, n_chunks, stride=l), :] = y


def _ssm_scan(u, tables, batch, seq):
    n_half, t, lanes = u.shape
    l, p = SSM_CHUNK, SSM_STATE
    gh = lanes // SSM_GROUP_CH
    tok = pl.BlockSpec((n_half, seq, lanes), lambda b: (0, b, 0))
    return pl.pallas_call(
        functools.partial(_ssm_scan_body, n_chunks=seq // l),
        grid=(batch,),
        in_specs=[tok] + [_resident(w.shape) for w in tables],
        out_specs=tok,
        out_shape=jax.ShapeDtypeStruct(u.shape, F32),
        scratch_shapes=[pltpu.VMEM((n_half, l * lanes, 2 * lanes), BF16),
                        pltpu.VMEM((n_half, l * lanes, 2 * gh * p), BF16),
                        pltpu.VMEM((n_half, 2 * gh * p, l * lanes), BF16)],
        compiler_params=_cparams("arbitrary"),
        name="ssm_chunk_scan",
    )(u, *tables)


def _ssm(u_slabs, batch, seq, prm):
    n_half, t, lanes = u_slabs.shape
    c, l, p = SSM_GROUP_CH, SSM_CHUNK, SSM_STATE
    gh = lanes // c
    kt, wre, wim, vre, vim, pre, pim = _ssm_prep(prm["a_re"], prm["a_im"], prm["log_step"],
                                                 prm["b_re"], prm["b_im"], prm["c_re"], prm["c_im"])
    kc = jnp.transpose(kt.reshape(n_half, gh, l, c, c), (0, 2, 1, 4, 3)).reshape(n_half, l * lanes, c)
    wc = jnp.transpose(jnp.stack([wre, wim]).reshape(2, n_half, gh, l, c, p), (1, 0, 3, 2, 4, 5))
    wc = wc.reshape(n_half, 2, l * lanes, p)
    vc = jnp.transpose(jnp.stack([vre, vim]).reshape(2, n_half, gh, l, c, p), (1, 0, 2, 5, 3, 4))
    vc = vc.reshape(n_half, 2, gh * p, l * c)
    ek = jnp.tile(jnp.eye(c, dtype=BF16), (1, gh))
    ew = jnp.tile(jnp.eye(p, dtype=BF16), (1, gh))
    ev = jnp.broadcast_to(jnp.einsum('ij,cd->icjd', jnp.eye(l, dtype=BF16), jnp.eye(c, dtype=BF16))[:, :, :, None, :],
                          (l, c, l, gh, c)).reshape(l * c, l * lanes)
    lv = SSM_SCAN_LEVELS
    m_re = jnp.transpose(pre.reshape(n_half, gh, lv, p), (0, 2, 1, 3)).reshape(n_half, lv, gh * p)
    m_im = jnp.transpose(pim.reshape(n_half, gh, lv, p), (0, 2, 1, 3)).reshape(n_half, lv, gh * p)
    dvec = prm["d"].reshape(n_half, 1, lanes).astype(F32)
    tables = [kc.astype(BF16), wc.astype(BF16), vc.astype(BF16), ek, ew, ev, m_re, m_im, dvec]
    return _ssm_scan(u_slabs, tables, batch, seq)


def _mix_ffn_body(oa_ref, or_ref, ys_ref, gate_ref, h_ref, wa_ref, wr_ref, ws_ref, wv_ref, wg_ref, wo_ref,
                  gain_ref, wgu_ref, wd_ref, fgain_ref, out_ref, *, hidden, final_norm):
    d = h_ref.shape[1]
    y = jnp.concatenate([ys_ref[i] for i in range(ys_ref.shape[0])], axis=1)
    zg = (0.5 * y * (1.0 + jnp.tanh(math.sqrt(2.0 / math.pi) * (y + 0.044715 * (y * y * y))))).astype(BF16)
    o_ssm = (jnp.dot(zg, wv_ref[...], preferred_element_type=F32)
             * _sigmoid(jnp.dot(zg, wg_ref[...], preferred_element_type=F32)))
    gate = lambda i: gate_ref[:, i * d:(i + 1) * d].astype(F32)
    merged = (gate(0) * _dot(oa_ref[...], wa_ref[...])
              + gate(1) * _dot(or_ref[...], wr_ref[...])
              + gate(2) * _dot(o_ssm, ws_ref[...]))
    x = h_ref[...] + _dot(merged, wo_ref[...])

    u = _rmsnorm(x, gain_ref[...]).astype(BF16)
    acc = jnp.zeros(x.shape, F32)
    for c in range(hidden // FFN_CHUNK):
        lo = c * FFN_CHUNK
        a = jnp.dot(u, wgu_ref[:, lo:lo + FFN_CHUNK], preferred_element_type=F32)
        b = jnp.dot(u, wgu_ref[:, hidden + lo:hidden + lo + FFN_CHUNK], preferred_element_type=F32)
        hm = (a * _sigmoid(a) * b).astype(BF16)
        acc = acc + jnp.dot(hm, wd_ref[lo:lo + FFN_CHUNK, :], preferred_element_type=F32)
    x = x + acc
    if final_norm:
        x = _rmsnorm(x, fgain_ref[...])
    out_ref[...] = x


def _mix_ffn(o_attn, o_rwkv, y_ssm, gates, h, mix_stacks, gain, wgu_stack, wd_stack, final_gain, final_norm, layer):
    t, d = h.shape
    tm = ROW_TILE
    rows = lambda a: pl.BlockSpec((tm, a.shape[1]), lambda i: (i, 0))
    slabs = lambda a: pl.BlockSpec((a.shape[0], tm, a.shape[2]), lambda i: (0, i, 0))
    stacked = lambda w: _resident_layer(w, layer)
    whole = lambda w: _resident(w.shape)
    weights = list(mix_stacks) + [gain.reshape(1, d), wgu_stack, wd_stack, final_gain.reshape(1, d)]
    specs = [stacked(w) for w in mix_stacks] + [whole(weights[-4]), stacked(wgu_stack), stacked(wd_stack),
                                                whole(weights[-1])]
    return pl.pallas_call(
        functools.partial(_mix_ffn_body, hidden=wd_stack.shape[1], final_norm=final_norm),
        grid=(t // tm,),
        in_specs=[rows(o_attn), rows(o_rwkv), slabs(y_ssm), rows(gates), rows(h)] + specs,
        out_specs=rows(h),
        out_shape=jax.ShapeDtypeStruct((t, d), F32),
        compiler_params=_cparams("arbitrary"),
        name="merge_ffn",
    )(o_attn, o_rwkv, y_ssm, gates, h, *weights)


def kernel(x, norm_mix, w_in, rwkv_shift_mix, rwkv_w0, rwkv_w2, rwkv_a0, rwkv_a2, rwkv_g2, rwkv_k_k, rwkv_k_a, rwkv_r_k, rwkv_ln_w, rwkv_ln_b, ssm_a_re, ssm_a_im, ssm_log_step, ssm_b_re, ssm_b_im, ssm_c_re, ssm_c_im, ssm_d, ssm_glu_val, ssm_glu_gate, w_branch_attn, w_branch_rwkv, w_branch_ssm, w_out, norm_ffn, ffn_w_gate_up, ffn_w_down, norm_final):
    batch, seq, d = x.shape
    depth = w_in.shape[0]
    attn_w = w_branch_attn.shape[1]
    rwkv_in = rwkv_shift_mix.shape[1]
    ssm_w = w_branch_ssm.shape[1]
    widths = (3 * attn_w, rwkv_in, ssm_w, N_BRANCH * d)
    assert sum(widths) == w_in.shape[2]
    assert attn_w == len(ATTN_GROUPS) * HEADS_PER_GROUP * HEAD_DIM
    assert seq % ATTN_BLOCK == 0 or seq in (w for w, _ in ATTN_GROUPS)
    assert seq // SSM_CHUNK <= 2 ** SSM_SCAN_LEVELS and (batch * seq) % ROW_TILE == 0

    bf = lambda w: w.astype(BF16)
    w_in_b, wgu_b, wd_b = bf(w_in), bf(ffn_w_gate_up), bf(ffn_w_down)
    mix_b = [bf(w) for w in (w_branch_attn, w_branch_rwkv, w_branch_ssm, ssm_glu_val, ssm_glu_gate, w_out)]
    h = x.reshape(batch * seq, d)
    for l in range(depth):
        qkv, z_rwkv, u_ssm, gates = _in_proj(h, norm_mix[l], w_in_b, l, widths)
        o_attn = _attention(qkv, batch, seq)
        o_rwkv = _rwkv(z_rwkv, batch, seq, dict(
            mix=rwkv_shift_mix[l], w0=rwkv_w0[l], w2=rwkv_w2[l], a0=rwkv_a0[l], a2=rwkv_a2[l],
            g2=rwkv_g2[l], k_k=rwkv_k_k[l], k_a=rwkv_k_a[l], r_k=rwkv_r_k[l],
            ln_w=rwkv_ln_w[l], ln_b=rwkv_ln_b[l]))
        y_ssm = _ssm(u_ssm, batch, seq, dict(
            a_re=ssm_a_re[l], a_im=ssm_a_im[l], log_step=ssm_log_step[l], b_re=ssm_b_re[l],
            b_im=ssm_b_im[l], c_re=ssm_c_re[l], c_im=ssm_c_im[l], d=ssm_d[l]))
        h = _mix_ffn(o_attn, o_rwkv, y_ssm, gates, h, mix_b, norm_ffn[l], wgu_b, wd_b, norm_final,
                     l == depth - 1, l)
    return h.reshape(batch, seq, d)
```
